```python
import math
import jax
import jax.numpy as jnp
from jax import lax
import numpy as np

D_MODEL = 2048
BATCH = 2
SEQ = 4096
DEPTH = 4

N_MIXERS = 3
D_INNER = D_MODEL
CHUNK = 128
SG_GROUPS = 16
SG_GROUP_DIM = D_INNER // SG_GROUPS
SG_COLS = 3 * D_INNER
HEAD_DIM = 64
SWA_HEADS = D_INNER // HEAD_DIM
SWA_KV_HEADS = SWA_HEADS // 8
SWA_REP = SWA_HEADS // SWA_KV_HEADS
WINDOW = 128
BLOCK = 128
ROPE_THETA = 10000.0
SWA_COLS = 2 * D_INNER + 2 * SWA_KV_HEADS * HEAD_DIM
RWKV_HEAD_DIM = 64
RWKV_HEADS = D_INNER // RWKV_HEAD_DIM
DECAY_LORA = 96
AAA_LORA = 96
RWKV_COLS = 4 * D_INNER + DECAY_LORA + AAA_LORA
DECAY_SCALE = math.exp(-0.5)
GN_EPS = 64e-5
RMS_EPS = 1e-6
LN_EPS = 1e-5
N_A = (DEPTH + 2) // 3
N_B = (DEPTH + 1) // 3
N_C = DEPTH // 3

kernel_name = 'hybrid_sgmlp_swa_rwkv7_adaln'


def rms_norm(x, g):
    xf = x.astype(jnp.float32)
    y = xf * lax.rsqrt(jnp.mean(xf * xf, axis=-1, keepdims=True) + RMS_EPS)
    return (y * g.astype(jnp.float32)).astype(x.dtype)


def token_shift(t):
    return jnp.concatenate([jnp.zeros_like(t[:, :1]), t[:, :-1]], axis=1)


def rope(x, positions):
    half = HEAD_DIM // 2
    inv_freq = ROPE_THETA ** (-jnp.arange(half, dtype=jnp.float32) / half)
    ang = positions.astype(jnp.float32)[..., None] * inv_freq
    cos = jnp.cos(ang)[:, :, None, :]
    sin = jnp.sin(ang)[:, :, None, :]
    xf = x.astype(jnp.float32)
    x1, x2 = xf[..., :half], xf[..., half:]
    return jnp.concatenate([x1 * cos - x2 * sin, x2 * cos + x1 * sin], axis=-1).astype(x.dtype)


def chunked_spatial_gating(p, ln_g, ln_b, w_s, b_s):
    B, T, _ = p.shape
    u, v, z = jnp.split(p, 3, axis=-1)
    u = jax.nn.gelu(u)
    vf = jax.nn.gelu(v).astype(jnp.float32)
    mean = jnp.mean(vf, axis=-1, keepdims=True)
    var = jnp.mean(jnp.square(vf - mean), axis=-1, keepdims=True)
    v = ((vf - mean) * lax.rsqrt(var + LN_EPS) * ln_g.astype(jnp.float32) + ln_b.astype(jnp.float32)).astype(p.dtype)
    nc = T // CHUNK
    v = v.reshape(B, nc, CHUNK, SG_GROUPS, SG_GROUP_DIM)
    causal = jnp.tril(jnp.ones((CHUNK, CHUNK), dtype=bool))
    w = jnp.where(causal[None], w_s, jnp.zeros_like(w_s))
    f = jnp.einsum('gts,bnsgc->bntgc', w, v) + b_s.T[:, :, None]
    f = f.reshape(B, T, D_INNER)
    return u * f * jax.nn.silu(z)


def sliding_window_attention(p, positions, sinks):
    B, T, _ = p.shape
    kvw = SWA_KV_HEADS * HEAD_DIM
    q, k, v, z = jnp.split(p, [D_INNER, D_INNER + kvw, D_INNER + 2 * kvw], axis=-1)
    q = rope(q.reshape(B, T, SWA_HEADS, HEAD_DIM), positions)
    k = rope(k.reshape(B, T, SWA_KV_HEADS, HEAD_DIM), positions)
    v = v.reshape(B, T, SWA_KV_HEADS, HEAD_DIM)
    nb = T // BLOCK
    qb = q.reshape(B, nb, BLOCK, SWA_KV_HEADS, SWA_REP, HEAD_DIM)

    def with_prev(t):
        tb = t.reshape(B, nb, BLOCK, SWA_KV_HEADS, HEAD_DIM)
        prev = jnp.concatenate([jnp.zeros_like(tb[:, :1]), tb[:, :-1]], axis=1)
        return jnp.concatenate([prev, tb], axis=2)

    kb, vb = with_prev(k), with_prev(v)
    s = jnp.einsum('bnqgrd,bnkgd->bngrqk', qb, kb,
                   preferred_element_type=jnp.float32) * (HEAD_DIM ** -0.5)
    qi = jnp.arange(BLOCK)[:, None]
    kj = jnp.arange(2 * BLOCK)[None, :]
    rel = qi + BLOCK - kj
    band = (rel >= 0) & (rel < WINDOW)
    key_pos = jnp.arange(nb)[:, None] * BLOCK + jnp.arange(2 * BLOCK)[None, :] - BLOCK
    mask = band[None] & (key_pos >= 0)[:, None, :]
    s = jnp.where(mask[None, :, None, None], s, -jnp.inf)
    sink = sinks.astype(jnp.float32).reshape(SWA_KV_HEADS, SWA_REP)[None, None, :, :, None, None]
    m = jnp.maximum(jnp.max(s, axis=-1, keepdims=True), sink)
    e = jnp.exp(s - m)
    denom = jnp.sum(e, axis=-1, keepdims=True) + jnp.exp(sink - m)
    prob = (e / denom).astype(p.dtype)
    o = jnp.einsum('bngrqk,bnkgd->bnqgrd', prob, vb).reshape(B, T, D_INNER)
    return o * jax.nn.silu(z)


def rwkv7_time_mix(p, mu, w0, w_lora, a0, a_lora, k_k, k_a, r_k, gn_g, gn_b):
    B, T, _ = p.shape
    H, N = RWKV_HEADS, RWKV_HEAD_DIM
    p = p + (token_shift(p) - p) * mu
    r, k, v, z, dw, da = jnp.split(p, [D_INNER, 2 * D_INNER, 3 * D_INNER, 4 * D_INNER,
                                       4 * D_INNER + DECAY_LORA], axis=-1)
    decay = jnp.exp(-DECAY_SCALE * jax.nn.sigmoid((w0 + jnp.tanh(dw) @ w_lora).astype(jnp.float32)))
    a = jax.nn.sigmoid((a0 + da @ a_lora).astype(jnp.float32))

    def heads(t):
        return t.astype(jnp.float32).reshape(B, T, H, N)

    r, k, v, decay, a = heads(r), heads(k), heads(v), heads(decay), heads(a)
    kk = k * k_k.astype(jnp.float32).reshape(H, N)
    kk = kk / jnp.maximum(jnp.sqrt(jnp.sum(kk * kk, axis=-1, keepdims=True)), 1e-12)
    k = k * (1.0 + (a - 1.0) * k_a.astype(jnp.float32).reshape(H, N))
    b_vec = kk * a

    def step(S, inp):
        r_t, w_t, k_t, v_t, kk_t, b_t = inp
        sa = jnp.einsum('bhvk,bhk->bhv', S, kk_t)
        S = S * w_t[:, :, None, :] - sa[..., None] * b_t[:, :, None, :] + v_t[..., None] * k_t[:, :, None, :]
        y = jnp.einsum('bhvk,bhk->bhv', S, r_t)
        return S, y

    xs = tuple(jnp.moveaxis(t, 1, 0) for t in (r, decay, k, v, kk, b_vec))
    S0 = jnp.zeros((B, H, N, N), jnp.float32)
    _, y = lax.scan(step, S0, xs)
    y = jnp.moveaxis(y, 0, 1)
    mean = jnp.mean(y, axis=-1, keepdims=True)
    var = jnp.mean(jnp.square(y - mean), axis=-1, keepdims=True)
    y = (y - mean) * lax.rsqrt(var + GN_EPS) * gn_g.astype(jnp.float32).reshape(H, N) \
        + gn_b.astype(jnp.float32).reshape(H, N)
    y = y + jnp.sum(r * k * r_k.astype(jnp.float32), axis=-1, keepdims=True) * v
    y = y.reshape(B, T, D_INNER).astype(p.dtype)
    return y * jax.nn.silu(z)


def setup_inputs(seed: int = 0) -> dict:
    key = jax.random.key(seed)
    keys = jax.random.split(key, 32)

    def nrm(i, shape, scale):
        return jax.random.normal(keys[i], shape, jnp.float32) * scale

    positions = jax.random.randint(keys[2], (BATCH, 1), 0, 1024, dtype=jnp.int32) \
        + jnp.arange(SEQ, dtype=jnp.int32)[None, :]
    return {
        'x': nrm(0, (BATCH, SEQ, D_MODEL), 1.0),
        'c': nrm(1, (BATCH, D_MODEL), 1.0),
        'positions': positions,
        'norm_g': 1.0 + nrm(3, (DEPTH, D_MODEL), 0.1),
        'mod_w': nrm(4, (DEPTH, D_MODEL, 3 * D_MODEL), 0.5 * D_MODEL ** -0.5),
        'mod_b': nrm(5, (DEPTH, 3 * D_MODEL), 0.02),
        'final_norm_g': 1.0 + nrm(6, (D_MODEL,), 0.1),
        'sg_w_in': nrm(7, (N_A, D_MODEL, SG_COLS), D_MODEL ** -0.5),
        'sg_w_out': nrm(8, (N_A, D_INNER, D_MODEL), D_INNER ** -0.5),
        'sg_ln_g': 1.0 + nrm(9, (N_A, D_INNER), 0.1),
        'sg_ln_b': nrm(10, (N_A, D_INNER), 0.02),
        'sg_w_spatial': nrm(11, (N_A, SG_GROUPS, CHUNK, CHUNK), CHUNK ** -0.5),
        'sg_b_spatial': 1.0 + nrm(12, (N_A, SG_GROUPS, CHUNK), 0.1),
        'swa_w_in': nrm(13, (N_B, D_MODEL, SWA_COLS), D_MODEL ** -0.5),
        'swa_w_out': nrm(14, (N_B, D_INNER, D_MODEL), D_INNER ** -0.5),
        'swa_sinks': nrm(15, (N_B, SWA_HEADS), 1.0),
        'rwkv_w_in': nrm(16, (N_C, D_MODEL, RWKV_COLS), D_MODEL ** -0.5),
        'rwkv_w_out': nrm(17, (N_C, D_INNER, D_MODEL), D_INNER ** -0.5),
        'rwkv_mu': jax.random.uniform(keys[18], (N_C, RWKV_COLS), jnp.float32),
        'rwkv_w0': jax.random.uniform(keys[19], (N_C, D_INNER), jnp.float32, -4.0, 1.0),
        'rwkv_w_lora': nrm(20, (N_C, DECAY_LORA, D_INNER), DECAY_LORA ** -0.5),
        'rwkv_a0': nrm(21, (N_C, D_INNER), 0.5),
        'rwkv_a_lora': nrm(22, (N_C, AAA_LORA, D_INNER), 0.5 * AAA_LORA ** -0.5),
        'rwkv_k_k': 0.85 + nrm(23, (N_C, D_INNER), 0.1),
        'rwkv_k_a': 1.0 + nrm(24, (N_C, D_INNER), 0.1),
        'rwkv_r_k': nrm(25, (N_C, RWKV_HEADS, RWKV_HEAD_DIM), 0.1),
        'rwkv_gn_g': 1.0 + nrm(26, (N_C, D_INNER), 0.1),
        'rwkv_gn_b': nrm(27, (N_C, D_INNER), 0.02),
    }


def reference(x, c, positions, norm_g, mod_w, mod_b, final_norm_g,
              sg_w_in, sg_w_out, sg_ln_g, sg_ln_b, sg_w_spatial, sg_b_spatial,
              swa_w_in, swa_w_out, swa_sinks,
              rwkv_w_in, rwkv_w_out, rwkv_mu, rwkv_w0, rwkv_w_lora, rwkv_a0, rwkv_a_lora,
              rwkv_k_k, rwkv_k_a, rwkv_r_k, rwkv_gn_g, rwkv_gn_b):
    cond = jax.nn.silu(c)
    for i in range(DEPTH):
        kind, j = i % N_MIXERS, i // N_MIXERS
        mod = (cond @ mod_w[i] + mod_b[i])[:, None, :]
        shift, scale, gate = jnp.split(mod, 3, axis=-1)
        h = rms_norm(x, norm_g[i]) * (1.0 + scale) + shift
        if kind == 0:
            y = chunked_spatial_gating(h @ sg_w_in[j], sg_ln_g[j], sg_ln_b[j],
                                       sg_w_spatial[j], sg_b_spatial[j]) @ sg_w_out[j]
        elif kind == 1:
            y = sliding_window_attention(h @ swa_w_in[j], positions, swa_sinks[j]) @ swa_w_out[j]
        else:
            y = rwkv7_time_mix(h @ rwkv_w_in[j], rwkv_mu[j], rwkv_w0[j], rwkv_w_lora[j],
                               rwkv_a0[j], rwkv_a_lora[j], rwkv_k_k[j], rwkv_k_a[j],
                               rwkv_r_k[j], rwkv_gn_g[j], rwkv_gn_b[j]) @ rwkv_w_out[j]
        x = x + gate * y
    return rms_norm(x, final_norm_g)
```

```python
import functools
import math

import jax
import jax.numpy as jnp
from jax import lax
from jax.experimental import pallas as pl
from jax.experimental.pallas import tpu as pltpu

F32 = jnp.float32
BF16 = jnp.bfloat16

D = 2048
SEQ = 4096
DEPTH = 4
HEAD = 64
LANES = 128
SG_CHUNK = 128
SG_GROUPS = 16
SWA_HEADS = D // HEAD
SWA_KV = SWA_HEADS // 8
SWA_KVW = SWA_KV * HEAD
SWA_BLOCK = 128
ROPE_THETA = 10000.0
RW_HEADS = D // HEAD
RW_LORA = 96
RW_LP = 256
RW_CHUNK = 64
RW_HG = 4
RW_GW = RW_HG * HEAD
RW_GROUPS = D // RW_GW
DECAY_SCALE = math.exp(-0.5)
GN_EPS = 64e-5
RMS_EPS = 1e-6
LN_EPS = 1e-5

VMEM_LIMIT = 48 * 1024 * 1024


def _cparams(*sem):
    return pltpu.CompilerParams(dimension_semantics=sem, vmem_limit_bytes=VMEM_LIMIT)


def _mod_kernel(c_ref, w_ref, b_ref, o_ref):
    cond = jax.nn.silu(c_ref[...]).astype(BF16)
    o_ref[0] = jnp.dot(cond, w_ref[0].astype(BF16), preferred_element_type=F32) + b_ref[0]


def _modulation(c8, mod_w, mod_b):
    tn = 1024
    n = mod_w.shape[2]
    return pl.pallas_call(
        _mod_kernel,
        out_shape=jax.ShapeDtypeStruct((DEPTH, 8, n), F32),
        grid=(DEPTH, n // tn),
        in_specs=[pl.BlockSpec((8, D), lambda l, j: (0, 0)),
                  pl.BlockSpec((1, D, tn), lambda l, j: (l, 0, j)),
                  pl.BlockSpec((1, 1, tn), lambda l, j: (l, 0, j))],
        out_specs=pl.BlockSpec((1, 8, tn), lambda l, j: (l, 0, j)),
        compiler_params=_cparams("parallel", "parallel"),
        name="modulation",
    )(c8, mod_w, mod_b.reshape(DEPTH, 1, n))


def _norm_mod(x, g, scale, shift):
    y = x * lax.rsqrt(jnp.mean(x * x, axis=-1, keepdims=True) + RMS_EPS) * g
    return y * (1.0 + scale) + shift


def _in_proj_kernel(x_ref, g_ref, sc_ref, sh_ref, w_ref, o_ref, h_scr):
    @pl.when(pl.program_id(1) == 0)
    def _():
        h_scr[...] = _norm_mod(x_ref[...], g_ref[...], sc_ref[0], sh_ref[0]).astype(BF16)

    o_ref[...] = jnp.dot(h_scr[...], w_ref[...], preferred_element_type=F32)


def _in_proj_lora_kernel(x_ref, g_ref, sc_ref, sh_ref, w_ref, wl_ref, o_ref, ol_ref, h_scr):
    @pl.when(pl.program_id(1) == 0)
    def _():
        h = _norm_mod(x_ref[...], g_ref[...], sc_ref[0], sh_ref[0]).astype(BF16)
        h_scr[...] = h
        ol_ref[...] = jnp.dot(h, wl_ref[...], preferred_element_type=F32)

    o_ref[...] = jnp.dot(h_scr[...], w_ref[...], preferred_element_type=F32)


def _in_proj(x2, g, mod3, w_bf, w_lora_bf=None, *, tm=1024, tn=512):
    n_tok = x2.shape[0]
    ncols = w_bf.shape[1]
    per_b = SEQ // tm
    x_spec = pl.BlockSpec((tm, D), lambda i, j: (i, 0))
    g_spec = pl.BlockSpec((1, D), lambda i, j: (0, 0))
    sh_spec = pl.BlockSpec((1, 1, D), lambda i, j: (i // per_b, 0, 0))
    sc_spec = pl.BlockSpec((1, 1, D), lambda i, j: (i // per_b, 0, 1))
    w_spec = pl.BlockSpec((D, tn), lambda i, j: (0, j))
    o_spec = pl.BlockSpec((tm, tn), lambda i, j: (i, j))
    grid = (n_tok // tm, ncols // tn)
    scratch = [pltpu.VMEM((tm, D), BF16)]
    if w_lora_bf is None:
        return pl.pallas_call(
            _in_proj_kernel,
            out_shape=jax.ShapeDtypeStruct((n_tok, ncols), F32),
            grid=grid,
            in_specs=[x_spec, g_spec, sc_spec, sh_spec, w_spec],
            out_specs=o_spec,
            scratch_shapes=scratch,
            compiler_params=_cparams("parallel", "arbitrary"),
            name="in_proj",
        )(x2, g, mod3, mod3, w_bf)
    nl = w_lora_bf.shape[1]
    return pl.pallas_call(
        _in_proj_lora_kernel,
        out_shape=(jax.ShapeDtypeStruct((n_tok, ncols), F32), jax.ShapeDtypeStruct((n_tok, nl), F32)),
        grid=grid,
        in_specs=[x_spec, g_spec, sc_spec, sh_spec, w_spec, pl.BlockSpec((D, nl), lambda i, j: (0, 0))],
        out_specs=(o_spec, pl.BlockSpec((tm, nl), lambda i, j: (i, 0))),
        scratch_shapes=scratch,
        compiler_params=_cparams("parallel", "arbitrary"),
        name="in_proj_lora",
    )(x2, g, mod3, mod3, w_bf, w_lora_bf)


def _out_proj_kernel(m_ref, w_ref, x_ref, gate_ref, o_ref):
    y = jnp.dot(m_ref[...], w_ref[...], preferred_element_type=F32)
    o_ref[...] = x_ref[...] + gate_ref[0] * y


def _out_proj_final_kernel(m_ref, w_ref, x_ref, gate_ref, fg_ref, o_ref):
    y = jnp.dot(m_ref[...], w_ref[...], preferred_element_type=F32)
    xn = x_ref[...] + gate_ref[0] * y
    o_ref[...] = xn * lax.rsqrt(jnp.mean(xn * xn, axis=-1, keepdims=True) + RMS_EPS) * fg_ref[...]


def _out_proj(mix, w_bf, x2, mod3, final_g=None, *, tm=512):
    n_tok = x2.shape[0]
    per_b = SEQ // tm
    in_specs = [pl.BlockSpec((tm, D), lambda i: (i, 0)),
                pl.BlockSpec((D, D), lambda i: (0, 0)),
                pl.BlockSpec((tm, D), lambda i: (i, 0)),
                pl.BlockSpec((1, 1, D), lambda i: (i // per_b, 0, 2))]
    args = [mix, w_bf, x2, mod3]
    kern = _out_proj_kernel
    if final_g is not None:
        in_specs.append(pl.BlockSpec((1, D), lambda i: (0, 0)))
        args.append(final_g)
        kern = _out_proj_final_kernel
    return pl.pallas_call(
        kern,
        out_shape=jax.ShapeDtypeStruct((n_tok, D), F32),
        grid=(n_tok // tm,),
        in_specs=in_specs,
        out_specs=pl.BlockSpec((tm, D), lambda i: (i, 0)),
        compiler_params=_cparams("parallel"),
        name="out_proj_final" if final_g is not None else "out_proj",
    )(*args)


def _sg_mix_kernel(u_ref, v_ref, z_ref, lng_ref, lnb_ref, ws_ref, bs_ref, o_ref, *, tm):
    ti = lax.broadcasted_iota(jnp.int32, (SG_CHUNK, SG_CHUNK), 0)
    si = lax.broadcasted_iota(jnp.int32, (SG_CHUNK, SG_CHUNK), 1)
    causal = ti >= si
    ws = [jnp.where(causal, ws_ref[g], 0.0).astype(BF16) for g in range(SG_GROUPS)]
    for c in range(tm // SG_CHUNK):
        rows = pl.ds(c * SG_CHUNK, SG_CHUNK)
        v = jax.nn.gelu(v_ref[rows, :])
        mean = jnp.mean(v, axis=-1, keepdims=True)
        d = v - mean
        var = jnp.mean(d * d, axis=-1, keepdims=True)
        vn = (d * lax.rsqrt(var + LN_EPS) * lng_ref[...] + lnb_ref[...]).astype(BF16)
        for g in range(SG_GROUPS):
            cols = pl.ds(g * LANES, LANES)
            f = jnp.dot(ws[g], vn[:, g * LANES:(g + 1) * LANES], preferred_element_type=F32) + bs_ref[g]
            u = jax.nn.gelu(u_ref[rows, cols])
            o_ref[rows, cols] = (u * f * jax.nn.silu(z_ref[rows, cols])).astype(BF16)


def _sg_mix(p, ln_g, ln_b, w_s, b_full, *, tm=256):
    n_tok = p.shape[0]
    return pl.pallas_call(
        functools.partial(_sg_mix_kernel, tm=tm),
        out_shape=jax.ShapeDtypeStruct((n_tok, D), BF16),
        grid=(n_tok // tm,),
        in_specs=[pl.BlockSpec((tm, D), lambda i: (i, 0)),
                  pl.BlockSpec((tm, D), lambda i: (i, 1)),
                  pl.BlockSpec((tm, D), lambda i: (i, 2)),
                  pl.BlockSpec((1, D), lambda i: (0, 0)),
                  pl.BlockSpec((1, D), lambda i: (0, 0)),
                  pl.BlockSpec((SG_GROUPS, SG_CHUNK, SG_CHUNK), lambda i: (0, 0, 0)),
                  pl.BlockSpec((SG_GROUPS, SG_CHUNK, LANES), lambda i: (0, 0, 0))],
        out_specs=pl.BlockSpec((tm, D), lambda i: (i, 0)),
        compiler_params=_cparams("parallel"),
        name="sg_mix",
    )(p, p, p, ln_g, ln_b, w_s, b_full)


def _rope_tables(pos_ref, invf_ref):
    ang = pos_ref[...].astype(F32) * invf_ref[...]
    lane = lax.broadcasted_iota(jnp.int32, ang.shape, 1)
    first = (lane % HEAD) < (HEAD // 2)
    return jnp.cos(ang), jnp.where(first, -jnp.sin(ang), jnp.sin(ang)), first


def _rope(x, cos, sin_signed, first):
    partner = jnp.where(first, pltpu.roll(x, LANES - HEAD // 2, 1), pltpu.roll(x, HEAD // 2, 1))
    return x * cos + partner * sin_signed


def _swa_mix_kernel(sink_ref, q_ref, z_ref, kv_ref, kvp_ref, pos_ref, posp_ref, invf_ref, o_ref):
    i = pl.program_id(0)
    first_key = jnp.where((i % (SEQ // SWA_BLOCK)) != 0, 0, SWA_BLOCK)
    cos, sin_s, first = _rope_tables(pos_ref, invf_ref)
    cosp, sinp_s, _ = _rope_tables(posp_ref, invf_ref)
    lane = lax.broadcasted_iota(jnp.int32, (2 * SWA_BLOCK, LANES), 1)
    low = lane < HEAD

    qi = lax.broadcasted_iota(jnp.int32, (SWA_BLOCK, 2 * SWA_BLOCK), 0)
    kj = lax.broadcasted_iota(jnp.int32, (SWA_BLOCK, 2 * SWA_BLOCK), 1)
    rel = qi + SWA_BLOCK - kj
    mask = (rel >= 0) & (rel < SWA_BLOCK) & (kj >= first_key)

    for kp in range(SWA_KV // 2):
        ksl = pl.ds(kp * LANES, LANES)
        vsl = pl.ds(SWA_KVW + kp * LANES, LANES)
        k2 = jnp.concatenate([_rope(kvp_ref[:, ksl], cosp, sinp_s, first),
                              _rope(kv_ref[:, ksl], cos, sin_s, first)], axis=0)
        v2 = jnp.concatenate([kvp_ref[:, vsl], kv_ref[:, vsl]], axis=0)
        k2r = pltpu.roll(k2, HEAD, 1)
        v2r = pltpu.roll(v2, HEAD, 1)
        for sub in range(2):
            g = 2 * kp + sub
            src_k, src_kr = (k2, k2r) if sub == 0 else (k2r, k2)
            src_v, src_vr = (v2, v2r) if sub == 0 else (v2r, v2)
            k_lo = jnp.where(low, src_k, 0.0).astype(BF16)
            k_hi = jnp.where(low, 0.0, src_kr).astype(BF16)
            v_lo = jnp.where(low, src_v, 0.0).astype(BF16)
            v_hi = jnp.where(low, 0.0, src_vr).astype(BF16)
            for pr in range(4):
                h0 = g * 8 + 2 * pr
                cols = pl.ds((h0 // 2) * LANES, LANES)
                q2 = (_rope(q_ref[:, cols], cos, sin_s, first) * (HEAD ** -0.5)).astype(BF16)
                acc = None
                for half, (kk, vv) in enumerate(((k_lo, v_lo), (k_hi, v_hi))):
                    s = lax.dot_general(q2, kk, (((1,), (1,)), ((), ())), preferred_element_type=F32)
                    s = jnp.where(mask, s, -jnp.inf)
                    sink = sink_ref[h0 + half]
                    m = jnp.maximum(jnp.max(s, axis=-1, keepdims=True), sink)
                    e = jnp.exp(s - m)
                    denom = jnp.sum(e, axis=-1, keepdims=True) + jnp.exp(sink - m)
                    prob = (e / denom).astype(BF16)
                    o = jnp.dot(prob, vv, preferred_element_type=F32)
                    acc = o if acc is None else acc + o
                o_ref[:, cols] = (acc * jax.nn.silu(z_ref[:, cols])).astype(BF16)


def _swa_mix(p, pos2, invf, sinks):
    n_tok = p.shape[0]
    tb = SWA_BLOCK
    kvb = (2 * D) // (2 * SWA_KVW)
    return pl.pallas_call(
        _swa_mix_kernel,
        out_shape=jax.ShapeDtypeStruct((n_tok, D), BF16),
        grid=(n_tok // tb,),
        in_specs=[pl.BlockSpec(memory_space=pltpu.SMEM),
                  pl.BlockSpec((tb, D), lambda i: (i, 0)),
                  pl.BlockSpec((tb, D), lambda i: (i, 1)),
                  pl.BlockSpec((tb, 2 * SWA_KVW), lambda i: (i, kvb)),
                  pl.BlockSpec((tb, 2 * SWA_KVW), lambda i: (jnp.maximum(i - 1, 0), kvb)),
                  pl.BlockSpec((tb, 1), lambda i: (i, 0)),
                  pl.BlockSpec((tb, 1), lambda i: (jnp.maximum(i - 1, 0), 0)),
                  pl.BlockSpec((1, LANES), lambda i: (0, 0))],
        out_specs=pl.BlockSpec((tb, D), lambda i: (i, 0)),
        compiler_params=_cparams("parallel"),
        name="swa_mix",
    )(sinks, p, p, p, p, pos2, pos2, invf)


def _split2(x):
    hi = x.astype(BF16)
    return hi, (x - hi.astype(F32)).astype(BF16)


def _head_sum(x, seg):
    hi, lo = _split2(x)
    return (jnp.dot(hi, seg, preferred_element_type=F32) + jnp.dot(lo, seg, preferred_element_type=F32))


def _bd(x, same_head):
    t = jnp.concatenate([x] * RW_HG, axis=0)
    return jnp.where(same_head, t, 0.0).astype(BF16)


def _diag_blocks(o, lane_head):
    acc = o[0:HEAD]
    for h in range(1, RW_HG):
        acc = jnp.where(lane_head == h, o[h * HEAD:(h + 1) * HEAD], acc)
    return acc


def _dot(a, b):
    return jnp.dot(a.astype(BF16), b.astype(BF16), preferred_element_type=F32)


def _rwkv_group(r, lw, cum, k, v, kap, bet, h0, masks):
    incl, strict, eye, same_head, lane_head = masks
    c = RW_CHUNK
    cref = cum[c // 2 - 1:c // 2]
    cend = cum[c - 1:c]
    r_true = r * jnp.exp(cum)
    kap_true = kap * jnp.exp(cum - lw)
    r_s = r * jnp.exp(cum - cref)
    kap_s = kap * jnp.exp(cum - lw - cref)
    ginv = jnp.exp(cref - cum)
    k_s = k * ginv
    bet_s = bet * ginv
    eend = jnp.exp(cend - cum)
    k_hat = k * eend
    bet_hat = bet * eend

    lhs = jnp.concatenate([r_s, kap_s], axis=0).astype(BF16)
    rhs = jnp.concatenate([_bd(k_s, same_head), _bd(bet_s, same_head)], axis=0)
    s = lax.dot_general(lhs, rhs, (((1,), (1,)), ((), ())), preferred_element_type=F32)
    a_rk = jnp.where(incl, s[:c, :RW_GW], 0.0)
    a_rb = jnp.where(incl, s[:c, RW_GW:], 0.0)
    a_kk = jnp.where(strict, s[c:, :RW_GW], 0.0)
    a_kb = jnp.where(strict, s[c:, RW_GW:], 0.0)

    x = eye - a_kb
    p = _dot(a_kb, _bd(a_kb, same_head))
    for _ in range(4):
        xp = _dot(jnp.concatenate([x, p], axis=0), _bd(p, same_head))
        x = x + xp[:c]
        p = xp[c:]
    t_inv = x + _dot(x, _bd(p, same_head))

    akv = _dot(a_kk, _bd(v, same_head))
    wu = -_dot(t_inv, jnp.concatenate([_bd(kap_true, same_head), _bd(akv, same_head)], axis=1))
    wt, ut = wu[:, :RW_GW], wu[:, RW_GW:]

    lhs2 = jnp.concatenate([k_hat, bet_hat], axis=0).astype(BF16)
    rhs2 = jnp.concatenate([jnp.concatenate([v, jnp.zeros_like(v)], axis=1),
                            jnp.concatenate([ut, wt], axis=1)], axis=0).astype(BF16)
    o2 = lax.dot_general(lhs2, rhs2, (((0,), (0,)), ((), ())), preferred_element_type=F32)
    n_cat = _diag_blocks(o2[:, :RW_GW], lane_head)
    m_cat = _diag_blocks(o2[:, RW_GW:], lane_head) + eye * jnp.exp(cend)

    y0 = _dot(jnp.concatenate([a_rk, a_rb], axis=1),
              jnp.concatenate([_bd(v, same_head), _bd(ut, same_head)], axis=0))
    r_hat = r_true + _dot(a_rb, _bd(wt, same_head))

    o3 = _dot(jnp.concatenate([m_cat, r_hat], axis=0), _bd(h0, same_head))
    return o3[c:] + y0, o3[:c] + n_cat


def _rwkv_mix_kernel(p_ref, pp_ref, pl_ref, plp_ref, mu_ref, mul_ref, w0_ref, a0_ref, wl_ref, al_ref,
                     kk_ref, ka_ref, rk_ref, gng_ref, gnb_ref, o_ref, h_scr):
    ci = pl.program_id(1)
    c = RW_CHUNK

    @pl.when(ci == 0)
    def _():
        h_scr[...] = jnp.zeros_like(h_scr)

    not_first = (ci != 0).astype(F32)
    row = lax.broadcasted_iota(jnp.int32, (c, 1), 0)

    def shifted(cur, prev_ref):
        prev_row = prev_ref[7:8, :] * not_first
        rolled = pltpu.roll(cur, 1, 0)
        return jnp.where(row == 0, prev_row, rolled)

    lcur = pl_ref[...]
    da = lcur + (shifted(lcur, plp_ref) - lcur) * mul_ref[...]
    dw = jnp.tanh(da)

    ti = lax.broadcasted_iota(jnp.int32, (c, RW_GW), 0)
    li = lax.broadcasted_iota(jnp.int32, (c, RW_GW), 1)
    si = li % HEAD
    lane_head = li // HEAD
    rb = lax.broadcasted_iota(jnp.int32, (RW_GW, RW_GW), 0) // HEAD
    cb = lax.broadcasted_iota(jnp.int32, (RW_GW, RW_GW), 1) // HEAD
    same_head = rb == cb
    seg = same_head.astype(BF16)
    masks = (ti >= si, ti > si, (ti == si).astype(F32), same_head, lane_head)
    tri = (lax.broadcasted_iota(jnp.int32, (c, c), 0) >= lax.broadcasted_iota(jnp.int32, (c, c), 1)).astype(BF16)

    for g in range(RW_GROUPS):
        gs = pl.ds(g * RW_GW, RW_GW)

        def lerp(off):
            cols = pl.ds(off + g * RW_GW, RW_GW)
            cur = p_ref[:, cols]
            return cur + (shifted(cur, pp_ref.at[:, cols]) - cur) * mu_ref[:, cols]

        r = lerp(0)
        k = lerp(D)
        v = lerp(2 * D)
        z = lerp(3 * D)
        lw = -DECAY_SCALE * jax.nn.sigmoid(w0_ref[:, gs] + _dot(dw, wl_ref[:, gs]))
        a = jax.nn.sigmoid(a0_ref[:, gs] + _dot(da, al_ref[:, gs]))
        kap = k * kk_ref[:, gs]
        kap = kap / jnp.maximum(jnp.sqrt(_head_sum(kap * kap, seg)), 1e-12)
        k = k * (1.0 + (a - 1.0) * ka_ref[:, gs])
        bet = kap * a
        lw_hi, lw_lo = _split2(lw)
        cum = jnp.dot(tri, lw_hi, preferred_element_type=F32) + jnp.dot(tri, lw_lo, preferred_element_type=F32)

        y, h_new = _rwkv_group(r, lw, cum, k, v, kap, bet, h_scr[g], masks)
        h_scr[g] = h_new

        mean = _head_sum(y, seg) * (1.0 / HEAD)
        dlt = y - mean
        var = _head_sum(dlt * dlt, seg) * (1.0 / HEAD)
        y = dlt * lax.rsqrt(var + GN_EPS) * gng_ref[:, gs] + gnb_ref[:, gs]
        y = y + _head_sum(r * k * rk_ref[:, gs], seg) * v
        o_ref[:, gs] = (y * jax.nn.silu(z)).astype(BF16)


def _rwkv_mix(p, p_lora, mu, mu_l, w0, a0, w_lora, a_lora, k_k, k_a, r_k, gn_g, gn_b):
    n_tok = p.shape[0]
    c = RW_CHUNK
    nc = SEQ // c
    ncol = p.shape[1]
    nl = p_lora.shape[1]
    row = lambda b, ci: (b * nc + ci, 0)
    prev = lambda b, ci: (jnp.maximum((b * nc + ci) * (c // 8) - 1, 0), 0)
    vec = pl.BlockSpec((1, D), lambda b, ci: (0, 0))
    return pl.pallas_call(
        _rwkv_mix_kernel,
        out_shape=jax.ShapeDtypeStruct((n_tok, D), BF16),
        grid=(n_tok // SEQ, nc),
        in_specs=[pl.BlockSpec((c, ncol), row),
                  pl.BlockSpec((8, ncol), prev),
                  pl.BlockSpec((c, nl), row),
                  pl.BlockSpec((8, nl), prev),
                  pl.BlockSpec((1, ncol), lambda b, ci: (0, 0)),
                  pl.BlockSpec((1, nl), lambda b, ci: (0, 0)),
                  vec, vec,
                  pl.BlockSpec((RW_LP, D), lambda b, ci: (0, 0)),
                  pl.BlockSpec((RW_LP, D), lambda b, ci: (0, 0)),
                  vec, vec, vec, vec, vec],
        out_specs=pl.BlockSpec((c, D), row),
        scratch_shapes=[pltpu.VMEM((RW_GROUPS, HEAD, RW_GW), F32)],
        compiler_params=_cparams("parallel", "arbitrary"),
        name="rwkv_mix",
    )(p, p, p_lora, p_lora, mu, mu_l, w0, a0, w_lora, a_lora, k_k, k_a, r_k, gn_g, gn_b)


def kernel(x, c, positions, norm_g, mod_w, mod_b, final_norm_g, sg_w_in, sg_w_out, sg_ln_g, sg_ln_b, sg_w_spatial,
           sg_b_spatial, swa_w_in, swa_w_out, swa_sinks, rwkv_w_in, rwkv_w_out, rwkv_mu, rwkv_w0, rwkv_w_lora,
           rwkv_a0, rwkv_a_lora, rwkv_k_k, rwkv_k_a, rwkv_r_k, rwkv_gn_g, rwkv_gn_b):
    batch, seq, d = x.shape
    assert (seq, d) == (SEQ, D) and norm_g.shape[0] == DEPTH
    n_tok = batch * seq
    x2 = x.reshape(n_tok, D)
    c8 = jnp.zeros((8, D), F32).at[:batch].set(c)
    mod = _modulation(c8, mod_w, mod_b)
    pos2 = positions.reshape(n_tok, 1)
    half = HEAD // 2
    inv_freq = ROPE_THETA ** (-jnp.arange(half, dtype=F32) / half)
    invf = jnp.tile(inv_freq, LANES // half).reshape(1, LANES)
    row = lambda a: a.reshape(1, -1)

    for i in range(DEPTH):
        kind, j = i % 3, i // 3
        mod3 = mod[i, :batch].reshape(batch, 1, 3 * D)
        g = row(norm_g[i])
        if kind == 0:
            p = _in_proj(x2, g, mod3, sg_w_in[j].astype(BF16))
            mix = _sg_mix(p, row(sg_ln_g[j]), row(sg_ln_b[j]), sg_w_spatial[j],
                          jnp.broadcast_to(sg_b_spatial[j][:, :, None], (SG_GROUPS, SG_CHUNK, LANES)))
            w_out = sg_w_out[j]
        elif kind == 1:
            w = swa_w_in[j]
            w = jnp.concatenate([w[:, :D], w[:, D + 2 * SWA_KVW:], w[:, D:D + 2 * SWA_KVW]], axis=1).astype(BF16)
            p = _in_proj(x2, g, mod3, w)
            mix = _swa_mix(p, pos2, invf, swa_sinks[j])
            w_out = swa_w_out[j]
        else:
            w = rwkv_w_in[j]
            lpad = RW_LP - 2 * RW_LORA
            w_l = jnp.pad(w[:, 4 * D:], ((0, 0), (0, lpad))).astype(BF16)
            p, p_lora = _in_proj(x2, g, mod3, w[:, :4 * D].astype(BF16), w_l)
            mu = rwkv_mu[j]
            wl_pad = jnp.pad(rwkv_w_lora[j], ((0, RW_LP - RW_LORA), (0, 0))).astype(BF16)
            al_pad = jnp.pad(rwkv_a_lora[j], ((RW_LORA, lpad), (0, 0))).astype(BF16)
            mix = _rwkv_mix(p, p_lora, row(mu[:4 * D]), row(jnp.pad(mu[4 * D:], (0, lpad))), row(rwkv_w0[j]),
                            row(rwkv_a0[j]), wl_pad, al_pad, row(rwkv_k_k[j]),
                            row(rwkv_k_a[j]), row(rwkv_r_k[j]), row(rwkv_gn_g[j]), row(rwkv_gn_b[j]))
            w_out = rwkv_w_out[j]
        x2 = _out_proj(mix, w_out.astype(BF16), x2, mod3, row(final_norm_g) if i == DEPTH - 1 else None)
    return x2.reshape(batch, seq, D)
```

```python
import functools
import math

import jax
import jax.numpy as jnp
from jax import lax
from jax.experimental import pallas as pl
from jax.experimental.pallas import tpu as pltpu

F32 = jnp.float32
BF16 = jnp.bfloat16

D = 2048
SEQ = 4096
DEPTH = 4
HEAD = 64
LANES = 128
SG_CHUNK = 128
SG_GROUPS = 16
SWA_HEADS = D // HEAD
SWA_KV = SWA_HEADS // 8
SWA_KVW = SWA_KV * HEAD
SWA_BLOCK = 128
ROPE_THETA = 10000.0
RW_HEADS = D // HEAD
RW_LORA = 96
RW_LP = 256
RW_CHUNK = 64
RW_HG = 4
RW_GW = RW_HG * HEAD
RW_GROUPS = D // RW_GW
DECAY_SCALE = math.exp(-0.5)
GN_EPS = 64e-5
RMS_EPS = 1e-6
LN_EPS = 1e-5

VMEM_LIMIT = 48 * 1024 * 1024


def _cparams(*sem):
    return pltpu.CompilerParams(dimension_semantics=sem, vmem_limit_bytes=VMEM_LIMIT)


def _mod_kernel(c_ref, w_ref, b_ref, o_ref):
    cond = jax.nn.silu(c_ref[...]).astype(BF16)
    o_ref[0] = jnp.dot(cond, w_ref[0].astype(BF16), preferred_element_type=F32) + b_ref[0]


def _modulation(c8, mod_w, mod_b):
    tn = 1024
    n = mod_w.shape[2]
    return pl.pallas_call(
        _mod_kernel,
        out_shape=jax.ShapeDtypeStruct((DEPTH, 8, n), F32),
        grid=(DEPTH, n // tn),
        in_specs=[pl.BlockSpec((8, D), lambda l, j: (0, 0)),
                  pl.BlockSpec((1, D, tn), lambda l, j: (l, 0, j)),
                  pl.BlockSpec((1, 1, tn), lambda l, j: (l, 0, j))],
        out_specs=pl.BlockSpec((1, 8, tn), lambda l, j: (l, 0, j)),
        compiler_params=_cparams("parallel", "parallel"),
        name="modulation",
    )(c8, mod_w, mod_b.reshape(DEPTH, 1, n))


NORM_ROWS = 128


def _row_rsqrt(x_ref, rs_scr):
    def body(i, carry):
        rows = pl.ds(pl.multiple_of(i * NORM_ROWS, NORM_ROWS), NORM_ROWS)
        x = x_ref[rows, :]
        rs_scr[rows, :] = lax.rsqrt(jnp.mean(x * x, axis=-1, keepdims=True) + RMS_EPS)
        return carry
    lax.fori_loop(0, x_ref.shape[0] // NORM_ROWS, body, 0)


def _modulated(x_ref, rs_scr, g_ref, sc_ref, sh_ref):
    gain = g_ref[...] * (1.0 + sc_ref[0])
    return (x_ref[...] * rs_scr[...] * gain + sh_ref[0]).astype(BF16)


def _in_proj_kernel(x_ref, g_ref, sc_ref, sh_ref, w_ref, o_ref, rs_scr):
    @pl.when(pl.program_id(1) == 0)
    def _():
        _row_rsqrt(x_ref, rs_scr)

    h = _modulated(x_ref, rs_scr, g_ref, sc_ref, sh_ref)
    o_ref[...] = jnp.dot(h, w_ref[...], preferred_element_type=F32)


def _in_proj_lora_kernel(x_ref, g_ref, sc_ref, sh_ref, w_ref, wl_ref, o_ref, ol_ref, rs_scr):
    @pl.when(pl.program_id(1) == 0)
    def _():
        _row_rsqrt(x_ref, rs_scr)
        ol_ref[...] = jnp.dot(_modulated(x_ref, rs_scr, g_ref, sc_ref, sh_ref), wl_ref[...],
                              preferred_element_type=F32)

    h = _modulated(x_ref, rs_scr, g_ref, sc_ref, sh_ref)
    o_ref[...] = jnp.dot(h, w_ref[...], preferred_element_type=F32)


def _in_proj(x2, g, mod3, w_bf, w_lora_bf=None, *, tm=1024):
    n_tok = x2.shape[0]
    ncols = w_bf.shape[1]
    tn = 1024 if ncols % 1024 == 0 else 1536
    assert ncols % tn == 0
    per_b = SEQ // tm
    x_spec = pl.BlockSpec((tm, D), lambda i, j: (i, 0))
    g_spec = pl.BlockSpec((1, D), lambda i, j: (0, 0))
    sh_spec = pl.BlockSpec((1, 1, D), lambda i, j: (i // per_b, 0, 0))
    sc_spec = pl.BlockSpec((1, 1, D), lambda i, j: (i // per_b, 0, 1))
    w_spec = pl.BlockSpec((D, tn), lambda i, j: (0, j))
    o_spec = pl.BlockSpec((tm, tn), lambda i, j: (i, j))
    grid = (n_tok // tm, ncols // tn)
    scratch = [pltpu.VMEM((tm, 1), F32)]
    if w_lora_bf is None:
        return pl.pallas_call(
            _in_proj_kernel,
            out_shape=jax.ShapeDtypeStruct((n_tok, ncols), F32),
            grid=grid,
            in_specs=[x_spec, g_spec, sc_spec, sh_spec, w_spec],
            out_specs=o_spec,
            scratch_shapes=scratch,
            compiler_params=_cparams("parallel", "arbitrary"),
            name="in_proj",
        )(x2, g, mod3, mod3, w_bf)
    nl = w_lora_bf.shape[1]
    return pl.pallas_call(
        _in_proj_lora_kernel,
        out_shape=(jax.ShapeDtypeStruct((n_tok, ncols), F32), jax.ShapeDtypeStruct((n_tok, nl), F32)),
        grid=grid,
        in_specs=[x_spec, g_spec, sc_spec, sh_spec, w_spec, pl.BlockSpec((D, nl), lambda i, j: (0, 0))],
        out_specs=(o_spec, pl.BlockSpec((tm, nl), lambda i, j: (i, 0))),
        scratch_shapes=scratch,
        compiler_params=_cparams("parallel", "arbitrary"),
        name="in_proj_lora",
    )(x2, g, mod3, mod3, w_bf, w_lora_bf)


def _out_proj_kernel(m_ref, w_ref, x_ref, gate_ref, o_ref):
    y = jnp.dot(m_ref[...], w_ref[...], preferred_element_type=F32)
    o_ref[...] = x_ref[...] + gate_ref[0] * y


def _out_proj_final_kernel(m_ref, w_ref, x_ref, gate_ref, fg_ref, o_ref):
    y = jnp.dot(m_ref[...], w_ref[...], preferred_element_type=F32)
    xn = x_ref[...] + gate_ref[0] * y
    o_ref[...] = xn * lax.rsqrt(jnp.mean(xn * xn, axis=-1, keepdims=True) + RMS_EPS) * fg_ref[...]


def _out_proj(mix, w_bf, x2, mod3, final_g=None, *, tm=512):
    n_tok = x2.shape[0]
    per_b = SEQ // tm
    in_specs = [pl.BlockSpec((tm, D), lambda i: (i, 0)),
                pl.BlockSpec((D, D), lambda i: (0, 0)),
                pl.BlockSpec((tm, D), lambda i: (i, 0)),
                pl.BlockSpec((1, 1, D), lambda i: (i // per_b, 0, 2))]
    args = [mix, w_bf, x2, mod3]
    kern = _out_proj_kernel
    if final_g is not None:
        in_specs.append(pl.BlockSpec((1, D), lambda i: (0, 0)))
        args.append(final_g)
        kern = _out_proj_final_kernel
    return pl.pallas_call(
        kern,
        out_shape=jax.ShapeDtypeStruct((n_tok, D), F32),
        grid=(n_tok // tm,),
        in_specs=in_specs,
        out_specs=pl.BlockSpec((tm, D), lambda i: (i, 0)),
        compiler_params=_cparams("parallel"),
        name="out_proj_final" if final_g is not None else "out_proj",
    )(*args)


def _sg_mix_kernel(u_ref, v_ref, z_ref, lng_ref, lnb_ref, ws_ref, bs_ref, o_ref, *, tm):
    ti = lax.broadcasted_iota(jnp.int32, (SG_CHUNK, SG_CHUNK), 0)
    si = lax.broadcasted_iota(jnp.int32, (SG_CHUNK, SG_CHUNK), 1)
    causal = ti >= si
    ws = [jnp.where(causal, ws_ref[g], 0.0).astype(BF16) for g in range(SG_GROUPS)]
    for c in range(tm // SG_CHUNK):
        rows = pl.ds(c * SG_CHUNK, SG_CHUNK)
        v = jax.nn.gelu(v_ref[rows, :])
        mean = jnp.mean(v, axis=-1, keepdims=True)
        d = v - mean
        var = jnp.mean(d * d, axis=-1, keepdims=True)
        vn = (d * lax.rsqrt(var + LN_EPS) * lng_ref[...] + lnb_ref[...]).astype(BF16)
        for g in range(SG_GROUPS):
            cols = pl.ds(g * LANES, LANES)
            f = jnp.dot(ws[g], vn[:, g * LANES:(g + 1) * LANES], preferred_element_type=F32) + bs_ref[g]
            u = jax.nn.gelu(u_ref[rows, cols])
            o_ref[rows, cols] = (u * f * jax.nn.silu(z_ref[rows, cols])).astype(BF16)


def _sg_mix(p, ln_g, ln_b, w_s, b_full, *, tm=256):
    n_tok = p.shape[0]
    return pl.pallas_call(
        functools.partial(_sg_mix_kernel, tm=tm),
        out_shape=jax.ShapeDtypeStruct((n_tok, D), BF16),
        grid=(n_tok // tm,),
        in_specs=[pl.BlockSpec((tm, D), lambda i: (i, 0)),
                  pl.BlockSpec((tm, D), lambda i: (i, 1)),
                  pl.BlockSpec((tm, D), lambda i: (i, 2)),
                  pl.BlockSpec((1, D), lambda i: (0, 0)),
                  pl.BlockSpec((1, D), lambda i: (0, 0)),
                  pl.BlockSpec((SG_GROUPS, SG_CHUNK, SG_CHUNK), lambda i: (0, 0, 0)),
                  pl.BlockSpec((SG_GROUPS, SG_CHUNK, LANES), lambda i: (0, 0, 0))],
        out_specs=pl.BlockSpec((tm, D), lambda i: (i, 0)),
        compiler_params=_cparams("parallel"),
        name="sg_mix",
    )(p, p, p, ln_g, ln_b, w_s, b_full)


def _rope_tables(pos_ref, invf_ref):
    ang = pos_ref[...].astype(F32) * invf_ref[...]
    lane = lax.broadcasted_iota(jnp.int32, ang.shape, 1)
    first = (lane % HEAD) < (HEAD // 2)
    return jnp.cos(ang), jnp.where(first, -jnp.sin(ang), jnp.sin(ang)), first


def _rope(x, cos, sin_signed, first):
    partner = jnp.where(first, pltpu.roll(x, LANES - HEAD // 2, 1), pltpu.roll(x, HEAD // 2, 1))
    return x * cos + partner * sin_signed


def _swa_mix_kernel(sink_ref, q_ref, z_ref, kv_ref, kvp_ref, pos_ref, posp_ref, invf_ref, o_ref):
    i = pl.program_id(0)
    first_key = jnp.where((i % (SEQ // SWA_BLOCK)) != 0, 0, SWA_BLOCK)
    cos, sin_s, first = _rope_tables(pos_ref, invf_ref)
    cosp, sinp_s, _ = _rope_tables(posp_ref, invf_ref)
    lane = lax.broadcasted_iota(jnp.int32, (2 * SWA_BLOCK, LANES), 1)
    low = lane < HEAD

    qi = lax.broadcasted_iota(jnp.int32, (SWA_BLOCK, 2 * SWA_BLOCK), 0)
    kj = lax.broadcasted_iota(jnp.int32, (SWA_BLOCK, 2 * SWA_BLOCK), 1)
    rel = qi + SWA_BLOCK - kj
    mask = (rel >= 0) & (rel < SWA_BLOCK) & (kj >= first_key)

    for kp in range(SWA_KV // 2):
        ksl = pl.ds(kp * LANES, LANES)
        vsl = pl.ds(SWA_KVW + kp * LANES, LANES)
        k2 = jnp.concatenate([_rope(kvp_ref[:, ksl], cosp, sinp_s, first),
                              _rope(kv_ref[:, ksl], cos, sin_s, first)], axis=0)
        v2 = jnp.concatenate([kvp_ref[:, vsl], kv_ref[:, vsl]], axis=0)
        k2r = pltpu.roll(k2, HEAD, 1)
        v2r = pltpu.roll(v2, HEAD, 1)
        for sub in range(2):
            g = 2 * kp + sub
            src_k, src_kr = (k2, k2r) if sub == 0 else (k2r, k2)
            src_v, src_vr = (v2, v2r) if sub == 0 else (v2r, v2)
            k_lo = jnp.where(low, src_k, 0.0).astype(BF16)
            k_hi = jnp.where(low, 0.0, src_kr).astype(BF16)
            v_lo = jnp.where(low, src_v, 0.0).astype(BF16)
            v_hi = jnp.where(low, 0.0, src_vr).astype(BF16)
            heads = [g * 8 + j for j in range(8)]
            cols = [pl.ds((g * 4 + pr) * LANES, LANES) for pr in range(4)]
            q2 = [(_rope(q_ref[:, cols[pr]], cos, sin_s, first) * (HEAD ** -0.5)).astype(BF16) for pr in range(4)]
            s = [lax.dot_general(q2[j // 2], k_hi if j % 2 else k_lo, (((1,), (1,)), ((), ())),
                                 preferred_element_type=F32) for j in range(8)]
            prob = []
            for j in range(8):
                sj = jnp.where(mask, s[j], -jnp.inf)
                sink = sink_ref[heads[j]]
                m = jnp.maximum(jnp.max(sj, axis=-1, keepdims=True), sink)
                e = jnp.exp(sj - m)
                denom = jnp.sum(e, axis=-1, keepdims=True) + jnp.exp(sink - m)
                prob.append((e * (1.0 / denom)).astype(BF16))
            o = [jnp.dot(prob[j], v_hi if j % 2 else v_lo, preferred_element_type=F32) for j in range(8)]
            for pr in range(4):
                o_ref[:, cols[pr]] = ((o[2 * pr] + o[2 * pr + 1]) * jax.nn.silu(z_ref[:, cols[pr]])).astype(BF16)


def _swa_mix(p, pos2, invf, sinks):
    n_tok = p.shape[0]
    tb = SWA_BLOCK
    kvb = (2 * D) // (2 * SWA_KVW)
    return pl.pallas_call(
        _swa_mix_kernel,
        out_shape=jax.ShapeDtypeStruct((n_tok, D), BF16),
        grid=(n_tok // tb,),
        in_specs=[pl.BlockSpec(memory_space=pltpu.SMEM),
                  pl.BlockSpec((tb, D), lambda i: (i, 0)),
                  pl.BlockSpec((tb, D), lambda i: (i, 1)),
                  pl.BlockSpec((tb, 2 * SWA_KVW), lambda i: (i, kvb)),
                  pl.BlockSpec((tb, 2 * SWA_KVW), lambda i: (jnp.maximum(i - 1, 0), kvb)),
                  pl.BlockSpec((tb, 1), lambda i: (i, 0)),
                  pl.BlockSpec((tb, 1), lambda i: (jnp.maximum(i - 1, 0), 0)),
                  pl.BlockSpec((1, LANES), lambda i: (0, 0))],
        out_specs=pl.BlockSpec((tb, D), lambda i: (i, 0)),
        compiler_params=_cparams("parallel"),
        name="swa_mix",
    )(sinks, p, p, p, p, pos2, pos2, invf)


def _split2(x):
    hi = x.astype(BF16)
    return hi, (x - hi.astype(F32)).astype(BF16)


def _head_sum(x, seg):
    hi, lo = _split2(x)
    return (jnp.dot(hi, seg, preferred_element_type=F32) + jnp.dot(lo, seg, preferred_element_type=F32))


def _bd(x, same_head):
    t = jnp.concatenate([x] * RW_HG, axis=0)
    return jnp.where(same_head, t, 0.0).astype(BF16)


def _diag_blocks(o, lane_head):
    acc = o[0:HEAD]
    for h in range(1, RW_HG):
        acc = jnp.where(lane_head == h, o[h * HEAD:(h + 1) * HEAD], acc)
    return acc


def _dot(a, b):
    return jnp.dot(a.astype(BF16), b.astype(BF16), preferred_element_type=F32)


def _rwkv_groups(r, lw, cum, k, v, kap, bet, h0, masks):
    incl, strict, eye, same_head, lane_head = masks
    c = RW_CHUNK
    gr = range(len(r))
    bd = lambda t: _bd(t, same_head)
    cref = [cum[g][c // 2 - 1:c // 2] for g in gr]
    cend = [cum[g][c - 1:c] for g in gr]
    r_true = [r[g] * jnp.exp(cum[g]) for g in gr]
    kap_true = [kap[g] * jnp.exp(cum[g] - lw[g]) for g in gr]
    r_s = [r[g] * jnp.exp(cum[g] - cref[g]) for g in gr]
    kap_s = [kap[g] * jnp.exp(cum[g] - lw[g] - cref[g]) for g in gr]
    ginv = [jnp.exp(cref[g] - cum[g]) for g in gr]
    eend = [jnp.exp(cend[g] - cum[g]) for g in gr]

    s = [lax.dot_general(jnp.concatenate([r_s[g], kap_s[g]], axis=0).astype(BF16),
                         jnp.concatenate([bd(k[g] * ginv[g]), bd(bet[g] * ginv[g])], axis=0),
                         (((1,), (1,)), ((), ())), preferred_element_type=F32) for g in gr]
    a_rk = [jnp.where(incl, s[g][:c, :RW_GW], 0.0) for g in gr]
    a_rb = [jnp.where(incl, s[g][:c, RW_GW:], 0.0) for g in gr]
    a_kk = [jnp.where(strict, s[g][c:, :RW_GW], 0.0) for g in gr]
    a_kb = [jnp.where(strict, s[g][c:, RW_GW:], 0.0) for g in gr]

    x = [eye - a_kb[g] for g in gr]
    p = [_dot(a_kb[g], bd(a_kb[g])) for g in gr]
    akv = [_dot(a_kk[g], bd(v[g])) for g in gr]
    for _ in range(4):
        xp = [_dot(jnp.concatenate([x[g], p[g]], axis=0), bd(p[g])) for g in gr]
        x = [x[g] + xp[g][:c] for g in gr]
        p = [xp[g][c:] for g in gr]
    t_inv = [x[g] + _dot(x[g], bd(p[g])) for g in gr]

    wu = [-_dot(t_inv[g], jnp.concatenate([bd(kap_true[g]), bd(akv[g])], axis=1)) for g in gr]
    wt = [wu[g][:, :RW_GW] for g in gr]
    ut = [wu[g][:, RW_GW:] for g in gr]

    o2 = [lax.dot_general(
        jnp.concatenate([k[g] * eend[g], bet[g] * eend[g]], axis=0).astype(BF16),
        jnp.concatenate([jnp.concatenate([v[g], jnp.zeros_like(v[g])], axis=1),
                         jnp.concatenate([ut[g], wt[g]], axis=1)], axis=0).astype(BF16),
        (((0,), (0,)), ((), ())), preferred_element_type=F32) for g in gr]
    n_cat = [_diag_blocks(o2[g][:, :RW_GW], lane_head) for g in gr]
    m_cat = [_diag_blocks(o2[g][:, RW_GW:], lane_head) + eye * jnp.exp(cend[g]) for g in gr]

    y0 = [_dot(jnp.concatenate([a_rk[g], a_rb[g]], axis=1),
               jnp.concatenate([bd(v[g]), bd(ut[g])], axis=0)) for g in gr]
    r_hat = [r_true[g] + _dot(a_rb[g], bd(wt[g])) for g in gr]

    o3 = [_dot(jnp.concatenate([m_cat[g], r_hat[g]], axis=0), bd(h0[g])) for g in gr]
    return [o3[g][c:] + y0[g] for g in gr], [o3[g][:c] + n_cat[g] for g in gr]


def _rwkv_mix_kernel(p_ref, pp_ref, pl_ref, plp_ref, mu_ref, mul_ref, w0_ref, a0_ref, wl_ref, al_ref,
                     kk_ref, ka_ref, rk_ref, gng_ref, gnb_ref, o_ref, h_scr):
    ci = pl.program_id(1)
    c = RW_CHUNK

    @pl.when(ci == 0)
    def _():
        h_scr[...] = jnp.zeros_like(h_scr)

    not_first = (ci != 0).astype(F32)
    row = lax.broadcasted_iota(jnp.int32, (c, 1), 0)

    def shifted(cur, prev_ref):
        prev_row = prev_ref[7:8, :] * not_first
        rolled = pltpu.roll(cur, 1, 0)
        return jnp.where(row == 0, prev_row, rolled)

    lcur = pl_ref[...]
    da = lcur + (shifted(lcur, plp_ref) - lcur) * mul_ref[...]
    dw = jnp.tanh(da)

    ti = lax.broadcasted_iota(jnp.int32, (c, RW_GW), 0)
    li = lax.broadcasted_iota(jnp.int32, (c, RW_GW), 1)
    si = li % HEAD
    lane_head = li // HEAD
    rb = lax.broadcasted_iota(jnp.int32, (RW_GW, RW_GW), 0) // HEAD
    cb = lax.broadcasted_iota(jnp.int32, (RW_GW, RW_GW), 1) // HEAD
    same_head = rb == cb
    seg = same_head.astype(BF16)
    masks = (ti >= si, ti > si, (ti == si).astype(F32), same_head, lane_head)
    tri = (lax.broadcasted_iota(jnp.int32, (c, c), 0) >= lax.broadcasted_iota(jnp.int32, (c, c), 1)).astype(BF16)

    gr = range(RW_GROUPS)
    gsl = [pl.ds(g * RW_GW, RW_GW) for g in gr]

    def lerp(off, g):
        cols = pl.ds(off + g * RW_GW, RW_GW)
        cur = p_ref[:, cols]
        return cur + (shifted(cur, pp_ref.at[:, cols]) - cur) * mu_ref[:, cols]

    r = [lerp(0, g) for g in gr]
    k = [lerp(D, g) for g in gr]
    v = [lerp(2 * D, g) for g in gr]
    lw = [-DECAY_SCALE * jax.nn.sigmoid(w0_ref[:, gsl[g]] + _dot(dw, wl_ref[:, gsl[g]])) for g in gr]
    a = [jax.nn.sigmoid(a0_ref[:, gsl[g]] + _dot(da, al_ref[:, gsl[g]])) for g in gr]
    kap = [k[g] * kk_ref[:, gsl[g]] for g in gr]
    kap = [kap[g] / jnp.maximum(jnp.sqrt(_head_sum(kap[g] * kap[g], seg)), 1e-12) for g in gr]
    k = [k[g] * (1.0 + (a[g] - 1.0) * ka_ref[:, gsl[g]]) for g in gr]
    bet = [kap[g] * a[g] for g in gr]
    lw_s = [_split2(lw[g]) for g in gr]
    cum = [jnp.dot(tri, lw_s[g][0], preferred_element_type=F32) + jnp.dot(tri, lw_s[g][1], preferred_element_type=F32)
           for g in gr]

    y, h_new = _rwkv_groups(r, lw, cum, k, v, kap, bet, [h_scr[g] for g in gr], masks)
    for g in gr:
        h_scr[g] = h_new[g]

    mean = [_head_sum(y[g], seg) * (1.0 / HEAD) for g in gr]
    dlt = [y[g] - mean[g] for g in gr]
    var = [_head_sum(dlt[g] * dlt[g], seg) * (1.0 / HEAD) for g in gr]
    bonus = [_head_sum(r[g] * k[g] * rk_ref[:, gsl[g]], seg) for g in gr]
    for g in gr:
        yn = dlt[g] * lax.rsqrt(var[g] + GN_EPS) * gng_ref[:, gsl[g]] + gnb_ref[:, gsl[g]] + bonus[g] * v[g]
        o_ref[:, gsl[g]] = (yn * jax.nn.silu(lerp(3 * D, g))).astype(BF16)


def _rwkv_mix(p, p_lora, mu, mu_l, w0, a0, w_lora, a_lora, k_k, k_a, r_k, gn_g, gn_b):
    n_tok = p.shape[0]
    c = RW_CHUNK
    nc = SEQ // c
    ncol = p.shape[1]
    nl = p_lora.shape[1]
    row = lambda b, ci: (b * nc + ci, 0)
    prev = lambda b, ci: (jnp.maximum((b * nc + ci) * (c // 8) - 1, 0), 0)
    vec = pl.BlockSpec((1, D), lambda b, ci: (0, 0))
    return pl.pallas_call(
        _rwkv_mix_kernel,
        out_shape=jax.ShapeDtypeStruct((n_tok, D), BF16),
        grid=(n_tok // SEQ, nc),
        in_specs=[pl.BlockSpec((c, ncol), row),
                  pl.BlockSpec((8, ncol), prev),
                  pl.BlockSpec((c, nl), row),
                  pl.BlockSpec((8, nl), prev),
                  pl.BlockSpec((1, ncol), lambda b, ci: (0, 0)),
                  pl.BlockSpec((1, nl), lambda b, ci: (0, 0)),
                  vec, vec,
                  pl.BlockSpec((RW_LP, D), lambda b, ci: (0, 0)),
                  pl.BlockSpec((RW_LP, D), lambda b, ci: (0, 0)),
                  vec, vec, vec, vec, vec],
        out_specs=pl.BlockSpec((c, D), row),
        scratch_shapes=[pltpu.VMEM((RW_GROUPS, HEAD, RW_GW), F32)],
        compiler_params=_cparams("parallel", "arbitrary"),
        name="rwkv_mix",
    )(p, p, p_lora, p_lora, mu, mu_l, w0, a0, w_lora, a_lora, k_k, k_a, r_k, gn_g, gn_b)


def kernel(x, c, positions, norm_g, mod_w, mod_b, final_norm_g, sg_w_in, sg_w_out, sg_ln_g, sg_ln_b, sg_w_spatial,
           sg_b_spatial, swa_w_in, swa_w_out, swa_sinks, rwkv_w_in, rwkv_w_out, rwkv_mu, rwkv_w0, rwkv_w_lora,
           rwkv_a0, rwkv_a_lora, rwkv_k_k, rwkv_k_a, rwkv_r_k, rwkv_gn_g, rwkv_gn_b):
    batch, seq, d = x.shape
    assert (seq, d) == (SEQ, D) and norm_g.shape[0] == DEPTH
    n_tok = batch * seq
    x2 = x.reshape(n_tok, D)
    c8 = jnp.zeros((8, D), F32).at[:batch].set(c)
    mod = _modulation(c8, mod_w, mod_b)
    pos2 = positions.reshape(n_tok, 1)
    half = HEAD // 2
    inv_freq = ROPE_THETA ** (-jnp.arange(half, dtype=F32) / half)
    invf = jnp.tile(inv_freq, LANES // half).reshape(1, LANES)
    row = lambda a: a.reshape(1, -1)

    for i in range(DEPTH):
        kind, j = i % 3, i // 3
        mod3 = mod[i, :batch].reshape(batch, 1, 3 * D)
        g = row(norm_g[i])
        if kind == 0:
            p = _in_proj(x2, g, mod3, sg_w_in[j].astype(BF16))
            mix = _sg_mix(p, row(sg_ln_g[j]), row(sg_ln_b[j]), sg_w_spatial[j],
                          jnp.broadcast_to(sg_b_spatial[j][:, :, None], (SG_GROUPS, SG_CHUNK, LANES)))
            w_out = sg_w_out[j]
        elif kind == 1:
            w = swa_w_in[j]
            w = jnp.concatenate([w[:, :D], w[:, D + 2 * SWA_KVW:], w[:, D:D + 2 * SWA_KVW]], axis=1).astype(BF16)
            p = _in_proj(x2, g, mod3, w)
            mix = _swa_mix(p, pos2, invf, swa_sinks[j])
            w_out = swa_w_out[j]
        else:
            w = rwkv_w_in[j]
            lpad = RW_LP - 2 * RW_LORA
            w_l = jnp.pad(w[:, 4 * D:], ((0, 0), (0, lpad))).astype(BF16)
            p, p_lora = _in_proj(x2, g, mod3, w[:, :4 * D].astype(BF16), w_l)
            mu = rwkv_mu[j]
            wl_pad = jnp.pad(rwkv_w_lora[j], ((0, RW_LP - RW_LORA), (0, 0))).astype(BF16)
            al_pad = jnp.pad(rwkv_a_lora[j], ((RW_LORA, lpad), (0, 0))).astype(BF16)
            mix = _rwkv_mix(p, p_lora, row(mu[:4 * D]), row(jnp.pad(mu[4 * D:], (0, lpad))), row(rwkv_w0[j]),
                            row(rwkv_a0[j]), wl_pad, al_pad, row(rwkv_k_k[j]),
                            row(rwkv_k_a[j]), row(rwkv_r_k[j]), row(rwkv_gn_g[j]), row(rwkv_gn_b[j]))
            w_out = rwkv_w_out[j]
        x2 = _out_proj(mix, w_out.astype(BF16), x2, mod3, row(final_norm_g) if i == DEPTH - 1 else None)
    return x2.reshape(batch, seq, D)
```

```python
import functools
import math

import jax
import jax.numpy as jnp
from jax import lax
from jax.experimental import pallas as pl
from jax.experimental.pallas import tpu as pltpu

F32 = jnp.float32
BF16 = jnp.bfloat16

D = 2048
SEQ = 4096
DEPTH = 4
HEAD = 64
LANES = 128
SG_CHUNK = 128
SG_GROUPS = 16
SWA_HEADS = D // HEAD
SWA_KV = SWA_HEADS // 8
SWA_KVW = SWA_KV * HEAD
SWA_BLOCK = 128
ROPE_THETA = 10000.0
RW_HEADS = D // HEAD
RW_LORA = 96
RW_LP = 256
RW_CHUNK = 64
RW_HG = 4
RW_GW = RW_HG * HEAD
RW_GROUPS = D // RW_GW
DECAY_SCALE = math.exp(-0.5)
GN_EPS = 64e-5
RMS_EPS = 1e-6
LN_EPS = 1e-5

VMEM_LIMIT = 48 * 1024 * 1024


def _cparams(*sem):
    return pltpu.CompilerParams(dimension_semantics=sem, vmem_limit_bytes=VMEM_LIMIT)


def _mod_kernel(c_ref, w_ref, b_ref, o_ref):
    cond = jax.nn.silu(c_ref[...]).astype(BF16)
    o_ref[0] = jnp.dot(cond, w_ref[0].astype(BF16), preferred_element_type=F32) + b_ref[0]


def _modulation(c8, mod_w, mod_b):
    tn = 1024
    n = mod_w.shape[2]
    return pl.pallas_call(
        _mod_kernel,
        out_shape=jax.ShapeDtypeStruct((DEPTH, 8, n), F32),
        grid=(DEPTH, n // tn),
        in_specs=[pl.BlockSpec((8, D), lambda l, j: (0, 0)),
                  pl.BlockSpec((1, D, tn), lambda l, j: (l, 0, j)),
                  pl.BlockSpec((1, 1, tn), lambda l, j: (l, 0, j))],
        out_specs=pl.BlockSpec((1, 8, tn), lambda l, j: (l, 0, j)),
        compiler_params=_cparams("parallel", "parallel"),
        name="modulation",
    )(c8, mod_w, mod_b.reshape(DEPTH, 1, n))


NORM_ROWS = 128


def _row_rsqrt(x_ref, rs_scr):
    def body(i, carry):
        rows = pl.ds(pl.multiple_of(i * NORM_ROWS, NORM_ROWS), NORM_ROWS)
        x = x_ref[rows, :]
        rs_scr[rows, :] = lax.rsqrt(jnp.mean(x * x, axis=-1, keepdims=True) + RMS_EPS)
        return carry
    lax.fori_loop(0, x_ref.shape[0] // NORM_ROWS, body, 0)


def _modulated(x_ref, rs_scr, g_ref, sc_ref, sh_ref):
    gain = g_ref[...] * (1.0 + sc_ref[0])
    return (x_ref[...] * rs_scr[...] * gain + sh_ref[0]).astype(BF16)


GELU_A = math.sqrt(2.0 / math.pi)
GELU_B = GELU_A * 0.044715
ACT_COLS = 256


def _in_proj_act_kernel(x_ref, g_ref, sc_ref, sh_ref, w_ref, act_ref, o_ref, rs_scr, *, select):
    @pl.when(pl.program_id(1) == 0)
    def _():
        _row_rsqrt(x_ref, rs_scr)

    h = _modulated(x_ref, rs_scr, g_ref, sc_ref, sh_ref)
    for n in range(o_ref.shape[1] // ACT_COLS):
        cols = pl.ds(n * ACT_COLS, ACT_COLS)
        p = jnp.dot(h, w_ref[:, cols], preferred_element_type=F32)
        gated = (0.5 * p) * (1.0 + jnp.tanh(p * (act_ref[0:1, cols] + act_ref[1:2, cols] * (p * p))))
        o_ref[:, cols] = jnp.where(act_ref[2:3, cols] > 0.0, gated, p) if select else gated


def _in_proj_lora_kernel(x_ref, g_ref, sc_ref, sh_ref, w_ref, wl_ref, o_ref, ol_ref, rs_scr):
    @pl.when(pl.program_id(1) == 0)
    def _():
        _row_rsqrt(x_ref, rs_scr)
        ol_ref[...] = jnp.dot(_modulated(x_ref, rs_scr, g_ref, sc_ref, sh_ref), wl_ref[...],
                              preferred_element_type=F32)

    h = _modulated(x_ref, rs_scr, g_ref, sc_ref, sh_ref)
    o_ref[...] = jnp.dot(h, w_ref[...], preferred_element_type=F32)


def _in_proj(x2, g, mod3, w_bf, ncols, act=None, select=True, w_lora_bf=None, *, tm=1024):
    n_tok = x2.shape[0]
    tn = 1024 if ncols % 1024 == 0 else 1536
    assert ncols % tn == 0
    per_b = SEQ // tm
    x_spec = pl.BlockSpec((tm, D), lambda i, j: (i, 0))
    g_spec = pl.BlockSpec((1, D), lambda i, j: (0, 0))
    sh_spec = pl.BlockSpec((1, 1, D), lambda i, j: (i // per_b, 0, 0))
    sc_spec = pl.BlockSpec((1, 1, D), lambda i, j: (i // per_b, 0, 1))
    w_spec = pl.BlockSpec((D, tn), lambda i, j: (0, j))
    o_spec = pl.BlockSpec((tm, tn), lambda i, j: (i, j))
    grid = (n_tok // tm, ncols // tn)
    scratch = [pltpu.VMEM((tm, 1), F32)]
    if w_lora_bf is None:
        return pl.pallas_call(
            functools.partial(_in_proj_act_kernel, select=select),
            out_shape=jax.ShapeDtypeStruct((n_tok, ncols), F32),
            grid=grid,
            in_specs=[x_spec, g_spec, sc_spec, sh_spec, w_spec, pl.BlockSpec((8, tn), lambda i, j: (0, j))],
            out_specs=o_spec,
            scratch_shapes=scratch,
            compiler_params=_cparams("parallel", "arbitrary"),
            name="in_proj_act",
        )(x2, g, mod3, mod3, w_bf, act)
    nl = w_lora_bf.shape[1]
    return pl.pallas_call(
        _in_proj_lora_kernel,
        out_shape=(jax.ShapeDtypeStruct((n_tok, ncols), F32), jax.ShapeDtypeStruct((n_tok, nl), F32)),
        grid=grid,
        in_specs=[x_spec, g_spec, sc_spec, sh_spec, w_spec, pl.BlockSpec((D, nl), lambda i, j: (0, 0))],
        out_specs=(o_spec, pl.BlockSpec((tm, nl), lambda i, j: (i, 0))),
        scratch_shapes=scratch,
        compiler_params=_cparams("parallel", "arbitrary"),
        name="in_proj_lora",
    )(x2, g, mod3, mod3, w_bf, w_lora_bf)


def _out_proj_kernel(m_ref, w_ref, x_ref, gate_ref, o_ref):
    y = jnp.dot(m_ref[...], w_ref[...], preferred_element_type=F32)
    o_ref[...] = x_ref[...] + gate_ref[0] * y


def _out_proj_final_kernel(m_ref, w_ref, x_ref, gate_ref, fg_ref, o_ref):
    y = jnp.dot(m_ref[...], w_ref[...], preferred_element_type=F32)
    xn = x_ref[...] + gate_ref[0] * y
    o_ref[...] = xn * lax.rsqrt(jnp.mean(xn * xn, axis=-1, keepdims=True) + RMS_EPS) * fg_ref[...]


def _out_proj(mix, w_bf, x2, mod3, final_g=None, *, tm=512):
    n_tok = x2.shape[0]
    per_b = SEQ // tm
    in_specs = [pl.BlockSpec((tm, D), lambda i: (i, 0)),
                pl.BlockSpec((D, D), lambda i: (0, 0)),
                pl.BlockSpec((tm, D), lambda i: (i, 0)),
                pl.BlockSpec((1, 1, D), lambda i: (i // per_b, 0, 2))]
    args = [mix, w_bf, x2, mod3]
    kern = _out_proj_kernel
    if final_g is not None:
        in_specs.append(pl.BlockSpec((1, D), lambda i: (0, 0)))
        args.append(final_g)
        kern = _out_proj_final_kernel
    return pl.pallas_call(
        kern,
        out_shape=jax.ShapeDtypeStruct((n_tok, D), F32),
        grid=(n_tok // tm,),
        in_specs=in_specs,
        out_specs=pl.BlockSpec((tm, D), lambda i: (i, 0)),
        compiler_params=_cparams("parallel"),
        name="out_proj_final" if final_g is not None else "out_proj",
    )(*args)


def _sg_mix_kernel(u_ref, v_ref, z_ref, lng_ref, lnb_ref, ws_ref, bs_ref, o_ref, *, tm):
    ti = lax.broadcasted_iota(jnp.int32, (SG_CHUNK, SG_CHUNK), 0)
    si = lax.broadcasted_iota(jnp.int32, (SG_CHUNK, SG_CHUNK), 1)
    causal = ti >= si
    ws = [jnp.where(causal, ws_ref[g], 0.0).astype(BF16) for g in range(SG_GROUPS)]
    for c in range(tm // SG_CHUNK):
        rows = pl.ds(c * SG_CHUNK, SG_CHUNK)
        v = v_ref[rows, :]
        mean = jnp.mean(v, axis=-1, keepdims=True)
        d = v - mean
        var = jnp.mean(d * d, axis=-1, keepdims=True)
        vn = (d * lax.rsqrt(var + LN_EPS) * lng_ref[...] + lnb_ref[...]).astype(BF16)
        f = [jnp.dot(ws[g], vn[:, g * LANES:(g + 1) * LANES], preferred_element_type=F32) for g in range(SG_GROUPS)]
        for g in range(SG_GROUPS):
            cols = pl.ds(g * LANES, LANES)
            o_ref[rows, cols] = (u_ref[rows, cols] * (f[g] + bs_ref[g]) * z_ref[rows, cols]).astype(BF16)


def _sg_mix(p, ln_g, ln_b, w_s, b_full, *, tm=256):
    n_tok = p.shape[0]
    return pl.pallas_call(
        functools.partial(_sg_mix_kernel, tm=tm),
        out_shape=jax.ShapeDtypeStruct((n_tok, D), BF16),
        grid=(n_tok // tm,),
        in_specs=[pl.BlockSpec((tm, D), lambda i: (i, 0)),
                  pl.BlockSpec((tm, D), lambda i: (i, 1)),
                  pl.BlockSpec((tm, D), lambda i: (i, 2)),
                  pl.BlockSpec((1, D), lambda i: (0, 0)),
                  pl.BlockSpec((1, D), lambda i: (0, 0)),
                  pl.BlockSpec((SG_GROUPS, SG_CHUNK, SG_CHUNK), lambda i: (0, 0, 0)),
                  pl.BlockSpec((SG_GROUPS, SG_CHUNK, LANES), lambda i: (0, 0, 0))],
        out_specs=pl.BlockSpec((tm, D), lambda i: (i, 0)),
        compiler_params=_cparams("parallel"),
        name="sg_mix",
    )(p, p, p, ln_g, ln_b, w_s, b_full)


def _rope_tables(pos_ref, invf_ref):
    ang = pos_ref[...].astype(F32) * invf_ref[...]
    lane = lax.broadcasted_iota(jnp.int32, ang.shape, 1)
    first = (lane % HEAD) < (HEAD // 2)
    return jnp.cos(ang), jnp.where(first, -jnp.sin(ang), jnp.sin(ang)), first


def _rope(x, cos, sin_signed, first):
    partner = jnp.where(first, pltpu.roll(x, LANES - HEAD // 2, 1), pltpu.roll(x, HEAD // 2, 1))
    return x * cos + partner * sin_signed


def _swa_mix_kernel(sink_ref, p_ref, kvp_ref, pos_ref, posp_ref, invf_ref, o_ref):
    i = pl.program_id(0)
    first_key = jnp.where((i % (SEQ // SWA_BLOCK)) != 0, 0, SWA_BLOCK)
    cos, sin_s, first = _rope_tables(pos_ref, invf_ref)
    cosp, sinp_s, _ = _rope_tables(posp_ref, invf_ref)
    lane = lax.broadcasted_iota(jnp.int32, (2 * SWA_BLOCK, LANES), 1)
    low = lane < HEAD

    qi = lax.broadcasted_iota(jnp.int32, (SWA_BLOCK, 2 * SWA_BLOCK), 0)
    kj = lax.broadcasted_iota(jnp.int32, (SWA_BLOCK, 2 * SWA_BLOCK), 1)
    rel = qi + SWA_BLOCK - kj
    mask = (rel >= 0) & (rel < SWA_BLOCK) & (kj >= first_key)

    k_lo, k_hi, v_lo, v_hi = [], [], [], []
    for kp in range(SWA_KV // 2):
        ksl = pl.ds(kp * LANES, LANES)
        vsl = pl.ds(SWA_KVW + kp * LANES, LANES)
        k2 = jnp.concatenate([_rope(kvp_ref[:, ksl], cosp, sinp_s, first),
                              _rope(p_ref[:, pl.ds(D + kp * LANES, LANES)], cos, sin_s, first)], axis=0)
        v2 = jnp.concatenate([kvp_ref[:, vsl], p_ref[:, pl.ds(D + SWA_KVW + kp * LANES, LANES)]], axis=0)
        k2r = pltpu.roll(k2, HEAD, 1)
        v2r = pltpu.roll(v2, HEAD, 1)
        for sub in range(2):
            src_k, src_kr = (k2, k2r) if sub == 0 else (k2r, k2)
            src_v, src_vr = (v2, v2r) if sub == 0 else (v2r, v2)
            k_lo.append(jnp.where(low, src_k, 0.0).astype(BF16))
            k_hi.append(jnp.where(low, 0.0, src_kr).astype(BF16))
            v_lo.append(jnp.where(low, src_v, 0.0).astype(BF16))
            v_hi.append(jnp.where(low, 0.0, src_vr).astype(BF16))

    npair = SWA_HEADS // 2
    cols = [pl.ds(pr * LANES, LANES) for pr in range(npair)]
    q2 = [(_rope(p_ref[:, cols[pr]], cos, sin_s, first) * (HEAD ** -0.5)).astype(BF16) for pr in range(npair)]
    s = [lax.dot_general(q2[h // 2], (k_hi if h % 2 else k_lo)[h // 8], (((1,), (1,)), ((), ())),
                         preferred_element_type=F32) for h in range(SWA_HEADS)]
    prob = []
    for h in range(SWA_HEADS):
        sh = jnp.where(mask, s[h], -jnp.inf)
        sink = sink_ref[h]
        m = jnp.maximum(jnp.max(sh, axis=-1, keepdims=True), sink)
        e = jnp.exp(sh - m)
        denom = jnp.sum(e, axis=-1, keepdims=True) + jnp.exp(sink - m)
        prob.append((e * (1.0 / denom)).astype(BF16))
    o = [jnp.dot(prob[h], (v_hi if h % 2 else v_lo)[h // 8], preferred_element_type=F32) for h in range(SWA_HEADS)]
    zoff = D + 2 * SWA_KVW
    for pr in range(npair):
        gate = p_ref[:, pl.ds(zoff + pr * LANES, LANES)]
        o_ref[:, cols[pr]] = ((o[2 * pr] + o[2 * pr + 1]) * gate).astype(BF16)


def _swa_mix(p, pos2, invf, sinks):
    n_tok, ncols = p.shape
    tb = SWA_BLOCK
    kvb = D // (2 * SWA_KVW)
    return pl.pallas_call(
        _swa_mix_kernel,
        out_shape=jax.ShapeDtypeStruct((n_tok, D), BF16),
        grid=(n_tok // tb,),
        in_specs=[pl.BlockSpec(memory_space=pltpu.SMEM),
                  pl.BlockSpec((tb, ncols), lambda i: (i, 0)),
                  pl.BlockSpec((tb, 2 * SWA_KVW), lambda i: (jnp.maximum(i - 1, 0), kvb)),
                  pl.BlockSpec((tb, 1), lambda i: (i, 0)),
                  pl.BlockSpec((tb, 1), lambda i: (jnp.maximum(i - 1, 0), 0)),
                  pl.BlockSpec((1, LANES), lambda i: (0, 0))],
        out_specs=pl.BlockSpec((tb, D), lambda i: (i, 0)),
        compiler_params=_cparams("parallel"),
        name="swa_mix",
    )(sinks, p, p, pos2, pos2, invf)


def _split2(x):
    hi = x.astype(BF16)
    return hi, (x - hi.astype(F32)).astype(BF16)


def _head_sums(xs, seg):
    parts = []
    for x in xs:
        parts.extend(_split2(x))
    out = jnp.dot(jnp.concatenate(parts, axis=0), seg, preferred_element_type=F32)
    n = xs[0].shape[0]
    return [out[2 * i * n:(2 * i + 1) * n] + out[(2 * i + 1) * n:(2 * i + 2) * n] for i in range(len(xs))]


def _bd(x, same_head):
    t = jnp.concatenate([x] * RW_HG, axis=0)
    return jnp.where(same_head, t, 0.0).astype(BF16)


def _diag_blocks(o, lane_head):
    acc = o[0:HEAD]
    for h in range(1, RW_HG):
        acc = jnp.where(lane_head == h, o[h * HEAD:(h + 1) * HEAD], acc)
    return acc


def _dot(a, b):
    return jnp.dot(a.astype(BF16), b.astype(BF16), preferred_element_type=F32)


def _rwkv_groups(r, lw, cum, k, v, kap, bet, h0, masks):
    incl, strict, eye, same_head, lane_head = masks
    c = RW_CHUNK
    gr = range(len(r))
    bd = lambda t: _bd(t, same_head)
    cref = [cum[g][c // 2 - 1:c // 2] for g in gr]
    cend = [cum[g][c - 1:c] for g in gr]
    r_true = [r[g] * jnp.exp(cum[g]) for g in gr]
    kap_true = [kap[g] * jnp.exp(cum[g] - lw[g]) for g in gr]
    r_s = [r[g] * jnp.exp(cum[g] - cref[g]) for g in gr]
    kap_s = [kap[g] * jnp.exp(cum[g] - lw[g] - cref[g]) for g in gr]
    ginv = [jnp.exp(cref[g] - cum[g]) for g in gr]
    eend = [jnp.exp(cend[g] - cum[g]) for g in gr]

    s = [lax.dot_general(jnp.concatenate([r_s[g], kap_s[g]], axis=0).astype(BF16),
                         jnp.concatenate([bd(k[g] * ginv[g]), bd(bet[g] * ginv[g])], axis=0),
                         (((1,), (1,)), ((), ())), preferred_element_type=F32) for g in gr]
    a_rk = [jnp.where(incl, s[g][:c, :RW_GW], 0.0) for g in gr]
    a_rb = [jnp.where(incl, s[g][:c, RW_GW:], 0.0) for g in gr]
    a_kk = [jnp.where(strict, s[g][c:, :RW_GW], 0.0) for g in gr]
    a_kb = [jnp.where(strict, s[g][c:, RW_GW:], 0.0) for g in gr]

    x = [eye - a_kb[g] for g in gr]
    p = [_dot(a_kb[g], bd(a_kb[g])) for g in gr]
    akv = [_dot(a_kk[g], bd(v[g])) for g in gr]
    for _ in range(4):
        xp = [_dot(jnp.concatenate([x[g], p[g]], axis=0), bd(p[g])) for g in gr]
        x = [x[g] + xp[g][:c] for g in gr]
        p = [xp[g][c:] for g in gr]
    t_inv = [x[g] + _dot(x[g], bd(p[g])) for g in gr]

    wu = [-_dot(t_inv[g], jnp.concatenate([bd(kap_true[g]), bd(akv[g])], axis=1)) for g in gr]
    wt = [wu[g][:, :RW_GW] for g in gr]
    ut = [wu[g][:, RW_GW:] for g in gr]

    o2 = [lax.dot_general(
        jnp.concatenate([k[g] * eend[g], bet[g] * eend[g]], axis=0).astype(BF16),
        jnp.concatenate([jnp.concatenate([v[g], jnp.zeros_like(v[g])], axis=1),
                         jnp.concatenate([ut[g], wt[g]], axis=1)], axis=0).astype(BF16),
        (((0,), (0,)), ((), ())), preferred_element_type=F32) for g in gr]
    n_cat = [_diag_blocks(o2[g][:, :RW_GW], lane_head) for g in gr]
    m_cat = [_diag_blocks(o2[g][:, RW_GW:], lane_head) + eye * jnp.exp(cend[g]) for g in gr]

    y0 = [_dot(jnp.concatenate([a_rk[g], a_rb[g]], axis=1),
               jnp.concatenate([bd(v[g]), bd(ut[g])], axis=0)) for g in gr]
    r_hat = [r_true[g] + _dot(a_rb[g], bd(wt[g])) for g in gr]

    o3 = [_dot(jnp.concatenate([m_cat[g], r_hat[g]], axis=0), bd(h0[g])) for g in gr]
    return [o3[g][c:] + y0[g] for g in gr], [o3[g][:c] + n_cat[g] for g in gr]


def _rwkv_mix_kernel(p_ref, pp_ref, pl_ref, plp_ref, mu_ref, mul_ref, w0_ref, a0_ref, wl_ref, al_ref,
                     kk_ref, ka_ref, rk_ref, gng_ref, gnb_ref, o_ref, h_scr):
    ci = pl.program_id(1)
    c = RW_CHUNK

    @pl.when(ci == 0)
    def _():
        h_scr[...] = jnp.zeros_like(h_scr)

    not_first = (ci != 0).astype(F32)
    row = lax.broadcasted_iota(jnp.int32, (c, 1), 0)

    def shifted(cur, prev_ref):
        prev_row = prev_ref[7:8, :] * not_first
        rolled = pltpu.roll(cur, 1, 0)
        return jnp.where(row == 0, prev_row, rolled)

    lcur = pl_ref[...]
    da = lcur + (shifted(lcur, plp_ref) - lcur) * mul_ref[...]
    dw = jnp.tanh(da)

    ti = lax.broadcasted_iota(jnp.int32, (c, RW_GW), 0)
    li = lax.broadcasted_iota(jnp.int32, (c, RW_GW), 1)
    si = li % HEAD
    lane_head = li // HEAD
    rb = lax.broadcasted_iota(jnp.int32, (RW_GW, RW_GW), 0) // HEAD
    cb = lax.broadcasted_iota(jnp.int32, (RW_GW, RW_GW), 1) // HEAD
    same_head = rb == cb
    seg = same_head.astype(BF16)
    masks = (ti >= si, ti > si, (ti == si).astype(F32), same_head, lane_head)
    tri = (lax.broadcasted_iota(jnp.int32, (c, c), 0) >= lax.broadcasted_iota(jnp.int32, (c, c), 1)).astype(BF16)

    gr = range(RW_GROUPS)
    gsl = [pl.ds(g * RW_GW, RW_GW) for g in gr]

    def lerp(off, g):
        cols = pl.ds(off + g * RW_GW, RW_GW)
        cur = p_ref[:, cols]
        return cur + (shifted(cur, pp_ref.at[:, cols]) - cur) * mu_ref[:, cols]

    r = [lerp(0, g) for g in gr]
    k = [lerp(D, g) for g in gr]
    v = [lerp(2 * D, g) for g in gr]
    lw = [-DECAY_SCALE * jax.nn.sigmoid(w0_ref[:, gsl[g]] + _dot(dw, wl_ref[:, gsl[g]])) for g in gr]
    a = [jax.nn.sigmoid(a0_ref[:, gsl[g]] + _dot(da, al_ref[:, gsl[g]])) for g in gr]
    kap = [k[g] * kk_ref[:, gsl[g]] for g in gr]
    k = [k[g] * (1.0 + (a[g] - 1.0) * ka_ref[:, gsl[g]]) for g in gr]
    sums = _head_sums([kap[g] * kap[g] for g in gr] + [r[g] * k[g] * rk_ref[:, gsl[g]] for g in gr], seg)
    kap = [kap[g] / jnp.maximum(jnp.sqrt(sums[g]), 1e-12) for g in gr]
    bonus = sums[RW_GROUPS:]
    bet = [kap[g] * a[g] for g in gr]
    lw_s = [_split2(lw[g]) for g in gr]
    cum = [jnp.dot(tri, lw_s[g][0], preferred_element_type=F32) + jnp.dot(tri, lw_s[g][1], preferred_element_type=F32)
           for g in gr]

    y, h_new = _rwkv_groups(r, lw, cum, k, v, kap, bet, [h_scr[g] for g in gr], masks)
    for g in gr:
        h_scr[g] = h_new[g]

    ysum = _head_sums(y, seg)
    dlt = [y[g] - ysum[g] * (1.0 / HEAD) for g in gr]
    dsum = _head_sums([dlt[g] * dlt[g] for g in gr], seg)
    for g in gr:
        rstd = lax.rsqrt(dsum[g] * (1.0 / HEAD) + GN_EPS)
        yn = dlt[g] * rstd * gng_ref[:, gsl[g]] + gnb_ref[:, gsl[g]] + bonus[g] * v[g]
        o_ref[:, gsl[g]] = (yn * jax.nn.silu(lerp(3 * D, g))).astype(BF16)


def _rwkv_mix(p, p_lora, mu, mu_l, w0, a0, w_lora, a_lora, k_k, k_a, r_k, gn_g, gn_b):
    n_tok = p.shape[0]
    c = RW_CHUNK
    nc = SEQ // c
    ncol = p.shape[1]
    nl = p_lora.shape[1]
    row = lambda b, ci: (b * nc + ci, 0)
    prev = lambda b, ci: (jnp.maximum((b * nc + ci) * (c // 8) - 1, 0), 0)
    vec = pl.BlockSpec((1, D), lambda b, ci: (0, 0))
    return pl.pallas_call(
        _rwkv_mix_kernel,
        out_shape=jax.ShapeDtypeStruct((n_tok, D), BF16),
        grid=(n_tok // SEQ, nc),
        in_specs=[pl.BlockSpec((c, ncol), row),
                  pl.BlockSpec((8, ncol), prev),
                  pl.BlockSpec((c, nl), row),
                  pl.BlockSpec((8, nl), prev),
                  pl.BlockSpec((1, ncol), lambda b, ci: (0, 0)),
                  pl.BlockSpec((1, nl), lambda b, ci: (0, 0)),
                  vec, vec,
                  pl.BlockSpec((RW_LP, D), lambda b, ci: (0, 0)),
                  pl.BlockSpec((RW_LP, D), lambda b, ci: (0, 0)),
                  vec, vec, vec, vec, vec],
        out_specs=pl.BlockSpec((c, D), row),
        scratch_shapes=[pltpu.VMEM((RW_GROUPS, HEAD, RW_GW), F32)],
        compiler_params=_cparams("parallel", "arbitrary"),
        name="rwkv_mix",
    )(p, p, p_lora, p_lora, mu, mu_l, w0, a0, w_lora, a_lora, k_k, k_a, r_k, gn_g, gn_b)


def kernel(x, c, positions, norm_g, mod_w, mod_b, final_norm_g, sg_w_in, sg_w_out, sg_ln_g, sg_ln_b, sg_w_spatial,
           sg_b_spatial, swa_w_in, swa_w_out, swa_sinks, rwkv_w_in, rwkv_w_out, rwkv_mu, rwkv_w0, rwkv_w_lora,
           rwkv_a0, rwkv_a_lora, rwkv_k_k, rwkv_k_a, rwkv_r_k, rwkv_gn_g, rwkv_gn_b):
    batch, seq, d = x.shape
    assert (seq, d) == (SEQ, D) and norm_g.shape[0] == DEPTH
    n_tok = batch * seq
    x2 = x.reshape(n_tok, D)
    c8 = jnp.zeros((8, D), F32).at[:batch].set(c)
    mod = _modulation(c8, mod_w, mod_b)
    pos2 = positions.reshape(n_tok, 1)
    half = HEAD // 2
    inv_freq = ROPE_THETA ** (-jnp.arange(half, dtype=F32) / half)
    invf = jnp.tile(inv_freq, LANES // half).reshape(1, LANES)
    row = lambda a: a.reshape(1, -1)

    def act_rows(spans):
        rows = [jnp.concatenate([jnp.full((w,), val[t], F32) for w, *val in spans]) for t in range(3)]
        return jnp.zeros((8, rows[0].shape[0]), F32).at[:3].set(jnp.stack(rows))

    gelu_span = (GELU_A, GELU_B, 1.0)
    silu_span = (0.5, 0.0, 1.0)
    for i in range(DEPTH):
        kind, j = i % 3, i // 3
        mod3 = mod[i, :batch].reshape(batch, 1, 3 * D)
        g = row(norm_g[i])
        if kind == 0:
            act = act_rows([(2 * D, *gelu_span), (D, *silu_span)])
            p = _in_proj(x2, g, mod3, sg_w_in[j].astype(BF16), 3 * D, act, select=False)
            mix = _sg_mix(p, row(sg_ln_g[j]), row(sg_ln_b[j]), sg_w_spatial[j],
                          jnp.broadcast_to(sg_b_spatial[j][:, :, None], (SG_GROUPS, SG_CHUNK, LANES)))
            w_out = sg_w_out[j]
        elif kind == 1:
            act = act_rows([(D + 2 * SWA_KVW, 0.0, 0.0, 0.0), (D, *silu_span)])
            p = _in_proj(x2, g, mod3, swa_w_in[j].astype(BF16), 2 * D + 2 * SWA_KVW, act)
            mix = _swa_mix(p, pos2, invf, swa_sinks[j])
            w_out = swa_w_out[j]
        else:
            w = rwkv_w_in[j]
            lpad = RW_LP - 2 * RW_LORA
            w_l = jnp.pad(w[:, 4 * D:], ((0, 0), (0, lpad))).astype(BF16)
            p, p_lora = _in_proj(x2, g, mod3, w.astype(BF16), 4 * D, w_lora_bf=w_l)
            mu = rwkv_mu[j]
            wl_pad = jnp.pad(rwkv_w_lora[j], ((0, RW_LP - RW_LORA), (0, 0))).astype(BF16)
            al_pad = jnp.pad(rwkv_a_lora[j], ((RW_LORA, lpad), (0, 0))).astype(BF16)
            mix = _rwkv_mix(p, p_lora, row(mu[:4 * D]), row(jnp.pad(mu[4 * D:], (0, lpad))), row(rwkv_w0[j]),
                            row(rwkv_a0[j]), wl_pad, al_pad, row(rwkv_k_k[j]),
                            row(rwkv_k_a[j]), row(rwkv_r_k[j]), row(rwkv_gn_g[j]), row(rwkv_gn_b[j]))
            w_out = rwkv_w_out[j]
        x2 = _out_proj(mix, w_out.astype(BF16), x2, mod3, row(final_norm_g) if i == DEPTH - 1 else None)
    return x2.reshape(batch, seq, D)
```

```python
import functools
import math

import jax
import jax.numpy as jnp
from jax import lax
from jax.experimental import pallas as pl
from jax.experimental.pallas import tpu as pltpu

F32 = jnp.float32
BF16 = jnp.bfloat16

D = 2048
SEQ = 4096
DEPTH = 4
HEAD = 64
LANES = 128
SG_CHUNK = 128
SG_GROUPS = 16
SWA_HEADS = D // HEAD
SWA_KV = SWA_HEADS // 8
SWA_KVW = SWA_KV * HEAD
SWA_BLOCK = 128
ROPE_THETA = 10000.0
RW_HEADS = D // HEAD
RW_LORA = 96
RW_LP = 256
RW_CHUNK = 64
RW_HG = 4
RW_GW = RW_HG * HEAD
RW_GROUPS = D // RW_GW
RW_STAGE = 8
DECAY_SCALE = math.exp(-0.5)
GN_EPS = 64e-5
RMS_EPS = 1e-6
LN_EPS = 1e-5

VMEM_LIMIT = 48 * 1024 * 1024


def _cparams(*sem):
    return pltpu.CompilerParams(dimension_semantics=sem, vmem_limit_bytes=VMEM_LIMIT)


def _mod_kernel(c_ref, w_ref, b_ref, o_ref):
    cond = jax.nn.silu(c_ref[...]).astype(BF16)
    o_ref[0] = jnp.dot(cond, w_ref[0].astype(BF16), preferred_element_type=F32) + b_ref[0]


def _modulation(c8, mod_w, mod_b):
    tn = 1024
    n = mod_w.shape[2]
    return pl.pallas_call(
        _mod_kernel,
        out_shape=jax.ShapeDtypeStruct((DEPTH, 8, n), F32),
        grid=(DEPTH, n // tn),
        in_specs=[pl.BlockSpec((8, D), lambda l, j: (0, 0)),
                  pl.BlockSpec((1, D, tn), lambda l, j: (l, 0, j)),
                  pl.BlockSpec((1, 1, tn), lambda l, j: (l, 0, j))],
        out_specs=pl.BlockSpec((1, 8, tn), lambda l, j: (l, 0, j)),
        compiler_params=_cparams("parallel", "parallel"),
        name="modulation",
    )(c8, mod_w, mod_b.reshape(DEPTH, 1, n))


NORM_ROWS = 128


def _row_rsqrt(x_ref, rs_scr):
    def body(i, carry):
        rows = pl.ds(pl.multiple_of(i * NORM_ROWS, NORM_ROWS), NORM_ROWS)
        x = x_ref[rows, :]
        rs_scr[rows, :] = lax.rsqrt(jnp.mean(x * x, axis=-1, keepdims=True) + RMS_EPS)
        return carry
    lax.fori_loop(0, x_ref.shape[0] // NORM_ROWS, body, 0)


def _modulated(x_ref, rs_scr, g_ref, sc_ref, sh_ref):
    gain = g_ref[...] * (1.0 + sc_ref[0])
    return (x_ref[...] * rs_scr[...] * gain + sh_ref[0]).astype(BF16)


GELU_A = math.sqrt(2.0 / math.pi)
GELU_B = GELU_A * 0.044715
ACT_COLS = 256


def _in_proj_act_kernel(x_ref, g_ref, sc_ref, sh_ref, w_ref, act_ref, o_ref, rs_scr, *, select):
    @pl.when(pl.program_id(1) == 0)
    def _():
        _row_rsqrt(x_ref, rs_scr)

    h = _modulated(x_ref, rs_scr, g_ref, sc_ref, sh_ref)
    for n in range(o_ref.shape[1] // ACT_COLS):
        cols = pl.ds(n * ACT_COLS, ACT_COLS)
        p = jnp.dot(h, w_ref[:, cols], preferred_element_type=F32)
        gated = (0.5 * p) * (1.0 + jnp.tanh(p * (act_ref[0:1, cols] + act_ref[1:2, cols] * (p * p))))
        o_ref[:, cols] = jnp.where(act_ref[2:3, cols] > 0.0, gated, p) if select else gated


def _in_proj_lora_kernel(x_ref, g_ref, sc_ref, sh_ref, w_ref, wl_ref, o_ref, ol_ref, rs_scr):
    @pl.when(pl.program_id(1) == 0)
    def _():
        _row_rsqrt(x_ref, rs_scr)
        ol_ref[...] = jnp.dot(_modulated(x_ref, rs_scr, g_ref, sc_ref, sh_ref), wl_ref[...],
                              preferred_element_type=F32)

    h = _modulated(x_ref, rs_scr, g_ref, sc_ref, sh_ref)
    o_ref[...] = jnp.dot(h, w_ref[...], preferred_element_type=F32)


def _in_proj(x2, g, mod3, w_bf, wl, ncols, act=None, select=True, w_lora_bf=None, *, tm=1024):
    n_tok = x2.shape[0]
    tn = 1024 if ncols % 1024 == 0 else 1536
    assert ncols % tn == 0
    per_b = SEQ // tm
    x_spec = pl.BlockSpec((tm, D), lambda i, j: (i, 0))
    g_spec = pl.BlockSpec((1, D), lambda i, j: (0, 0))
    sh_spec = pl.BlockSpec((1, 1, D), lambda i, j: (i // per_b, 0, 0))
    sc_spec = pl.BlockSpec((1, 1, D), lambda i, j: (i // per_b, 0, 1))
    w_spec = pl.BlockSpec((None, D, tn), lambda i, j: (wl, 0, j))
    o_spec = pl.BlockSpec((tm, tn), lambda i, j: (i, j))
    grid = (n_tok // tm, ncols // tn)
    scratch = [pltpu.VMEM((tm, 1), F32)]
    if w_lora_bf is None:
        return pl.pallas_call(
            functools.partial(_in_proj_act_kernel, select=select),
            out_shape=jax.ShapeDtypeStruct((n_tok, ncols), F32),
            grid=grid,
            in_specs=[x_spec, g_spec, sc_spec, sh_spec, w_spec, pl.BlockSpec((8, tn), lambda i, j: (0, j))],
            out_specs=o_spec,
            scratch_shapes=scratch,
            compiler_params=_cparams("parallel", "arbitrary"),
            name="in_proj_act",
        )(x2, g, mod3, mod3, w_bf, act)
    nl = w_lora_bf.shape[1]
    return pl.pallas_call(
        _in_proj_lora_kernel,
        out_shape=(jax.ShapeDtypeStruct((n_tok, ncols), F32), jax.ShapeDtypeStruct((n_tok, nl), F32)),
        grid=grid,
        in_specs=[x_spec, g_spec, sc_spec, sh_spec, w_spec, pl.BlockSpec((D, nl), lambda i, j: (0, 0))],
        out_specs=(o_spec, pl.BlockSpec((tm, nl), lambda i, j: (i, 0))),
        scratch_shapes=scratch,
        compiler_params=_cparams("parallel", "arbitrary"),
        name="in_proj_lora",
    )(x2, g, mod3, mod3, w_bf, w_lora_bf)


def _out_proj_kernel(m_ref, w_ref, x_ref, gate_ref, o_ref):
    y = jnp.dot(m_ref[...], w_ref[...], preferred_element_type=F32)
    o_ref[...] = x_ref[...] + gate_ref[0] * y


def _out_proj_final_kernel(m_ref, w_ref, x_ref, gate_ref, fg_ref, o_ref):
    y = jnp.dot(m_ref[...], w_ref[...], preferred_element_type=F32)
    xn = x_ref[...] + gate_ref[0] * y
    o_ref[...] = xn * lax.rsqrt(jnp.mean(xn * xn, axis=-1, keepdims=True) + RMS_EPS) * fg_ref[...]


def _out_proj(mix, w_bf, wl, x2, mod3, final_g=None, *, tm=512):
    n_tok = x2.shape[0]
    per_b = SEQ // tm
    in_specs = [pl.BlockSpec((tm, D), lambda i: (i, 0)),
                pl.BlockSpec((None, D, D), lambda i: (wl, 0, 0)),
                pl.BlockSpec((tm, D), lambda i: (i, 0)),
                pl.BlockSpec((1, 1, D), lambda i: (i // per_b, 0, 2))]
    args = [mix, w_bf, x2, mod3]
    kern = _out_proj_kernel
    if final_g is not None:
        in_specs.append(pl.BlockSpec((1, D), lambda i: (0, 0)))
        args.append(final_g)
        kern = _out_proj_final_kernel
    return pl.pallas_call(
        kern,
        out_shape=jax.ShapeDtypeStruct((n_tok, D), F32),
        grid=(n_tok // tm,),
        in_specs=in_specs,
        out_specs=pl.BlockSpec((tm, D), lambda i: (i, 0)),
        compiler_params=_cparams("parallel"),
        name="out_proj_final" if final_g is not None else "out_proj",
    )(*args)


def _sg_mix_kernel(u_ref, v_ref, z_ref, lng_ref, lnb_ref, ws_ref, bs_ref, o_ref, *, tm):
    ti = lax.broadcasted_iota(jnp.int32, (SG_CHUNK, SG_CHUNK), 0)
    si = lax.broadcasted_iota(jnp.int32, (SG_CHUNK, SG_CHUNK), 1)
    causal = ti >= si
    ws = [jnp.where(causal, ws_ref[g], 0.0).astype(BF16) for g in range(SG_GROUPS)]
    for c in range(tm // SG_CHUNK):
        rows = pl.ds(c * SG_CHUNK, SG_CHUNK)
        v = v_ref[rows, :]
        mean = jnp.mean(v, axis=-1, keepdims=True)
        d = v - mean
        var = jnp.mean(d * d, axis=-1, keepdims=True)
        vn = (d * lax.rsqrt(var + LN_EPS) * lng_ref[...] + lnb_ref[...]).astype(BF16)
        f = [jnp.dot(ws[g], vn[:, g * LANES:(g + 1) * LANES], preferred_element_type=F32) for g in range(SG_GROUPS)]
        for g in range(SG_GROUPS):
            cols = pl.ds(g * LANES, LANES)
            o_ref[rows, cols] = (u_ref[rows, cols] * (f[g] + bs_ref[g]) * z_ref[rows, cols]).astype(BF16)


def _sg_mix(p, ln_g, ln_b, w_s, b_full, *, tm=256):
    n_tok = p.shape[0]
    return pl.pallas_call(
        functools.partial(_sg_mix_kernel, tm=tm),
        out_shape=jax.ShapeDtypeStruct((n_tok, D), BF16),
        grid=(n_tok // tm,),
        in_specs=[pl.BlockSpec((tm, D), lambda i: (i, 0)),
                  pl.BlockSpec((tm, D), lambda i: (i, 1)),
                  pl.BlockSpec((tm, D), lambda i: (i, 2)),
                  pl.BlockSpec((1, D), lambda i: (0, 0)),
                  pl.BlockSpec((1, D), lambda i: (0, 0)),
                  pl.BlockSpec((SG_GROUPS, SG_CHUNK, SG_CHUNK), lambda i: (0, 0, 0)),
                  pl.BlockSpec((SG_GROUPS, SG_CHUNK, LANES), lambda i: (0, 0, 0))],
        out_specs=pl.BlockSpec((tm, D), lambda i: (i, 0)),
        compiler_params=_cparams("parallel"),
        name="sg_mix",
    )(p, p, p, ln_g, ln_b, w_s, b_full)


def _first_half(shape):
    return (lax.broadcasted_iota(jnp.int32, shape, 1) % HEAD) < (HEAD // 2)


def _rope_table_kernel(pos_ref, invf_ref, cos_ref, sin_ref):
    ang = pos_ref[...].astype(F32) * invf_ref[...]
    cos_ref[...] = jnp.cos(ang)
    sin_ref[...] = jnp.where(_first_half(ang.shape), -jnp.sin(ang), jnp.sin(ang))


def _rope_tables(pos2, invf, *, tm=1024):
    n_tok = pos2.shape[0]
    spec = pl.BlockSpec((tm, LANES), lambda i: (i, 0))
    return pl.pallas_call(
        _rope_table_kernel,
        out_shape=(jax.ShapeDtypeStruct((n_tok, LANES), F32),) * 2,
        grid=(n_tok // tm,),
        in_specs=[pl.BlockSpec((tm, 1), lambda i: (i, 0)), pl.BlockSpec((1, LANES), lambda i: (0, 0))],
        out_specs=(spec, spec),
        compiler_params=_cparams("parallel"),
        name="rope_tables",
    )(pos2, invf)


def _rope(x, cos, sin_signed, first):
    partner = jnp.where(first, pltpu.roll(x, LANES - HEAD // 2, 1), pltpu.roll(x, HEAD // 2, 1))
    return x * cos + partner * sin_signed


def _swa_mix_kernel(sink_ref, p_ref, kvp_ref, cos_ref, sin_ref, cosp_ref, sinp_ref, o_ref):
    i = pl.program_id(0)
    first_key = jnp.where((i % (SEQ // SWA_BLOCK)) != 0, 0, SWA_BLOCK)
    cos, sin_s, cosp, sinp_s = cos_ref[...], sin_ref[...], cosp_ref[...], sinp_ref[...]
    first = _first_half(cos.shape)
    lane = lax.broadcasted_iota(jnp.int32, (2 * SWA_BLOCK, LANES), 1)
    low = lane < HEAD

    qi = lax.broadcasted_iota(jnp.int32, (SWA_BLOCK, 2 * SWA_BLOCK), 0)
    kj = lax.broadcasted_iota(jnp.int32, (SWA_BLOCK, 2 * SWA_BLOCK), 1)
    rel = qi + SWA_BLOCK - kj
    mask = (rel >= 0) & (rel < SWA_BLOCK) & (kj >= first_key)

    k_lo, k_hi, v_lo, v_hi = [], [], [], []
    for kp in range(SWA_KV // 2):
        ksl = pl.ds(kp * LANES, LANES)
        vsl = pl.ds(SWA_KVW + kp * LANES, LANES)
        k2 = jnp.concatenate([_rope(kvp_ref[:, ksl], cosp, sinp_s, first),
                              _rope(p_ref[:, pl.ds(D + kp * LANES, LANES)], cos, sin_s, first)], axis=0)
        v2 = jnp.concatenate([kvp_ref[:, vsl], p_ref[:, pl.ds(D + SWA_KVW + kp * LANES, LANES)]], axis=0)
        k2r = pltpu.roll(k2, HEAD, 1)
        v2r = pltpu.roll(v2, HEAD, 1)
        for sub in range(2):
            src_k, src_kr = (k2, k2r) if sub == 0 else (k2r, k2)
            src_v, src_vr = (v2, v2r) if sub == 0 else (v2r, v2)
            k_lo.append(jnp.where(low, src_k, 0.0).astype(BF16))
            k_hi.append(jnp.where(low, 0.0, src_kr).astype(BF16))
            v_lo.append(jnp.where(low, src_v, 0.0).astype(BF16))
            v_hi.append(jnp.where(low, 0.0, src_vr).astype(BF16))

    npair = SWA_HEADS // 2
    cols = [pl.ds(pr * LANES, LANES) for pr in range(npair)]
    q2 = [(_rope(p_ref[:, cols[pr]], cos, sin_s, first) * (HEAD ** -0.5)).astype(BF16) for pr in range(npair)]
    s = [lax.dot_general(q2[h // 2], (k_hi if h % 2 else k_lo)[h // 8], (((1,), (1,)), ((), ())),
                         preferred_element_type=F32) for h in range(SWA_HEADS)]
    hs = range(SWA_HEADS)
    s = [jnp.where(mask, s[h], -jnp.inf) for h in hs]
    m = [jnp.maximum(jnp.max(s[h], axis=-1, keepdims=True), sink_ref[h]) for h in hs]
    e = [jnp.exp(s[h] - m[h]) for h in hs]
    denom = [jnp.sum(e[h], axis=-1, keepdims=True) + jnp.exp(sink_ref[h] - m[h]) for h in hs]
    prob = [(e[h] * (1.0 / denom[h])).astype(BF16) for h in hs]
    o = [jnp.dot(prob[h], (v_hi if h % 2 else v_lo)[h // 8], preferred_element_type=F32) for h in range(SWA_HEADS)]
    zoff = D + 2 * SWA_KVW
    for pr in range(npair):
        gate = p_ref[:, pl.ds(zoff + pr * LANES, LANES)]
        o_ref[:, cols[pr]] = ((o[2 * pr] + o[2 * pr + 1]) * gate).astype(BF16)


def _swa_mix(p, cos_t, sin_t, sinks):
    n_tok, ncols = p.shape
    tb = SWA_BLOCK
    kvb = D // (2 * SWA_KVW)
    cur = pl.BlockSpec((tb, LANES), lambda i: (i, 0))
    prev = pl.BlockSpec((tb, LANES), lambda i: (jnp.maximum(i - 1, 0), 0))
    return pl.pallas_call(
        _swa_mix_kernel,
        out_shape=jax.ShapeDtypeStruct((n_tok, D), BF16),
        grid=(n_tok // tb,),
        in_specs=[pl.BlockSpec(memory_space=pltpu.SMEM),
                  pl.BlockSpec((tb, ncols), lambda i: (i, 0)),
                  pl.BlockSpec((tb, 2 * SWA_KVW), lambda i: (jnp.maximum(i - 1, 0), kvb)),
                  cur, cur, prev, prev],
        out_specs=pl.BlockSpec((tb, D), lambda i: (i, 0)),
        compiler_params=_cparams("parallel"),
        name="swa_mix",
    )(sinks, p, p, cos_t, sin_t, cos_t, sin_t)


def _split2(x):
    hi = x.astype(BF16)
    return hi, (x - hi.astype(F32)).astype(BF16)


def _head_sums(xs, seg):
    parts = []
    for x in xs:
        parts.extend(_split2(x))
    out = jnp.dot(jnp.concatenate(parts, axis=0), seg, preferred_element_type=F32)
    n = xs[0].shape[0]
    return [out[2 * i * n:(2 * i + 1) * n] + out[(2 * i + 1) * n:(2 * i + 2) * n] for i in range(len(xs))]


def _bd(x, same_head):
    t = jnp.concatenate([x] * RW_HG, axis=0)
    return jnp.where(same_head, t, 0.0).astype(BF16)


def _diag_blocks(o, lane_head):
    acc = o[0:HEAD]
    for h in range(1, RW_HG):
        acc = jnp.where(lane_head == h, o[h * HEAD:(h + 1) * HEAD], acc)
    return acc


def _dot(a, b):
    return jnp.dot(a.astype(BF16), b.astype(BF16), preferred_element_type=F32)


def _dot_nt(a, b):
    return lax.dot_general(a, b, (((1,), (1,)), ((), ())), preferred_element_type=F32)


def _dot_tn(a, b):
    return lax.dot_general(a, b, (((0,), (0,)), ((), ())), preferred_element_type=F32)


def _rwkv_groups(r, lw, cum, k, v, kap, bet, h0, masks):
    incl, strict, eye, same_head, lane_head = masks
    c = RW_CHUNK
    gr = range(len(r))
    bd = lambda t: _bd(t, same_head)
    cref = [cum[g][c // 2 - 1:c // 2] for g in gr]
    cend = [cum[g][c - 1:c] for g in gr]
    r_true = [r[g] * jnp.exp(cum[g]) for g in gr]
    kap_true = [kap[g] * jnp.exp(cum[g] - lw[g]) for g in gr]
    r_s = [r[g] * jnp.exp(cum[g] - cref[g]) for g in gr]
    kap_s = [kap[g] * jnp.exp(cum[g] - lw[g] - cref[g]) for g in gr]
    ginv = [jnp.exp(cref[g] - cum[g]) for g in gr]
    eend = [jnp.exp(cend[g] - cum[g]) for g in gr]

    s = [_dot_nt(jnp.concatenate([r_s[g], kap_s[g]], axis=0).astype(BF16),
                 jnp.concatenate([bd(k[g] * ginv[g]), bd(bet[g] * ginv[g])], axis=0)) for g in gr]
    a_rk = [jnp.where(incl, s[g][:c, :RW_GW], 0.0) for g in gr]
    a_rb = [jnp.where(incl, s[g][:c, RW_GW:], 0.0) for g in gr]
    a_kk = [jnp.where(strict, s[g][c:, :RW_GW], 0.0) for g in gr]
    a_kb = [jnp.where(strict, s[g][c:, RW_GW:], 0.0) for g in gr]

    x = [eye - a_kb[g] for g in gr]
    p = [_dot(a_kb[g], bd(a_kb[g])) for g in gr]
    akv = [_dot(a_kk[g], bd(v[g])) for g in gr]
    for _ in range(4):
        xp = [_dot(jnp.concatenate([x[g], p[g]], axis=0), bd(p[g])) for g in gr]
        x = [x[g] + xp[g][:c] for g in gr]
        p = [xp[g][c:] for g in gr]
    t_inv = [x[g] + _dot(x[g], bd(p[g])) for g in gr]

    wu = [-_dot(t_inv[g], jnp.concatenate([bd(kap_true[g]), bd(akv[g])], axis=1)) for g in gr]

    hw = [_dot(jnp.concatenate([wu[g][:, :RW_GW], r_true[g], eye * jnp.exp(cend[g])], axis=0), bd(h0[g]))
          for g in gr]
    u = [hw[g][:c] + wu[g][:, RW_GW:] for g in gr]
    y = [hw[g][c:2 * c] + _dot(jnp.concatenate([a_rk[g], a_rb[g]], axis=1),
                               jnp.concatenate([bd(v[g]), bd(u[g])], axis=0)) for g in gr]
    o2 = [_dot_tn(jnp.concatenate([k[g] * eend[g], bet[g] * eend[g]], axis=0).astype(BF16),
                  jnp.concatenate([v[g], u[g]], axis=0).astype(BF16)) for g in gr]
    h_new = [hw[g][2 * c:] + _diag_blocks(o2[g], lane_head) for g in gr]
    return y, h_new


def _rwkv_mix_kernel(p_ref, pp_ref, pl_ref, plp_ref, mu_ref, mul_ref, w0_ref, a0_ref, wl_ref, al_ref,
                     kk_ref, ka_ref, rk_ref, gng_ref, gnb_ref, o_ref, h_scr):
    ci = pl.program_id(1)
    c = RW_CHUNK

    @pl.when(ci == 0)
    def _():
        h_scr[...] = jnp.zeros_like(h_scr)

    not_first = (ci != 0).astype(F32)
    row = lax.broadcasted_iota(jnp.int32, (c, 1), 0)

    def shifted(cur, prev_ref):
        prev_row = prev_ref[7:8, :] * not_first
        rolled = pltpu.roll(cur, 1, 0)
        return jnp.where(row == 0, prev_row, rolled)

    lcur = pl_ref[...]
    lcur = lcur + (shifted(lcur, plp_ref) - lcur) * mul_ref[...]
    dw = jnp.tanh(lcur[:, :RW_LP // 2])
    da = lcur[:, RW_LP // 2:]

    ti = lax.broadcasted_iota(jnp.int32, (c, RW_GW), 0)
    li = lax.broadcasted_iota(jnp.int32, (c, RW_GW), 1)
    si = li % HEAD
    lane_head = li // HEAD
    rb = lax.broadcasted_iota(jnp.int32, (RW_GW, RW_GW), 0) // HEAD
    cb = lax.broadcasted_iota(jnp.int32, (RW_GW, RW_GW), 1) // HEAD
    same_head = rb == cb
    seg = same_head.astype(BF16)
    masks = (ti >= si, ti > si, (ti == si).astype(F32), same_head, lane_head)
    tri = (lax.broadcasted_iota(jnp.int32, (c, c), 0) >= lax.broadcasted_iota(jnp.int32, (c, c), 1)).astype(BF16)

    gr = range(RW_GROUPS)
    gsl = [pl.ds(g * RW_GW, RW_GW) for g in gr]

    def lerp(off, g):
        cols = pl.ds(off + g * RW_GW, RW_GW)
        cur = p_ref[:, cols]
        return cur + (shifted(cur, pp_ref.at[:, cols]) - cur) * mu_ref[:, cols]

    r = [lerp(0, g) for g in gr]
    k = [lerp(D, g) for g in gr]
    v = [lerp(2 * D, g) for g in gr]
    lw = [-DECAY_SCALE * jax.nn.sigmoid(w0_ref[:, gsl[g]] + _dot(dw, wl_ref[:, gsl[g]])) for g in gr]
    a = [jax.nn.sigmoid(a0_ref[:, gsl[g]] + _dot(da, al_ref[:, gsl[g]])) for g in gr]
    kap = [k[g] * kk_ref[:, gsl[g]] for g in gr]
    k = [k[g] * (1.0 + (a[g] - 1.0) * ka_ref[:, gsl[g]]) for g in gr]
    sums = _head_sums([kap[g] * kap[g] for g in gr] + [r[g] * k[g] * rk_ref[:, gsl[g]] for g in gr], seg)
    kap = [kap[g] / jnp.maximum(jnp.sqrt(sums[g]), 1e-12) for g in gr]
    bonus = sums[RW_GROUPS:]
    bet = [kap[g] * a[g] for g in gr]
    lw_s = [_split2(lw[g]) for g in gr]
    cum = [jnp.dot(tri, lw_s[g][0], preferred_element_type=F32) + jnp.dot(tri, lw_s[g][1], preferred_element_type=F32)
           for g in gr]

    y, h_new = [], []
    for lo in range(0, RW_GROUPS, RW_STAGE):
        sl = slice(lo, lo + RW_STAGE)
        ys, hs = _rwkv_groups(r[sl], lw[sl], cum[sl], k[sl], v[sl], kap[sl], bet[sl],
                              [h_scr[g] for g in range(lo, lo + RW_STAGE)], masks)
        y += ys
        h_new += hs
    for g in gr:
        h_scr[g] = h_new[g]

    ysum = _head_sums(y, seg)
    dlt = [y[g] - ysum[g] * (1.0 / HEAD) for g in gr]
    dsum = _head_sums([dlt[g] * dlt[g] for g in gr], seg)
    for g in gr:
        rstd = lax.rsqrt(dsum[g] * (1.0 / HEAD) + GN_EPS)
        yn = dlt[g] * rstd * gng_ref[:, gsl[g]] + gnb_ref[:, gsl[g]] + bonus[g] * v[g]
        o_ref[:, gsl[g]] = (yn * jax.nn.silu(lerp(3 * D, g))).astype(BF16)


def _rwkv_mix(p, p_lora, mu, mu_l, w0, a0, w_lora, a_lora, k_k, k_a, r_k, gn_g, gn_b):
    n_tok = p.shape[0]
    c = RW_CHUNK
    nc = SEQ // c
    ncol = p.shape[1]
    nl = p_lora.shape[1]
    row = lambda b, ci: (b * nc + ci, 0)
    prev = lambda b, ci: (jnp.maximum((b * nc + ci) * (c // 8) - 1, 0), 0)
    vec = pl.BlockSpec((1, D), lambda b, ci: (0, 0))
    lora = pl.BlockSpec((RW_LP // 2, D), lambda b, ci: (0, 0))
    return pl.pallas_call(
        _rwkv_mix_kernel,
        out_shape=jax.ShapeDtypeStruct((n_tok, D), BF16),
        grid=(n_tok // SEQ, nc),
        in_specs=[pl.BlockSpec((c, ncol), row),
                  pl.BlockSpec((8, ncol), prev),
                  pl.BlockSpec((c, nl), row),
                  pl.BlockSpec((8, nl), prev),
                  pl.BlockSpec((1, ncol), lambda b, ci: (0, 0)),
                  pl.BlockSpec((1, nl), lambda b, ci: (0, 0)),
                  vec, vec, lora, lora,
                  vec, vec, vec, vec, vec],
        out_specs=pl.BlockSpec((c, D), row),
        scratch_shapes=[pltpu.VMEM((RW_GROUPS, HEAD, RW_GW), F32)],
        compiler_params=_cparams("parallel", "arbitrary"),
        name="rwkv_mix",
    )(p, p, p_lora, p_lora, mu, mu_l, w0, a0, w_lora, a_lora, k_k, k_a, r_k, gn_g, gn_b)


def kernel(x, c, positions, norm_g, mod_w, mod_b, final_norm_g, sg_w_in, sg_w_out, sg_ln_g, sg_ln_b, sg_w_spatial,
           sg_b_spatial, swa_w_in, swa_w_out, swa_sinks, rwkv_w_in, rwkv_w_out, rwkv_mu, rwkv_w0, rwkv_w_lora,
           rwkv_a0, rwkv_a_lora, rwkv_k_k, rwkv_k_a, rwkv_r_k, rwkv_gn_g, rwkv_gn_b):
    batch, seq, d = x.shape
    assert (seq, d) == (SEQ, D) and norm_g.shape[0] == DEPTH
    n_tok = batch * seq
    x2 = x.reshape(n_tok, D)
    c8 = jnp.zeros((8, D), F32).at[:batch].set(c)
    mod = _modulation(c8, mod_w, mod_b)
    pos2 = positions.reshape(n_tok, 1)
    half = HEAD // 2
    inv_freq = ROPE_THETA ** (-jnp.arange(half, dtype=F32) / half)
    invf = jnp.tile(inv_freq, LANES // half).reshape(1, LANES)
    row = lambda a: a.reshape(1, -1)

    def act_rows(spans):
        rows = [jnp.concatenate([jnp.full((w,), val[t], F32) for w, *val in spans]) for t in range(3)]
        return jnp.zeros((8, rows[0].shape[0]), F32).at[:3].set(jnp.stack(rows))

    gelu_span = (GELU_A, GELU_B, 1.0)
    silu_span = (0.5, 0.0, 1.0)
    sg_in_bf, swa_in_bf, rwkv_in_bf = sg_w_in.astype(BF16), swa_w_in.astype(BF16), rwkv_w_in.astype(BF16)
    w_out_bf = (sg_w_out.astype(BF16), swa_w_out.astype(BF16), rwkv_w_out.astype(BF16))
    cos_t, sin_t = _rope_tables(pos2, invf)
    for i in range(DEPTH):
        kind, j = i % 3, i // 3
        mod3 = mod[i, :batch].reshape(batch, 1, 3 * D)
        g = row(norm_g[i])
        if kind == 0:
            act = act_rows([(2 * D, *gelu_span), (D, *silu_span)])
            p = _in_proj(x2, g, mod3, sg_in_bf, j, 3 * D, act, select=False)
            mix = _sg_mix(p, row(sg_ln_g[j]), row(sg_ln_b[j]), sg_w_spatial[j],
                          jnp.broadcast_to(sg_b_spatial[j][:, :, None], (SG_GROUPS, SG_CHUNK, LANES)))
        elif kind == 1:
            act = act_rows([(D + 2 * SWA_KVW, 0.0, 0.0, 0.0), (D, *silu_span)])
            p = _in_proj(x2, g, mod3, swa_in_bf, j, 2 * D + 2 * SWA_KVW, act)
            mix = _swa_mix(p, cos_t, sin_t, swa_sinks[j])
        else:
            lp = RW_LP // 2 - RW_LORA
            w_in = rwkv_w_in[j]
            w_l = jnp.concatenate([jnp.pad(w_in[:, 4 * D:4 * D + RW_LORA], ((0, 0), (0, lp))),
                                   jnp.pad(w_in[:, 4 * D + RW_LORA:], ((0, 0), (0, lp)))], axis=1).astype(BF16)
            p, p_lora = _in_proj(x2, g, mod3, rwkv_in_bf, j, 4 * D, w_lora_bf=w_l)
            mu = rwkv_mu[j]
            mu_l = jnp.concatenate([jnp.pad(mu[4 * D:4 * D + RW_LORA], (0, lp)), jnp.pad(mu[4 * D + RW_LORA:], (0, lp))])
            wl_pad = jnp.pad(rwkv_w_lora[j], ((0, lp), (0, 0))).astype(BF16)
            al_pad = jnp.pad(rwkv_a_lora[j], ((0, lp), (0, 0))).astype(BF16)
            mix = _rwkv_mix(p, p_lora, row(mu[:4 * D]), row(mu_l), row(rwkv_w0[j]),
                            row(rwkv_a0[j]), wl_pad, al_pad, row(rwkv_k_k[j]),
                            row(rwkv_k_a[j]), row(rwkv_r_k[j]), row(rwkv_gn_g[j]), row(rwkv_gn_b[j]))
        x2 = _out_proj(mix, w_out_bf[kind], j, x2, mod3, row(final_norm_g) if i == DEPTH - 1 else None)
    return x2.reshape(batch, seq, D)
```

```python
import functools
import math

import jax
import jax.numpy as jnp
from jax import lax
from jax.experimental import pallas as pl
from jax.experimental.pallas import tpu as pltpu

F32 = jnp.float32
BF16 = jnp.bfloat16

D = 2048
SEQ = 4096
DEPTH = 4
HEAD = 64
LANES = 128
SG_CHUNK = 128
SG_GROUPS = 16
SWA_HEADS = D // HEAD
SWA_KV = SWA_HEADS // 8
SWA_KVW = SWA_KV * HEAD
SWA_BLOCK = 128
ROPE_THETA = 10000.0
RW_HEADS = D // HEAD
RW_LORA = 96
RW_LP = 256
RW_CHUNK = 64
RW_HG = 4
RW_GW = RW_HG * HEAD
RW_GROUPS = D // RW_GW
RW_STAGE = 8
DECAY_SCALE = math.exp(-0.5)
GN_EPS = 64e-5
RMS_EPS = 1e-6
LN_EPS = 1e-5

VMEM_LIMIT = 48 * 1024 * 1024


def _cparams(*sem):
    return pltpu.CompilerParams(dimension_semantics=sem, vmem_limit_bytes=VMEM_LIMIT)


def _mod_kernel(c_ref, w_ref, b_ref, o_ref):
    cond = jax.nn.silu(c_ref[...]).astype(BF16)
    o_ref[0] = jnp.dot(cond, w_ref[0].astype(BF16), preferred_element_type=F32) + b_ref[0]


def _modulation(c8, mod_w, mod_b):
    tn = 1024
    n = mod_w.shape[2]
    return pl.pallas_call(
        _mod_kernel,
        out_shape=jax.ShapeDtypeStruct((DEPTH, 8, n), F32),
        grid=(DEPTH, n // tn),
        in_specs=[pl.BlockSpec((8, D), lambda l, j: (0, 0)),
                  pl.BlockSpec((1, D, tn), lambda l, j: (l, 0, j)),
                  pl.BlockSpec((1, 1, tn), lambda l, j: (l, 0, j))],
        out_specs=pl.BlockSpec((1, 8, tn), lambda l, j: (l, 0, j)),
        compiler_params=_cparams("parallel", "parallel"),
        name="modulation",
    )(c8, mod_w, mod_b.reshape(DEPTH, 1, n))


NORM_ROWS = 128


def _row_rsqrt(x_ref, rs_scr):
    def body(i, carry):
        rows = pl.ds(pl.multiple_of(i * NORM_ROWS, NORM_ROWS), NORM_ROWS)
        x = x_ref[rows, :]
        rs_scr[rows, :] = lax.rsqrt(jnp.mean(x * x, axis=-1, keepdims=True) + RMS_EPS)
        return carry
    lax.fori_loop(0, x_ref.shape[0] // NORM_ROWS, body, 0)


def _modulated(x_ref, rs_scr, g_ref, sc_ref, sh_ref):
    gain = g_ref[...] * (1.0 + sc_ref[0])
    return (x_ref[...] * rs_scr[...] * gain + sh_ref[0]).astype(BF16)


GELU_A = math.sqrt(2.0 / math.pi)
GELU_B = GELU_A * 0.044715
ACT_COLS = 256


def _in_proj_act_kernel(x_ref, g_ref, sc_ref, sh_ref, w_ref, act_ref, o_ref, rs_scr, *, select):
    @pl.when(pl.program_id(1) == 0)
    def _():
        _row_rsqrt(x_ref, rs_scr)

    h = _modulated(x_ref, rs_scr, g_ref, sc_ref, sh_ref)
    for n in range(o_ref.shape[1] // ACT_COLS):
        cols = pl.ds(n * ACT_COLS, ACT_COLS)
        p = jnp.dot(h, w_ref[:, cols].astype(BF16), preferred_element_type=F32)
        gated = (0.5 * p) * (1.0 + jnp.tanh(p * (act_ref[0:1, cols] + act_ref[1:2, cols] * (p * p))))
        o_ref[:, cols] = (jnp.where(act_ref[2:3, cols] > 0.0, gated, p) if select else gated).astype(o_ref.dtype)


def _in_proj_lora_kernel(x_ref, g_ref, sc_ref, sh_ref, w_ref, wl_ref, o_ref, ol_ref, rs_scr):
    @pl.when(pl.program_id(1) == 0)
    def _():
        _row_rsqrt(x_ref, rs_scr)
        ol_ref[...] = jnp.dot(_modulated(x_ref, rs_scr, g_ref, sc_ref, sh_ref), wl_ref[...],
                              preferred_element_type=F32)

    h = _modulated(x_ref, rs_scr, g_ref, sc_ref, sh_ref)
    o_ref[...] = jnp.dot(h, w_ref[...].astype(BF16), preferred_element_type=F32)


def _in_proj_act(x2, g, mod3, w_bf, wl, ncols, act, select, out_dtype, *, tm=1024):
    n_tok = x2.shape[0]
    tn = 1024 if ncols % 1024 == 0 else 768
    assert ncols % tn == 0
    per_b = SEQ // tm
    return pl.pallas_call(
        functools.partial(_in_proj_act_kernel, select=select),
        out_shape=jax.ShapeDtypeStruct((n_tok, ncols), out_dtype),
        grid=(n_tok // tm, ncols // tn),
        in_specs=[pl.BlockSpec((tm, D), lambda i, j: (i, 0)),
                  pl.BlockSpec((1, D), lambda i, j: (0, 0)),
                  pl.BlockSpec((1, 1, D), lambda i, j: (i // per_b, 0, 1)),
                  pl.BlockSpec((1, 1, D), lambda i, j: (i // per_b, 0, 0)),
                  pl.BlockSpec((None, D, tn), lambda i, j: (wl, 0, j)),
                  pl.BlockSpec((8, tn), lambda i, j: (0, j))],
        out_specs=pl.BlockSpec((tm, tn), lambda i, j: (i, j)),
        scratch_shapes=[pltpu.VMEM((tm, 1), F32)],
        compiler_params=_cparams("parallel", "arbitrary"),
        name="in_proj_act",
    )(x2, g, mod3, mod3, w_bf, act)


def _in_proj(x2, g, mod3, w_bf, wl, ncols, w_lora_bf, *, tm=1024):
    n_tok = x2.shape[0]
    tn = 1024
    assert ncols % tn == 0
    per_b = SEQ // tm
    x_spec = pl.BlockSpec((tm, D), lambda i, j: (i, 0))
    g_spec = pl.BlockSpec((1, D), lambda i, j: (0, 0))
    sh_spec = pl.BlockSpec((1, 1, D), lambda i, j: (i // per_b, 0, 0))
    sc_spec = pl.BlockSpec((1, 1, D), lambda i, j: (i // per_b, 0, 1))
    w_spec = pl.BlockSpec((None, D, tn), lambda i, j: (wl, 0, j))
    o_spec = pl.BlockSpec((tm, tn), lambda i, j: (i, j))
    grid = (n_tok // tm, ncols // tn)
    scratch = [pltpu.VMEM((tm, 1), F32)]
    nl = w_lora_bf.shape[1]
    return pl.pallas_call(
        _in_proj_lora_kernel,
        out_shape=(jax.ShapeDtypeStruct((n_tok, ncols), F32), jax.ShapeDtypeStruct((n_tok, nl), F32)),
        grid=grid,
        in_specs=[x_spec, g_spec, sc_spec, sh_spec, w_spec, pl.BlockSpec((D, nl), lambda i, j: (0, 0))],
        out_specs=(o_spec, pl.BlockSpec((tm, nl), lambda i, j: (i, 0))),
        scratch_shapes=scratch,
        compiler_params=_cparams("parallel", "arbitrary"),
        name="in_proj_lora",
    )(x2, g, mod3, mod3, w_bf, w_lora_bf)


def _out_proj_kernel(m_ref, w_ref, x_ref, gate_ref, o_ref):
    y = jnp.dot(m_ref[...], w_ref[...], preferred_element_type=F32)
    o_ref[...] = x_ref[...] + gate_ref[0] * y


def _out_proj_final_kernel(m_ref, w_ref, x_ref, gate_ref, fg_ref, o_ref):
    y = jnp.dot(m_ref[...], w_ref[...], preferred_element_type=F32)
    xn = x_ref[...] + gate_ref[0] * y
    o_ref[...] = xn * lax.rsqrt(jnp.mean(xn * xn, axis=-1, keepdims=True) + RMS_EPS) * fg_ref[...]


def _out_proj(mix, w_bf, wl, x2, mod3, final_g=None, *, tm=512):
    n_tok = x2.shape[0]
    per_b = SEQ // tm
    in_specs = [pl.BlockSpec((tm, D), lambda i: (i, 0)),
                pl.BlockSpec((None, D, D), lambda i: (wl, 0, 0)),
                pl.BlockSpec((tm, D), lambda i: (i, 0)),
                pl.BlockSpec((1, 1, D), lambda i: (i // per_b, 0, 2))]
    args = [mix, w_bf, x2, mod3]
    kern = _out_proj_kernel
    if final_g is not None:
        in_specs.append(pl.BlockSpec((1, D), lambda i: (0, 0)))
        args.append(final_g)
        kern = _out_proj_final_kernel
    return pl.pallas_call(
        kern,
        out_shape=jax.ShapeDtypeStruct((n_tok, D), F32),
        grid=(n_tok // tm,),
        in_specs=in_specs,
        out_specs=pl.BlockSpec((tm, D), lambda i: (i, 0)),
        compiler_params=_cparams("parallel"),
        name="out_proj_final" if final_g is not None else "out_proj",
    )(*args)


def _sg_mix_kernel(u_ref, v_ref, z_ref, lng_ref, lnb_ref, ws_ref, bs_ref, o_ref, *, tm):
    ti = lax.broadcasted_iota(jnp.int32, (SG_CHUNK, SG_CHUNK), 0)
    si = lax.broadcasted_iota(jnp.int32, (SG_CHUNK, SG_CHUNK), 1)
    causal = ti >= si
    ws = [jnp.where(causal, ws_ref[g], 0.0).astype(BF16) for g in range(SG_GROUPS)]
    for c in range(tm // SG_CHUNK):
        rows = pl.ds(c * SG_CHUNK, SG_CHUNK)
        v = v_ref[rows, :].astype(F32)
        mean = jnp.mean(v, axis=-1, keepdims=True)
        d = v - mean
        var = jnp.mean(d * d, axis=-1, keepdims=True)
        vn = (d * lax.rsqrt(var + LN_EPS) * lng_ref[...] + lnb_ref[...]).astype(BF16)
        f = [jnp.dot(ws[g], vn[:, g * LANES:(g + 1) * LANES], preferred_element_type=F32) for g in range(SG_GROUPS)]
        for g in range(SG_GROUPS):
            cols = pl.ds(g * LANES, LANES)
            uz = u_ref[rows, cols].astype(F32) * z_ref[rows, cols].astype(F32)
            o_ref[rows, cols] = (uz * (f[g] + bs_ref[g])).astype(BF16)


def _sg_mix(p, ln_g, ln_b, w_s, b_full, *, tm=256):
    n_tok = p.shape[0]
    return pl.pallas_call(
        functools.partial(_sg_mix_kernel, tm=tm),
        out_shape=jax.ShapeDtypeStruct((n_tok, D), BF16),
        grid=(n_tok // tm,),
        in_specs=[pl.BlockSpec((tm, D), lambda i: (i, 0)),
                  pl.BlockSpec((tm, D), lambda i: (i, 1)),
                  pl.BlockSpec((tm, D), lambda i: (i, 2)),
                  pl.BlockSpec((1, D), lambda i: (0, 0)),
                  pl.BlockSpec((1, D), lambda i: (0, 0)),
                  pl.BlockSpec((SG_GROUPS, SG_CHUNK, SG_CHUNK), lambda i: (0, 0, 0)),
                  pl.BlockSpec((SG_GROUPS, SG_CHUNK, LANES), lambda i: (0, 0, 0))],
        out_specs=pl.BlockSpec((tm, D), lambda i: (i, 0)),
        compiler_params=_cparams("parallel"),
        name="sg_mix",
    )(p, p, p, ln_g, ln_b, w_s, b_full)


def _first_half(shape):
    return (lax.broadcasted_iota(jnp.int32, shape, 1) % HEAD) < (HEAD // 2)


def _rope_table_kernel(pos_ref, invf_ref, cos_ref, sin_ref):
    ang = pos_ref[...].astype(F32) * invf_ref[...]
    cos_ref[...] = jnp.cos(ang)
    sin_ref[...] = jnp.where(_first_half(ang.shape), -jnp.sin(ang), jnp.sin(ang))


def _rope_tables(pos2, invf, *, tm=1024):
    n_tok = pos2.shape[0]
    spec = pl.BlockSpec((tm, LANES), lambda i: (i, 0))
    return pl.pallas_call(
        _rope_table_kernel,
        out_shape=(jax.ShapeDtypeStruct((n_tok, LANES), F32),) * 2,
        grid=(n_tok // tm,),
        in_specs=[pl.BlockSpec((tm, 1), lambda i: (i, 0)), pl.BlockSpec((1, LANES), lambda i: (0, 0))],
        out_specs=(spec, spec),
        compiler_params=_cparams("parallel"),
        name="rope_tables",
    )(pos2, invf)


def _rope(x, cos, sin_signed, first):
    partner = jnp.where(first, pltpu.roll(x, LANES - HEAD // 2, 1), pltpu.roll(x, HEAD // 2, 1))
    return x * cos + partner * sin_signed


def _swa_mix_kernel(sink_ref, p_ref, kvp_ref, cos_ref, sin_ref, cosp_ref, sinp_ref, o_ref):
    i = pl.program_id(0)
    first_key = jnp.where((i % (SEQ // SWA_BLOCK)) != 0, 0, SWA_BLOCK)
    cos, sin_s, cosp, sinp_s = cos_ref[...], sin_ref[...], cosp_ref[...], sinp_ref[...]
    first = _first_half(cos.shape)
    lane = lax.broadcasted_iota(jnp.int32, (2 * SWA_BLOCK, LANES), 1)
    low = lane < HEAD

    qi = lax.broadcasted_iota(jnp.int32, (SWA_BLOCK, 2 * SWA_BLOCK), 0)
    kj = lax.broadcasted_iota(jnp.int32, (SWA_BLOCK, 2 * SWA_BLOCK), 1)
    rel = qi + SWA_BLOCK - kj
    mask = (rel >= 0) & (rel < SWA_BLOCK) & (kj >= first_key)

    k_lo, k_hi, v_lo, v_hi = [], [], [], []
    for kp in range(SWA_KV // 2):
        ksl = pl.ds(kp * LANES, LANES)
        vsl = pl.ds(SWA_KVW + kp * LANES, LANES)
        k2 = jnp.concatenate([_rope(kvp_ref[:, ksl], cosp, sinp_s, first),
                              _rope(p_ref[:, pl.ds(D + kp * LANES, LANES)], cos, sin_s, first)], axis=0)
        v2 = jnp.concatenate([kvp_ref[:, vsl], p_ref[:, pl.ds(D + SWA_KVW + kp * LANES, LANES)]], axis=0)
        k2r = pltpu.roll(k2, HEAD, 1)
        v2r = pltpu.roll(v2, HEAD, 1)
        for sub in range(2):
            src_k, src_kr = (k2, k2r) if sub == 0 else (k2r, k2)
            src_v, src_vr = (v2, v2r) if sub == 0 else (v2r, v2)
            k_lo.append(jnp.where(low, src_k, 0.0).astype(BF16))
            k_hi.append(jnp.where(low, 0.0, src_kr).astype(BF16))
            v_lo.append(jnp.where(low, src_v, 0.0).astype(BF16))
            v_hi.append(jnp.where(low, 0.0, src_vr).astype(BF16))

    npair = SWA_HEADS // 2
    cols = [pl.ds(pr * LANES, LANES) for pr in range(npair)]
    q2 = [(_rope(p_ref[:, cols[pr]], cos, sin_s, first) * (HEAD ** -0.5)).astype(BF16) for pr in range(npair)]
    s = [lax.dot_general(q2[h // 2], (k_hi if h % 2 else k_lo)[h // 8], (((1,), (1,)), ((), ())),
                         preferred_element_type=F32) for h in range(SWA_HEADS)]
    hs = range(SWA_HEADS)
    s = [jnp.where(mask, s[h], -jnp.inf) for h in hs]
    m = [jnp.maximum(jnp.max(s[h], axis=-1, keepdims=True), sink_ref[h]) for h in hs]
    e = [jnp.exp(s[h] - m[h]) for h in hs]
    denom = [jnp.sum(e[h], axis=-1, keepdims=True) + jnp.exp(sink_ref[h] - m[h]) for h in hs]
    prob = [(e[h] * (1.0 / denom[h])).astype(BF16) for h in hs]
    o = [jnp.dot(prob[h], (v_hi if h % 2 else v_lo)[h // 8], preferred_element_type=F32) for h in range(SWA_HEADS)]
    zoff = D + 2 * SWA_KVW
    for pr in range(npair):
        gate = p_ref[:, pl.ds(zoff + pr * LANES, LANES)]
        o_ref[:, cols[pr]] = ((o[2 * pr] + o[2 * pr + 1]) * gate).astype(BF16)


def _swa_mix(p, cos_t, sin_t, sinks):
    n_tok, ncols = p.shape
    tb = SWA_BLOCK
    kvb = D // (2 * SWA_KVW)
    cur = pl.BlockSpec((tb, LANES), lambda i: (i, 0))
    prev = pl.BlockSpec((tb, LANES), lambda i: (jnp.maximum(i - 1, 0), 0))
    return pl.pallas_call(
        _swa_mix_kernel,
        out_shape=jax.ShapeDtypeStruct((n_tok, D), BF16),
        grid=(n_tok // tb,),
        in_specs=[pl.BlockSpec(memory_space=pltpu.SMEM),
                  pl.BlockSpec((tb, ncols), lambda i: (i, 0)),
                  pl.BlockSpec((tb, 2 * SWA_KVW), lambda i: (jnp.maximum(i - 1, 0), kvb)),
                  cur, cur, prev, prev],
        out_specs=pl.BlockSpec((tb, D), lambda i: (i, 0)),
        compiler_params=_cparams("parallel"),
        name="swa_mix",
    )(sinks, p, p, cos_t, sin_t, cos_t, sin_t)


def _split2(x):
    hi = x.astype(BF16)
    return hi, (x - hi.astype(F32)).astype(BF16)


def _head_sums(xs, seg):
    parts = []
    for x in xs:
        parts.extend(_split2(x))
    out = jnp.dot(jnp.concatenate(parts, axis=0), seg, preferred_element_type=F32)
    n = xs[0].shape[0]
    return [out[2 * i * n:(2 * i + 1) * n] + out[(2 * i + 1) * n:(2 * i + 2) * n] for i in range(len(xs))]


def _bd(x, same_head):
    t = jnp.concatenate([x] * RW_HG, axis=0)
    return jnp.where(same_head, t, 0.0).astype(BF16)


def _diag_blocks(o, lane_head):
    acc = o[0:HEAD]
    for h in range(1, RW_HG):
        acc = jnp.where(lane_head == h, o[h * HEAD:(h + 1) * HEAD], acc)
    return acc


def _dot(a, b):
    return jnp.dot(a.astype(BF16), b.astype(BF16), preferred_element_type=F32)


def _dot_nt(a, b):
    return lax.dot_general(a, b, (((1,), (1,)), ((), ())), preferred_element_type=F32)


def _dot_tn(a, b):
    return lax.dot_general(a, b, (((0,), (0,)), ((), ())), preferred_element_type=F32)


def _rwkv_groups(r, lw, cum, k, v, kap, bet, h0, masks):
    incl, strict, eye, same_head, lane_head = masks
    c = RW_CHUNK
    gr = range(len(r))
    bd = lambda t: _bd(t, same_head)
    cref = [cum[g][c // 2 - 1:c // 2] for g in gr]
    cend = [cum[g][c - 1:c] for g in gr]
    r_true = [r[g] * jnp.exp(cum[g]) for g in gr]
    kap_true = [kap[g] * jnp.exp(cum[g] - lw[g]) for g in gr]
    r_s = [r[g] * jnp.exp(cum[g] - cref[g]) for g in gr]
    kap_s = [kap[g] * jnp.exp(cum[g] - lw[g] - cref[g]) for g in gr]
    ginv = [jnp.exp(cref[g] - cum[g]) for g in gr]
    eend = [jnp.exp(cend[g] - cum[g]) for g in gr]

    s = [_dot_nt(jnp.concatenate([r_s[g], kap_s[g]], axis=0).astype(BF16),
                 jnp.concatenate([bd(k[g] * ginv[g]), bd(bet[g] * ginv[g])], axis=0)) for g in gr]
    a_rk = [jnp.where(incl, s[g][:c, :RW_GW], 0.0) for g in gr]
    a_rb = [jnp.where(incl, s[g][:c, RW_GW:], 0.0) for g in gr]
    a_kk = [jnp.where(strict, s[g][c:, :RW_GW], 0.0) for g in gr]
    a_kb = [jnp.where(strict, s[g][c:, RW_GW:], 0.0) for g in gr]

    x = [eye - a_kb[g] for g in gr]
    p = [_dot(a_kb[g], bd(a_kb[g])) for g in gr]
    akv = [_dot(a_kk[g], bd(v[g])) for g in gr]
    for _ in range(4):
        xp = [_dot(jnp.concatenate([x[g], p[g]], axis=0), bd(p[g])) for g in gr]
        x = [x[g] + xp[g][:c] for g in gr]
        p = [xp[g][c:] for g in gr]
    t_inv = [x[g] + _dot(x[g], bd(p[g])) for g in gr]

    wu = [-_dot(t_inv[g], jnp.concatenate([bd(kap_true[g]), bd(akv[g])], axis=1)) for g in gr]

    hw = [_dot(jnp.concatenate([wu[g][:, :RW_GW], r_true[g], eye * jnp.exp(cend[g])], axis=0), bd(h0[g]))
          for g in gr]
    u = [hw[g][:c] + wu[g][:, RW_GW:] for g in gr]
    y = [hw[g][c:2 * c] + _dot(jnp.concatenate([a_rk[g], a_rb[g]], axis=1),
                               jnp.concatenate([bd(v[g]), bd(u[g])], axis=0)) for g in gr]
    o2 = [_dot_tn(jnp.concatenate([k[g] * eend[g], bet[g] * eend[g]], axis=0).astype(BF16),
                  jnp.concatenate([v[g], u[g]], axis=0).astype(BF16)) for g in gr]
    h_new = [hw[g][2 * c:] + _diag_blocks(o2[g], lane_head) for g in gr]
    return y, h_new


def _rwkv_mix_kernel(p_ref, pp_ref, pl_ref, plp_ref, mu_ref, mul_ref, w0_ref, a0_ref, wl_ref, al_ref,
                     kk_ref, ka_ref, rk_ref, gng_ref, gnb_ref, o_ref, h_scr):
    ci = pl.program_id(1)
    c = RW_CHUNK

    @pl.when(ci == 0)
    def _():
        h_scr[...] = jnp.zeros_like(h_scr)

    not_first = (ci != 0).astype(F32)
    row = lax.broadcasted_iota(jnp.int32, (c, 1), 0)

    def shifted(cur, prev_ref):
        prev_row = prev_ref[7:8, :] * not_first
        rolled = pltpu.roll(cur, 1, 0)
        return jnp.where(row == 0, prev_row, rolled)

    lcur = pl_ref[...]
    lcur = lcur + (shifted(lcur, plp_ref) - lcur) * mul_ref[...]
    dw = jnp.tanh(lcur[:, :RW_LP // 2])
    da = lcur[:, RW_LP // 2:]

    ti = lax.broadcasted_iota(jnp.int32, (c, RW_GW), 0)
    li = lax.broadcasted_iota(jnp.int32, (c, RW_GW), 1)
    si = li % HEAD
    lane_head = li // HEAD
    rb = lax.broadcasted_iota(jnp.int32, (RW_GW, RW_GW), 0) // HEAD
    cb = lax.broadcasted_iota(jnp.int32, (RW_GW, RW_GW), 1) // HEAD
    same_head = rb == cb
    seg = same_head.astype(BF16)
    masks = (ti >= si, ti > si, (ti == si).astype(F32), same_head, lane_head)
    tri = (lax.broadcasted_iota(jnp.int32, (c, c), 0) >= lax.broadcasted_iota(jnp.int32, (c, c), 1)).astype(BF16)

    gr = range(RW_GROUPS)
    gsl = [pl.ds(g * RW_GW, RW_GW) for g in gr]

    def lerp(off, g):
        cols = pl.ds(off + g * RW_GW, RW_GW)
        cur = p_ref[:, cols]
        return cur + (shifted(cur, pp_ref.at[:, cols]) - cur) * mu_ref[:, cols]

    r = [lerp(0, g) for g in gr]
    k = [lerp(D, g) for g in gr]
    v = [lerp(2 * D, g) for g in gr]
    lw = [-DECAY_SCALE * jax.nn.sigmoid(w0_ref[:, gsl[g]] + _dot(dw, wl_ref[:, gsl[g]])) for g in gr]
    a = [jax.nn.sigmoid(a0_ref[:, gsl[g]] + _dot(da, al_ref[:, gsl[g]])) for g in gr]
    kap = [k[g] * kk_ref[:, gsl[g]] for g in gr]
    k = [k[g] * (1.0 + (a[g] - 1.0) * ka_ref[:, gsl[g]]) for g in gr]
    sums = _head_sums([kap[g] * kap[g] for g in gr] + [r[g] * k[g] * rk_ref[:, gsl[g]] for g in gr], seg)
    kap = [kap[g] / jnp.maximum(jnp.sqrt(sums[g]), 1e-12) for g in gr]
    bonus = sums[RW_GROUPS:]
    bet = [kap[g] * a[g] for g in gr]
    lw_s = [_split2(lw[g]) for g in gr]
    cum = [jnp.dot(tri, lw_s[g][0], preferred_element_type=F32) + jnp.dot(tri, lw_s[g][1], preferred_element_type=F32)
           for g in gr]

    y, h_new = [], []
    for lo in range(0, RW_GROUPS, RW_STAGE):
        sl = slice(lo, lo + RW_STAGE)
        ys, hs = _rwkv_groups(r[sl], lw[sl], cum[sl], k[sl], v[sl], kap[sl], bet[sl],
                              [h_scr[g] for g in range(lo, lo + RW_STAGE)], masks)
        y += ys
        h_new += hs
    for g in gr:
        h_scr[g] = h_new[g]

    ysum = _head_sums(y, seg)
    dlt = [y[g] - ysum[g] * (1.0 / HEAD) for g in gr]
    dsum = _head_sums([dlt[g] * dlt[g] for g in gr], seg)
    for g in gr:
        rstd = lax.rsqrt(dsum[g] * (1.0 / HEAD) + GN_EPS)
        yn = dlt[g] * rstd * gng_ref[:, gsl[g]] + gnb_ref[:, gsl[g]] + bonus[g] * v[g]
        o_ref[:, gsl[g]] = (yn * jax.nn.silu(lerp(3 * D, g))).astype(BF16)


def _rwkv_mix(p, p_lora, mu, mu_l, w0, a0, w_lora, a_lora, k_k, k_a, r_k, gn_g, gn_b):
    n_tok = p.shape[0]
    c = RW_CHUNK
    nc = SEQ // c
    ncol = p.shape[1]
    nl = p_lora.shape[1]
    row = lambda b, ci: (b * nc + ci, 0)
    prev = lambda b, ci: (jnp.maximum((b * nc + ci) * (c // 8) - 1, 0), 0)
    vec = pl.BlockSpec((1, D), lambda b, ci: (0, 0))
    lora = pl.BlockSpec((RW_LP // 2, D), lambda b, ci: (0, 0))
    return pl.pallas_call(
        _rwkv_mix_kernel,
        out_shape=jax.ShapeDtypeStruct((n_tok, D), BF16),
        grid=(n_tok // SEQ, nc),
        in_specs=[pl.BlockSpec((c, ncol), row),
                  pl.BlockSpec((8, ncol), prev),
                  pl.BlockSpec((c, nl), row),
                  pl.BlockSpec((8, nl), prev),
                  pl.BlockSpec((1, ncol), lambda b, ci: (0, 0)),
                  pl.BlockSpec((1, nl), lambda b, ci: (0, 0)),
                  vec, vec, lora, lora,
                  vec, vec, vec, vec, vec],
        out_specs=pl.BlockSpec((c, D), row),
        scratch_shapes=[pltpu.VMEM((RW_GROUPS, HEAD, RW_GW), F32)],
        compiler_params=_cparams("parallel", "arbitrary"),
        name="rwkv_mix",
    )(p, p, p_lora, p_lora, mu, mu_l, w0, a0, w_lora, a_lora, k_k, k_a, r_k, gn_g, gn_b)


def kernel(x, c, positions, norm_g, mod_w, mod_b, final_norm_g, sg_w_in, sg_w_out, sg_ln_g, sg_ln_b, sg_w_spatial,
           sg_b_spatial, swa_w_in, swa_w_out, swa_sinks, rwkv_w_in, rwkv_w_out, rwkv_mu, rwkv_w0, rwkv_w_lora,
           rwkv_a0, rwkv_a_lora, rwkv_k_k, rwkv_k_a, rwkv_r_k, rwkv_gn_g, rwkv_gn_b):
    batch, seq, d = x.shape
    assert (seq, d) == (SEQ, D) and norm_g.shape[0] == DEPTH
    n_tok = batch * seq
    x2 = x.reshape(n_tok, D)
    c8 = jnp.zeros((8, D), F32).at[:batch].set(c)
    mod = _modulation(c8, mod_w, mod_b)
    pos2 = positions.reshape(n_tok, 1)
    half = HEAD // 2
    inv_freq = ROPE_THETA ** (-jnp.arange(half, dtype=F32) / half)
    invf = jnp.tile(inv_freq, LANES // half).reshape(1, LANES)
    row = lambda a: a.reshape(1, -1)

    def act_rows(spans):
        rows = [jnp.concatenate([jnp.full((w,), val[t], F32) for w, *val in spans]) for t in range(3)]
        return jnp.zeros((8, rows[0].shape[0]), F32).at[:3].set(jnp.stack(rows))

    gelu_span = (GELU_A, GELU_B, 1.0)
    silu_span = (0.5, 0.0, 1.0)
    w_out_bf =(sg_w_out.astype(BF16), swa_w_out.astype(BF16), rwkv_w_out.astype(BF16))
    cos_t, sin_t = _rope_tables(pos2, invf)
    for i in range(DEPTH):
        kind, j = i % 3, i // 3
        mod3 = mod[i, :batch].reshape(batch, 1, 3 * D)
        g = row(norm_g[i])
        if kind == 0:
            act = act_rows([(2 * D, *gelu_span), (D, *silu_span)])
            p = _in_proj_act(x2, g, mod3, sg_w_in, j, 3 * D, act, False, BF16)
            mix = _sg_mix(p, row(sg_ln_g[j]), row(sg_ln_b[j]), sg_w_spatial[j],
                          jnp.broadcast_to(sg_b_spatial[j][:, :, None], (SG_GROUPS, SG_CHUNK, LANES)))
        elif kind == 1:
            act = act_rows([(D + 2 * SWA_KVW, 0.0, 0.0, 0.0), (D, *silu_span)])
            p = _in_proj_act(x2, g, mod3, swa_w_in, j, 2 * D + 2 * SWA_KVW, act, True, F32)
            mix = _swa_mix(p, cos_t, sin_t, swa_sinks[j])
        else:
            lp = RW_LP // 2 - RW_LORA
            w_in = rwkv_w_in[j]
            w_l = jnp.concatenate([jnp.pad(w_in[:, 4 * D:4 * D + RW_LORA], ((0, 0), (0, lp))),
                                   jnp.pad(w_in[:, 4 * D + RW_LORA:], ((0, 0), (0, lp)))], axis=1).astype(BF16)
            p, p_lora = _in_proj(x2, g, mod3, rwkv_w_in, j, 4 * D, w_l)
            mu = rwkv_mu[j]
            mu_l = jnp.concatenate([jnp.pad(mu[4 * D:4 * D + RW_LORA], (0, lp)), jnp.pad(mu[4 * D + RW_LORA:], (0, lp))])
            wl_pad = jnp.pad(rwkv_w_lora[j], ((0, lp), (0, 0))).astype(BF16)
            al_pad = jnp.pad(rwkv_a_lora[j], ((0, lp), (0, 0))).astype(BF16)
            mix = _rwkv_mix(p, p_lora, row(mu[:4 * D]), row(mu_l), row(rwkv_w0[j]),
                            row(rwkv_a0[j]), wl_pad, al_pad, row(rwkv_k_k[j]),
                            row(rwkv_k_a[j]), row(rwkv_r_k[j]), row(rwkv_gn_g[j]), row(rwkv_gn_b[j]))
        x2 = _out_proj(mix, w_out_bf[kind], j, x2, mod3, row(final_norm_g) if i == DEPTH - 1 else None)
    return x2.reshape(batch, seq, D)
```

```python
import functools
import math

import jax
import jax.numpy as jnp
from jax import lax
from jax.experimental import pallas as pl
from jax.experimental.pallas import tpu as pltpu

F32 = jnp.float32
BF16 = jnp.bfloat16

D = 2048
SEQ = 4096
DEPTH = 4
HEAD = 64
LANES = 128
SG_CHUNK = 128
SG_GROUPS = 16
SWA_HEADS = D // HEAD
SWA_KV = SWA_HEADS // 8
SWA_KVW = SWA_KV * HEAD
SWA_BLOCK = 128
ROPE_THETA = 10000.0
RW_HEADS = D // HEAD
RW_LORA = 96
RW_LP = 256
RW_CHUNK = 64
RW_HG = 4
RW_GW = RW_HG * HEAD
RW_GROUPS = D // RW_GW
RW_STAGE = 8
DECAY_SCALE = math.exp(-0.5)
GN_EPS = 64e-5
RMS_EPS = 1e-6
LN_EPS = 1e-5

VMEM_LIMIT = 48 * 1024 * 1024


def _cparams(*sem):
    return pltpu.CompilerParams(dimension_semantics=sem, vmem_limit_bytes=VMEM_LIMIT)


def _mod_kernel(c_ref, w_ref, b_ref, o_ref):
    cond = jax.nn.silu(c_ref[...]).astype(BF16)
    o_ref[0] = jnp.dot(cond, w_ref[0].astype(BF16), preferred_element_type=F32) + b_ref[0]


def _modulation(c8, mod_w, mod_b):
    tn = 1024
    n = mod_w.shape[2]
    return pl.pallas_call(
        _mod_kernel,
        out_shape=jax.ShapeDtypeStruct((DEPTH, 8, n), F32),
        grid=(DEPTH, n // tn),
        in_specs=[pl.BlockSpec((8, D), lambda l, j: (0, 0)),
                  pl.BlockSpec((1, D, tn), lambda l, j: (l, 0, j)),
                  pl.BlockSpec((1, 1, tn), lambda l, j: (l, 0, j))],
        out_specs=pl.BlockSpec((1, 8, tn), lambda l, j: (l, 0, j)),
        compiler_params=_cparams("parallel", "parallel"),
        name="modulation",
    )(c8, mod_w, mod_b.reshape(DEPTH, 1, n))


NORM_ROWS = 128


def _row_rsqrt(x_ref, rs_scr):
    def body(i, carry):
        rows = pl.ds(pl.multiple_of(i * NORM_ROWS, NORM_ROWS), NORM_ROWS)
        x = x_ref[rows, :]
        rs_scr[rows, :] = lax.rsqrt(jnp.mean(x * x, axis=-1, keepdims=True) + RMS_EPS)
        return carry
    lax.fori_loop(0, x_ref.shape[0] // NORM_ROWS, body, 0)


def _modulated(x_ref, rs_scr, g_ref, sc_ref, sh_ref):
    gain = g_ref[...] * (1.0 + sc_ref[0])
    return (x_ref[...] * rs_scr[...] * gain + sh_ref[0]).astype(BF16)


GELU_A = math.sqrt(2.0 / math.pi)
GELU_B = GELU_A * 0.044715
ACT_COLS = 256


def _in_proj_act_kernel(x_ref, g_ref, sc_ref, sh_ref, w_ref, act_ref, o_ref, rs_scr, *, select):
    @pl.when(pl.program_id(1) == 0)
    def _():
        _row_rsqrt(x_ref, rs_scr)

    h = _modulated(x_ref, rs_scr, g_ref, sc_ref, sh_ref)
    for n in range(o_ref.shape[1] // ACT_COLS):
        cols = pl.ds(n * ACT_COLS, ACT_COLS)
        p = jnp.dot(h, w_ref[:, cols], preferred_element_type=F32)
        gated = (0.5 * p) * (1.0 + jnp.tanh(p * (act_ref[0:1, cols] + act_ref[1:2, cols] * (p * p))))
        o_ref[:, cols] = (jnp.where(act_ref[2:3, cols] > 0.0, gated, p) if select else gated).astype(o_ref.dtype)


def _in_proj_lora_kernel(x_ref, g_ref, sc_ref, sh_ref, wt_ref, wl_ref, o_ref, ol_ref, rs_scr):
    @pl.when(pl.program_id(1) == 0)
    def _():
        _row_rsqrt(x_ref, rs_scr)
        ol_ref[...] = jnp.dot(_modulated(x_ref, rs_scr, g_ref, sc_ref, sh_ref), wl_ref[...],
                              preferred_element_type=F32)

    h = _modulated(x_ref, rs_scr, g_ref, sc_ref, sh_ref)
    o_ref[...] = _dot_nt(h, wt_ref[...])


def _in_proj_act(x2, g, mod3, w_bf, wl, ncols, act, select, out_dtype, *, tm=1024):
    n_tok = x2.shape[0]
    tn = 2048 if (out_dtype == BF16 and ncols % 2048 == 0) else (1024 if ncols % 1024 == 0 else 1536)
    assert ncols % tn == 0
    per_b = SEQ // tm
    return pl.pallas_call(
        functools.partial(_in_proj_act_kernel, select=select),
        out_shape=jax.ShapeDtypeStruct((n_tok, ncols), out_dtype),
        grid=(n_tok // tm, ncols // tn),
        in_specs=[pl.BlockSpec((tm, D), lambda i, j: (i, 0)),
                  pl.BlockSpec((1, D), lambda i, j: (0, 0)),
                  pl.BlockSpec((1, 1, D), lambda i, j: (i // per_b, 0, 1)),
                  pl.BlockSpec((1, 1, D), lambda i, j: (i // per_b, 0, 0)),
                  pl.BlockSpec((None, D, tn), lambda i, j: (wl, 0, j)),
                  pl.BlockSpec((8, tn), lambda i, j: (0, j))],
        out_specs=pl.BlockSpec((tm, tn), lambda i, j: (i, j)),
        scratch_shapes=[pltpu.VMEM((tm, 1), F32)],
        compiler_params=_cparams("parallel", "arbitrary"),
        name="in_proj_act",
    )(x2, g, mod3, mod3, w_bf, act)


def _in_proj(x2, g, mod3, w_bf, wl, ncols, w_lora_bf, *, tm=1024):
    n_tok = x2.shape[0]
    tn = 1024
    assert ncols % tn == 0
    per_b = SEQ // tm
    x_spec = pl.BlockSpec((tm, D), lambda i, j: (i, 0))
    g_spec = pl.BlockSpec((1, D), lambda i, j: (0, 0))
    sh_spec = pl.BlockSpec((1, 1, D), lambda i, j: (i // per_b, 0, 0))
    sc_spec = pl.BlockSpec((1, 1, D), lambda i, j: (i // per_b, 0, 1))
    w_spec = pl.BlockSpec((None, tn, D), lambda i, j: (wl, j, 0))
    o_spec = pl.BlockSpec((tm, tn), lambda i, j: (i, j))
    grid = (n_tok // tm, ncols // tn)
    scratch = [pltpu.VMEM((tm, 1), F32)]
    nl = w_lora_bf.shape[1]
    return pl.pallas_call(
        _in_proj_lora_kernel,
        out_shape=(jax.ShapeDtypeStruct((n_tok, ncols), F32), jax.ShapeDtypeStruct((n_tok, nl), F32)),
        grid=grid,
        in_specs=[x_spec, g_spec, sc_spec, sh_spec, w_spec, pl.BlockSpec((D, nl), lambda i, j: (0, 0))],
        out_specs=(o_spec, pl.BlockSpec((tm, nl), lambda i, j: (i, 0))),
        scratch_shapes=scratch,
        compiler_params=_cparams("parallel", "arbitrary"),
        name="in_proj_lora",
    )(x2, g, mod3, mod3, w_bf, w_lora_bf)


def _out_proj_kernel(m_ref, w_ref, x_ref, gate_ref, o_ref):
    y = jnp.dot(m_ref[...], w_ref[...], preferred_element_type=F32)
    o_ref[...] = x_ref[...] + gate_ref[0] * y


def _out_proj_final_kernel(m_ref, w_ref, x_ref, gate_ref, fg_ref, o_ref):
    y = jnp.dot(m_ref[...], w_ref[...], preferred_element_type=F32)
    xn = x_ref[...] + gate_ref[0] * y
    o_ref[...] = xn * lax.rsqrt(jnp.mean(xn * xn, axis=-1, keepdims=True) + RMS_EPS) * fg_ref[...]


def _out_proj(mix, w_bf, wl, x2, mod3, final_g=None, *, tm=512):
    n_tok = x2.shape[0]
    per_b = SEQ // tm
    in_specs = [pl.BlockSpec((tm, D), lambda i: (i, 0)),
                pl.BlockSpec((None, D, D), lambda i: (wl, 0, 0)),
                pl.BlockSpec((tm, D), lambda i: (i, 0)),
                pl.BlockSpec((1, 1, D), lambda i: (i // per_b, 0, 2))]
    args = [mix, w_bf, x2, mod3]
    kern = _out_proj_kernel
    if final_g is not None:
        in_specs.append(pl.BlockSpec((1, D), lambda i: (0, 0)))
        args.append(final_g)
        kern = _out_proj_final_kernel
    return pl.pallas_call(
        kern,
        out_shape=jax.ShapeDtypeStruct((n_tok, D), F32),
        grid=(n_tok // tm,),
        in_specs=in_specs,
        out_specs=pl.BlockSpec((tm, D), lambda i: (i, 0)),
        compiler_params=_cparams("parallel"),
        name="out_proj_final" if final_g is not None else "out_proj",
    )(*args)


def _sg_mix_kernel(u_ref, v_ref, z_ref, lng_ref, lnb_ref, ws_ref, bs_ref, o_ref, *, tm):
    ti = lax.broadcasted_iota(jnp.int32, (SG_CHUNK, SG_CHUNK), 0)
    si = lax.broadcasted_iota(jnp.int32, (SG_CHUNK, SG_CHUNK), 1)
    causal = ti >= si
    ws = [jnp.where(causal, ws_ref[g], 0.0).astype(BF16) for g in range(SG_GROUPS)]
    for c in range(tm // SG_CHUNK):
        rows = pl.ds(c * SG_CHUNK, SG_CHUNK)
        v = v_ref[rows, :].astype(F32)
        mean = jnp.mean(v, axis=-1, keepdims=True)
        d = v - mean
        var = jnp.mean(d * d, axis=-1, keepdims=True)
        vn = (d * lax.rsqrt(var + LN_EPS) * lng_ref[...] + lnb_ref[...]).astype(BF16)
        f = [jnp.dot(ws[g], vn[:, g * LANES:(g + 1) * LANES], preferred_element_type=F32) for g in range(SG_GROUPS)]
        for g in range(SG_GROUPS):
            cols = pl.ds(g * LANES, LANES)
            uz = u_ref[rows, cols].astype(F32) * z_ref[rows, cols].astype(F32)
            o_ref[rows, cols] = (uz * (f[g] + bs_ref[g])).astype(BF16)


def _sg_mix(p, ln_g, ln_b, w_s, b_full, *, tm=512):
    n_tok = p.shape[0]
    return pl.pallas_call(
        functools.partial(_sg_mix_kernel, tm=tm),
        out_shape=jax.ShapeDtypeStruct((n_tok, D), BF16),
        grid=(n_tok // tm,),
        in_specs=[pl.BlockSpec((tm, D), lambda i: (i, 0)),
                  pl.BlockSpec((tm, D), lambda i: (i, 1)),
                  pl.BlockSpec((tm, D), lambda i: (i, 2)),
                  pl.BlockSpec((1, D), lambda i: (0, 0)),
                  pl.BlockSpec((1, D), lambda i: (0, 0)),
                  pl.BlockSpec((SG_GROUPS, SG_CHUNK, SG_CHUNK), lambda i: (0, 0, 0)),
                  pl.BlockSpec((SG_GROUPS, SG_CHUNK, LANES), lambda i: (0, 0, 0))],
        out_specs=pl.BlockSpec((tm, D), lambda i: (i, 0)),
        compiler_params=_cparams("parallel"),
        name="sg_mix",
    )(p, p, p, ln_g, ln_b, w_s, b_full)


def _first_half(shape):
    return (lax.broadcasted_iota(jnp.int32, shape, 1) % HEAD) < (HEAD // 2)


def _rope_table_kernel(pos_ref, invf_ref, cos_ref, sin_ref):
    ang = pos_ref[...].astype(F32) * invf_ref[...]
    cos_ref[...] = jnp.cos(ang)
    sin_ref[...] = jnp.where(_first_half(ang.shape), -jnp.sin(ang), jnp.sin(ang))


def _rope_tables(pos2, invf, *, tm=1024):
    n_tok = pos2.shape[0]
    spec = pl.BlockSpec((tm, LANES), lambda i: (i, 0))
    return pl.pallas_call(
        _rope_table_kernel,
        out_shape=(jax.ShapeDtypeStruct((n_tok, LANES), F32),) * 2,
        grid=(n_tok // tm,),
        in_specs=[pl.BlockSpec((tm, 1), lambda i: (i, 0)), pl.BlockSpec((1, LANES), lambda i: (0, 0))],
        out_specs=(spec, spec),
        compiler_params=_cparams("parallel"),
        name="rope_tables",
    )(pos2, invf)


def _rope(x, cos, sin_signed, first):
    partner = jnp.where(first, pltpu.roll(x, LANES - HEAD // 2, 1), pltpu.roll(x, HEAD // 2, 1))
    return x * cos + partner * sin_signed


def _swa_mix_kernel(sink_ref, p_ref, kvp_ref, cos_ref, sin_ref, cosp_ref, sinp_ref, o_ref):
    i = pl.program_id(0)
    first_key = jnp.where((i % (SEQ // SWA_BLOCK)) != 0, 0, SWA_BLOCK)
    cos, sin_s, cosp, sinp_s = cos_ref[...], sin_ref[...], cosp_ref[...], sinp_ref[...]
    first = _first_half(cos.shape)
    lane = lax.broadcasted_iota(jnp.int32, (2 * SWA_BLOCK, LANES), 1)
    low = lane < HEAD

    qi = lax.broadcasted_iota(jnp.int32, (SWA_BLOCK, 2 * SWA_BLOCK), 0)
    kj = lax.broadcasted_iota(jnp.int32, (SWA_BLOCK, 2 * SWA_BLOCK), 1)
    rel = qi + SWA_BLOCK - kj
    mask = (rel >= 0) & (rel < SWA_BLOCK) & (kj >= first_key)

    k_lo, k_hi, v_lo, v_hi = [], [], [], []
    for kp in range(SWA_KV // 2):
        ksl = pl.ds(kp * LANES, LANES)
        vsl = pl.ds(SWA_KVW + kp * LANES, LANES)
        k2 = jnp.concatenate([_rope(kvp_ref[:, ksl], cosp, sinp_s, first),
                              _rope(p_ref[:, pl.ds(D + kp * LANES, LANES)], cos, sin_s, first)], axis=0)
        v2 = jnp.concatenate([kvp_ref[:, vsl], p_ref[:, pl.ds(D + SWA_KVW + kp * LANES, LANES)]], axis=0)
        k2r = pltpu.roll(k2, HEAD, 1)
        v2r = pltpu.roll(v2, HEAD, 1)
        for sub in range(2):
            src_k, src_kr = (k2, k2r) if sub == 0 else (k2r, k2)
            src_v, src_vr = (v2, v2r) if sub == 0 else (v2r, v2)
            k_lo.append(jnp.where(low, src_k, 0.0).astype(BF16))
            k_hi.append(jnp.where(low, 0.0, src_kr).astype(BF16))
            v_lo.append(jnp.where(low, src_v, 0.0).astype(BF16))
            v_hi.append(jnp.where(low, 0.0, src_vr).astype(BF16))

    npair = SWA_HEADS // 2
    cols = [pl.ds(pr * LANES, LANES) for pr in range(npair)]
    q2 = [(_rope(p_ref[:, cols[pr]], cos, sin_s, first) * (HEAD ** -0.5)).astype(BF16) for pr in range(npair)]
    s = [lax.dot_general(q2[h // 2], (k_hi if h % 2 else k_lo)[h // 8], (((1,), (1,)), ((), ())),
                         preferred_element_type=F32) for h in range(SWA_HEADS)]
    hs = range(SWA_HEADS)
    s = [jnp.where(mask, s[h], -jnp.inf) for h in hs]
    m = [jnp.maximum(jnp.max(s[h], axis=-1, keepdims=True), sink_ref[h]) for h in hs]
    e = [jnp.exp(s[h] - m[h]) for h in hs]
    denom = [jnp.sum(e[h], axis=-1, keepdims=True) + jnp.exp(sink_ref[h] - m[h]) for h in hs]
    prob = [(e[h] * (1.0 / denom[h])).astype(BF16) for h in hs]
    o = [jnp.dot(prob[h], (v_hi if h % 2 else v_lo)[h // 8], preferred_element_type=F32) for h in range(SWA_HEADS)]
    zoff = D + 2 * SWA_KVW
    for pr in range(npair):
        gate = p_ref[:, pl.ds(zoff + pr * LANES, LANES)]
        o_ref[:, cols[pr]] = ((o[2 * pr] + o[2 * pr + 1]) * gate).astype(BF16)


def _swa_mix(p, cos_t, sin_t, sinks):
    n_tok, ncols = p.shape
    tb = SWA_BLOCK
    kvb = D // (2 * SWA_KVW)
    cur = pl.BlockSpec((tb, LANES), lambda i: (i, 0))
    prev = pl.BlockSpec((tb, LANES), lambda i: (jnp.maximum(i - 1, 0), 0))
    return pl.pallas_call(
        _swa_mix_kernel,
        out_shape=jax.ShapeDtypeStruct((n_tok, D), BF16),
        grid=(n_tok // tb,),
        in_specs=[pl.BlockSpec(memory_space=pltpu.SMEM),
                  pl.BlockSpec((tb, ncols), lambda i: (i, 0)),
                  pl.BlockSpec((tb, 2 * SWA_KVW), lambda i: (jnp.maximum(i - 1, 0), kvb)),
                  cur, cur, prev, prev],
        out_specs=pl.BlockSpec((tb, D), lambda i: (i, 0)),
        compiler_params=_cparams("parallel"),
        name="swa_mix",
    )(sinks, p, p, cos_t, sin_t, cos_t, sin_t)


def _split2(x):
    hi = x.astype(BF16)
    return hi, (x - hi.astype(F32)).astype(BF16)


def _head_sums(xs, seg):
    parts = []
    for x in xs:
        parts.extend(_split2(x))
    out = jnp.dot(jnp.concatenate(parts, axis=0), seg, preferred_element_type=F32)
    n = xs[0].shape[0]
    return [out[2 * i * n:(2 * i + 1) * n] + out[(2 * i + 1) * n:(2 * i + 2) * n] for i in range(len(xs))]


def _bd(x, same_head):
    t = jnp.concatenate([x] * RW_HG, axis=0)
    return jnp.where(same_head, t, 0.0).astype(BF16)


def _diag_blocks(o, lane_head):
    acc = o[0:HEAD]
    for h in range(1, RW_HG):
        acc = jnp.where(lane_head == h, o[h * HEAD:(h + 1) * HEAD], acc)
    return acc


def _dot(a, b):
    return jnp.dot(a.astype(BF16), b.astype(BF16), preferred_element_type=F32)


def _dot_nt(a, b):
    return lax.dot_general(a, b, (((1,), (1,)), ((), ())), preferred_element_type=F32)


def _dot_tn(a, b):
    return lax.dot_general(a, b, (((0,), (0,)), ((), ())), preferred_element_type=F32)


def _rwkv_groups(r, lw, cum, k, v, kap, bet, h0, masks):
    incl, strict, eye, same_head, lane_head = masks
    c = RW_CHUNK
    gr = range(len(r))
    bd = lambda t: _bd(t, same_head)
    cref = [cum[g][c // 2 - 1:c // 2] for g in gr]
    cend = [cum[g][c - 1:c] for g in gr]
    r_true = [r[g] * jnp.exp(cum[g]) for g in gr]
    kap_true = [kap[g] * jnp.exp(cum[g] - lw[g]) for g in gr]
    r_s = [r[g] * jnp.exp(cum[g] - cref[g]) for g in gr]
    kap_s = [kap[g] * jnp.exp(cum[g] - lw[g] - cref[g]) for g in gr]
    ginv = [jnp.exp(cref[g] - cum[g]) for g in gr]
    eend = [jnp.exp(cend[g] - cum[g]) for g in gr]

    s = [_dot_nt(jnp.concatenate([r_s[g], kap_s[g]], axis=0).astype(BF16),
                 jnp.concatenate([bd(k[g] * ginv[g]), bd(bet[g] * ginv[g])], axis=0)) for g in gr]
    a_rk = [jnp.where(incl, s[g][:c, :RW_GW], 0.0) for g in gr]
    a_rb = [jnp.where(incl, s[g][:c, RW_GW:], 0.0) for g in gr]
    a_kk = [jnp.where(strict, s[g][c:, :RW_GW], 0.0) for g in gr]
    a_kb = [jnp.where(strict, s[g][c:, RW_GW:], 0.0) for g in gr]

    x = [eye - a_kb[g] for g in gr]
    p = [_dot(a_kb[g], bd(a_kb[g])) for g in gr]
    akv = [_dot(a_kk[g], bd(v[g])) for g in gr]
    for _ in range(4):
        xp = [_dot(jnp.concatenate([x[g], p[g]], axis=0), bd(p[g])) for g in gr]
        x = [x[g] + xp[g][:c] for g in gr]
        p = [xp[g][c:] for g in gr]
    t_inv = [x[g] + _dot(x[g], bd(p[g])) for g in gr]

    wu = [-_dot(t_inv[g], jnp.concatenate([bd(kap_true[g]), bd(akv[g])], axis=1)) for g in gr]

    hw = [_dot(jnp.concatenate([wu[g][:, :RW_GW], r_true[g], eye * jnp.exp(cend[g])], axis=0), bd(h0[g]))
          for g in gr]
    u = [hw[g][:c] + wu[g][:, RW_GW:] for g in gr]
    y = [hw[g][c:2 * c] + _dot(jnp.concatenate([a_rk[g], a_rb[g]], axis=1),
                               jnp.concatenate([bd(v[g]), bd(u[g])], axis=0)) for g in gr]
    o2 = [_dot_tn(jnp.concatenate([k[g] * eend[g], bet[g] * eend[g]], axis=0).astype(BF16),
                  jnp.concatenate([v[g], u[g]], axis=0).astype(BF16)) for g in gr]
    h_new = [hw[g][2 * c:] + _diag_blocks(o2[g], lane_head) for g in gr]
    return y, h_new


def _rwkv_mix_kernel(p_ref, pp_ref, pl_ref, plp_ref, mu_ref, mul_ref, w0_ref, a0_ref, wl_ref, al_ref,
                     kk_ref, ka_ref, rk_ref, gng_ref, gnb_ref, o_ref, h_scr):
    ci = pl.program_id(1)
    c = RW_CHUNK

    @pl.when(ci == 0)
    def _():
        h_scr[...] = jnp.zeros_like(h_scr)

    not_first = (ci != 0).astype(F32)
    row = lax.broadcasted_iota(jnp.int32, (c, 1), 0)

    def shifted(cur, prev_ref):
        prev_row = prev_ref[7:8, :] * not_first
        rolled = pltpu.roll(cur, 1, 0)
        return jnp.where(row == 0, prev_row, rolled)

    lcur = pl_ref[...]
    lcur = lcur + (shifted(lcur, plp_ref) - lcur) * mul_ref[...]
    dw = jnp.tanh(lcur[:, :RW_LP // 2])
    da = lcur[:, RW_LP // 2:]

    ti = lax.broadcasted_iota(jnp.int32, (c, RW_GW), 0)
    li = lax.broadcasted_iota(jnp.int32, (c, RW_GW), 1)
    si = li % HEAD
    lane_head = li // HEAD
    rb = lax.broadcasted_iota(jnp.int32, (RW_GW, RW_GW), 0) // HEAD
    cb = lax.broadcasted_iota(jnp.int32, (RW_GW, RW_GW), 1) // HEAD
    same_head = rb == cb
    seg = same_head.astype(BF16)
    masks = (ti >= si, ti > si, (ti == si).astype(F32), same_head, lane_head)
    tri = (lax.broadcasted_iota(jnp.int32, (c, c), 0) >= lax.broadcasted_iota(jnp.int32, (c, c), 1)).astype(BF16)

    gr = range(RW_GROUPS)
    gsl = [pl.ds(g * RW_GW, RW_GW) for g in gr]

    def lerp(off, g):
        cols = pl.ds(off + g * RW_GW, RW_GW)
        cur = p_ref[:, cols]
        return cur + (shifted(cur, pp_ref.at[:, cols]) - cur) * mu_ref[:, cols]

    r = [lerp(0, g) for g in gr]
    k = [lerp(D, g) for g in gr]
    v = [lerp(2 * D, g) for g in gr]
    lw = [-DECAY_SCALE * jax.nn.sigmoid(w0_ref[:, gsl[g]] + _dot(dw, wl_ref[:, gsl[g]])) for g in gr]
    a = [jax.nn.sigmoid(a0_ref[:, gsl[g]] + _dot(da, al_ref[:, gsl[g]])) for g in gr]
    kap = [k[g] * kk_ref[:, gsl[g]] for g in gr]
    k = [k[g] * (1.0 + (a[g] - 1.0) * ka_ref[:, gsl[g]]) for g in gr]
    sums = _head_sums([kap[g] * kap[g] for g in gr] + [r[g] * k[g] * rk_ref[:, gsl[g]] for g in gr], seg)
    kap = [kap[g] / jnp.maximum(jnp.sqrt(sums[g]), 1e-12) for g in gr]
    bonus = sums[RW_GROUPS:]
    bet = [kap[g] * a[g] for g in gr]
    lw_s = [_split2(lw[g]) for g in gr]
    cum = [jnp.dot(tri, lw_s[g][0], preferred_element_type=F32) + jnp.dot(tri, lw_s[g][1], preferred_element_type=F32)
           for g in gr]

    y, h_new = [], []
    for lo in range(0, RW_GROUPS, RW_STAGE):
        sl = slice(lo, lo + RW_STAGE)
        ys, hs = _rwkv_groups(r[sl], lw[sl], cum[sl], k[sl], v[sl], kap[sl], bet[sl],
                              [h_scr[g] for g in range(lo, lo + RW_STAGE)], masks)
        y += ys
        h_new += hs
    for g in gr:
        h_scr[g] = h_new[g]

    ysum = _head_sums(y, seg)
    dlt = [y[g] - ysum[g] * (1.0 / HEAD) for g in gr]
    dsum = _head_sums([dlt[g] * dlt[g] for g in gr], seg)
    for g in gr:
        rstd = lax.rsqrt(dsum[g] * (1.0 / HEAD) + GN_EPS)
        yn = dlt[g] * rstd * gng_ref[:, gsl[g]] + gnb_ref[:, gsl[g]] + bonus[g] * v[g]
        o_ref[:, gsl[g]] = (yn * jax.nn.silu(lerp(3 * D, g))).astype(BF16)


def _rwkv_mix(p, p_lora, mu, mu_l, w0, a0, w_lora, a_lora, k_k, k_a, r_k, gn_g, gn_b):
    n_tok = p.shape[0]
    c = RW_CHUNK
    nc = SEQ // c
    ncol = p.shape[1]
    nl = p_lora.shape[1]
    row = lambda b, ci: (b * nc + ci, 0)
    prev = lambda b, ci: (jnp.maximum((b * nc + ci) * (c // 8) - 1, 0), 0)
    vec = pl.BlockSpec((1, D), lambda b, ci: (0, 0))
    lora = pl.BlockSpec((RW_LP // 2, D), lambda b, ci: (0, 0))
    return pl.pallas_call(
        _rwkv_mix_kernel,
        out_shape=jax.ShapeDtypeStruct((n_tok, D), BF16),
        grid=(n_tok // SEQ, nc),
        in_specs=[pl.BlockSpec((c, ncol), row),
                  pl.BlockSpec((8, ncol), prev),
                  pl.BlockSpec((c, nl), row),
                  pl.BlockSpec((8, nl), prev),
                  pl.BlockSpec((1, ncol), lambda b, ci: (0, 0)),
                  pl.BlockSpec((1, nl), lambda b, ci: (0, 0)),
                  vec, vec, lora, lora,
                  vec, vec, vec, vec, vec],
        out_specs=pl.BlockSpec((c, D), row),
        scratch_shapes=[pltpu.VMEM((RW_GROUPS, HEAD, RW_GW), F32)],
        compiler_params=_cparams("parallel", "arbitrary"),
        name="rwkv_mix",
    )(p, p, p_lora, p_lora, mu, mu_l, w0, a0, w_lora, a_lora, k_k, k_a, r_k, gn_g, gn_b)


def kernel(x, c, positions, norm_g, mod_w, mod_b, final_norm_g, sg_w_in, sg_w_out, sg_ln_g, sg_ln_b, sg_w_spatial,
           sg_b_spatial, swa_w_in, swa_w_out, swa_sinks, rwkv_w_in, rwkv_w_out, rwkv_mu, rwkv_w0, rwkv_w_lora,
           rwkv_a0, rwkv_a_lora, rwkv_k_k, rwkv_k_a, rwkv_r_k, rwkv_gn_g, rwkv_gn_b):
    batch, seq, d = x.shape
    assert (seq, d) == (SEQ, D) and norm_g.shape[0] == DEPTH
    n_tok = batch * seq
    x2 = x.reshape(n_tok, D)
    c8 = jnp.zeros((8, D), F32).at[:batch].set(c)
    mod = _modulation(c8, mod_w, mod_b)
    pos2 = positions.reshape(n_tok, 1)
    half = HEAD // 2
    inv_freq = ROPE_THETA ** (-jnp.arange(half, dtype=F32) / half)
    invf = jnp.tile(inv_freq, LANES // half).reshape(1, LANES)
    row = lambda a: a.reshape(1, -1)

    def act_rows(spans):
        rows = [jnp.concatenate([jnp.full((w,), val[t], F32) for w, *val in spans]) for t in range(3)]
        return jnp.zeros((8, rows[0].shape[0]), F32).at[:3].set(jnp.stack(rows))

    gelu_span = (GELU_A, GELU_B, 1.0)
    silu_span = (0.5, 0.0, 1.0)
    sg_in_bf, swa_in_bf = sg_w_in.astype(BF16), swa_w_in.astype(BF16)
    rwkv_in_t = jnp.swapaxes(rwkv_w_in, 1, 2).astype(BF16)
    w_out_bf =(sg_w_out.astype(BF16), swa_w_out.astype(BF16), rwkv_w_out.astype(BF16))
    cos_t, sin_t = _rope_tables(pos2, invf)
    for i in range(DEPTH):
        kind, j = i % 3, i // 3
        mod3 = mod[i, :batch].reshape(batch, 1, 3 * D)
        g = row(norm_g[i])
        if kind == 0:
            act = act_rows([(2 * D, *gelu_span), (D, *silu_span)])
            p = _in_proj_act(x2, g, mod3, sg_in_bf, j, 3 * D, act, False, BF16)
            mix = _sg_mix(p, row(sg_ln_g[j]), row(sg_ln_b[j]), sg_w_spatial[j],
                          jnp.broadcast_to(sg_b_spatial[j][:, :, None], (SG_GROUPS, SG_CHUNK, LANES)))
        elif kind == 1:
            act = act_rows([(D + 2 * SWA_KVW, 0.0, 0.0, 0.0), (D, *silu_span)])
            p = _in_proj_act(x2, g, mod3, swa_in_bf, j, 2 * D + 2 * SWA_KVW, act, True, F32)
            mix = _swa_mix(p, cos_t, sin_t, swa_sinks[j])
        else:
            lp = RW_LP // 2 - RW_LORA
            w_lt = rwkv_in_t[j, 4 * D:].T
            w_l = jnp.concatenate([jnp.pad(w_lt[:, :RW_LORA], ((0, 0), (0, lp))),
                                   jnp.pad(w_lt[:, RW_LORA:], ((0, 0), (0, lp)))], axis=1)
            p, p_lora = _in_proj(x2, g, mod3, rwkv_in_t, j, 4 * D, w_l)
            mu = rwkv_mu[j]
            mu_l = jnp.concatenate([jnp.pad(mu[4 * D:4 * D + RW_LORA], (0, lp)), jnp.pad(mu[4 * D + RW_LORA:], (0, lp))])
            wl_pad = jnp.pad(rwkv_w_lora[j], ((0, lp), (0, 0))).astype(BF16)
            al_pad = jnp.pad(rwkv_a_lora[j], ((0, lp), (0, 0))).astype(BF16)
            mix = _rwkv_mix(p, p_lora, row(mu[:4 * D]), row(mu_l), row(rwkv_w0[j]),
                            row(rwkv_a0[j]), wl_pad, al_pad, row(rwkv_k_k[j]),
                            row(rwkv_k_a[j]), row(rwkv_r_k[j]), row(rwkv_gn_g[j]), row(rwkv_gn_b[j]))
        x2 = _out_proj(mix, w_out_bf[kind], j, x2, mod3, row(final_norm_g) if i == DEPTH - 1 else None)
    return x2.reshape(batch, seq, D)
```

```python
import functools
import math

import jax
import jax.numpy as jnp
from jax import lax
from jax.experimental import pallas as pl
from jax.experimental.pallas import tpu as pltpu

F32 = jnp.float32
BF16 = jnp.bfloat16

D = 2048
SEQ = 4096
DEPTH = 4
HEAD = 64
LANES = 128
SG_CHUNK = 128
SG_GROUPS = 16
SWA_HEADS = D // HEAD
SWA_KV = SWA_HEADS // 8
SWA_KVW = SWA_KV * HEAD
SWA_BLOCK = 128
ROPE_THETA = 10000.0
RW_HEADS = D // HEAD
RW_LORA = 96
RW_LP = 256
RW_CHUNK = 64
RW_HG = 4
RW_GW = RW_HG * HEAD
RW_GROUPS = D // RW_GW
RW_STAGE = 8
DECAY_SCALE = math.exp(-0.5)
GN_EPS = 64e-5
RMS_EPS = 1e-6
LN_EPS = 1e-5

VMEM_LIMIT = 48 * 1024 * 1024


def _cparams(*sem):
    return pltpu.CompilerParams(dimension_semantics=sem, vmem_limit_bytes=VMEM_LIMIT)


def _mod_kernel(c_ref, w_ref, b_ref, o_ref):
    cond = jax.nn.silu(c_ref[...]).astype(BF16)
    o_ref[0] = jnp.dot(cond, w_ref[0].astype(BF16), preferred_element_type=F32) + b_ref[0]


def _modulation(c8, mod_w, mod_b):
    tn = 1024
    n = mod_w.shape[2]
    return pl.pallas_call(
        _mod_kernel,
        out_shape=jax.ShapeDtypeStruct((DEPTH, 8, n), F32),
        grid=(DEPTH, n // tn),
        in_specs=[pl.BlockSpec((8, D), lambda l, j: (0, 0)),
                  pl.BlockSpec((1, D, tn), lambda l, j: (l, 0, j)),
                  pl.BlockSpec((1, 1, tn), lambda l, j: (l, 0, j))],
        out_specs=pl.BlockSpec((1, 8, tn), lambda l, j: (l, 0, j)),
        compiler_params=_cparams("parallel", "parallel"),
        name="modulation",
    )(c8, mod_w, mod_b.reshape(DEPTH, 1, n))


NORM_ROWS = 128


def _row_rsqrt(x_ref, rs_scr):
    def body(i, carry):
        rows = pl.ds(pl.multiple_of(i * NORM_ROWS, NORM_ROWS), NORM_ROWS)
        x = x_ref[rows, :]
        rs_scr[rows, :] = lax.rsqrt(jnp.mean(x * x, axis=-1, keepdims=True) + RMS_EPS)
        return carry
    lax.fori_loop(0, x_ref.shape[0] // NORM_ROWS, body, 0)


def _modulated(x_ref, rs_scr, g_ref, sc_ref, sh_ref):
    gain = g_ref[...] * (1.0 + sc_ref[0])
    return (x_ref[...] * rs_scr[...] * gain + sh_ref[0]).astype(BF16)


GELU_A = math.sqrt(2.0 / math.pi)
GELU_B = GELU_A * 0.044715
ACT_COLS = 256


def _in_proj_act_kernel(x_ref, g_ref, sc_ref, sh_ref, w_ref, act_ref, o_ref, rs_scr, *, select):
    @pl.when(pl.program_id(1) == 0)
    def _():
        _row_rsqrt(x_ref, rs_scr)

    h = _modulated(x_ref, rs_scr, g_ref, sc_ref, sh_ref)
    for n in range(o_ref.shape[1] // ACT_COLS):
        cols = pl.ds(n * ACT_COLS, ACT_COLS)
        p = jnp.dot(h, w_ref[:, cols], preferred_element_type=F32)
        gated = (0.5 * p) * (1.0 + jnp.tanh(p * (act_ref[0:1, cols] + act_ref[1:2, cols] * (p * p))))
        o_ref[:, cols] = (jnp.where(act_ref[2:3, cols] > 0.0, gated, p) if select else gated).astype(o_ref.dtype)


def _in_proj_lora_kernel(x_ref, g_ref, sc_ref, sh_ref, w_ref, wl_ref, o_ref, ol_ref, rs_scr):
    @pl.when(pl.program_id(1) == 0)
    def _():
        _row_rsqrt(x_ref, rs_scr)
        ol_ref[...] = jnp.dot(_modulated(x_ref, rs_scr, g_ref, sc_ref, sh_ref), wl_ref[...],
                              preferred_element_type=F32)

    h = _modulated(x_ref, rs_scr, g_ref, sc_ref, sh_ref)
    o_ref[...] = jnp.dot(h, w_ref[...], preferred_element_type=F32)


def _in_proj_act(x2, g, mod3, w_bf, wl, ncols, act, select, out_dtype, *, tm=1024):
    n_tok = x2.shape[0]
    tn = 2048 if (out_dtype == BF16 and ncols % 2048 == 0) else (1024 if ncols % 1024 == 0 else 1536)
    assert ncols % tn == 0
    per_b = SEQ // tm
    return pl.pallas_call(
        functools.partial(_in_proj_act_kernel, select=select),
        out_shape=jax.ShapeDtypeStruct((n_tok, ncols), out_dtype),
        grid=(n_tok // tm, ncols // tn),
        in_specs=[pl.BlockSpec((tm, D), lambda i, j: (i, 0)),
                  pl.BlockSpec((1, D), lambda i, j: (0, 0)),
                  pl.BlockSpec((1, 1, D), lambda i, j: (i // per_b, 0, 1)),
                  pl.BlockSpec((1, 1, D), lambda i, j: (i // per_b, 0, 0)),
                  pl.BlockSpec((None, D, tn), lambda i, j: (wl, 0, j)),
                  pl.BlockSpec((8, tn), lambda i, j: (0, j))],
        out_specs=pl.BlockSpec((tm, tn), lambda i, j: (i, j)),
        scratch_shapes=[pltpu.VMEM((tm, 1), F32)],
        compiler_params=_cparams("parallel", "arbitrary"),
        name="in_proj_act",
    )(x2, g, mod3, mod3, w_bf, act)


def _in_proj(x2, g, mod3, w_bf, wl, ncols, w_lora_bf, *, tm=1024):
    n_tok = x2.shape[0]
    tn = 1024
    assert ncols % tn == 0
    per_b = SEQ // tm
    x_spec = pl.BlockSpec((tm, D), lambda i, j: (i, 0))
    g_spec = pl.BlockSpec((1, D), lambda i, j: (0, 0))
    sh_spec = pl.BlockSpec((1, 1, D), lambda i, j: (i // per_b, 0, 0))
    sc_spec = pl.BlockSpec((1, 1, D), lambda i, j: (i // per_b, 0, 1))
    w_spec = pl.BlockSpec((None, D, tn), lambda i, j: (wl, 0, j))
    o_spec = pl.BlockSpec((tm, tn), lambda i, j: (i, j))
    grid = (n_tok // tm, ncols // tn)
    scratch = [pltpu.VMEM((tm, 1), F32)]
    nl = w_lora_bf.shape[1]
    return pl.pallas_call(
        _in_proj_lora_kernel,
        out_shape=(jax.ShapeDtypeStruct((n_tok, ncols), F32), jax.ShapeDtypeStruct((n_tok, nl), F32)),
        grid=grid,
        in_specs=[x_spec, g_spec, sc_spec, sh_spec, w_spec, pl.BlockSpec((D, nl), lambda i, j: (0, 0))],
        out_specs=(o_spec, pl.BlockSpec((tm, nl), lambda i, j: (i, 0))),
        scratch_shapes=scratch,
        compiler_params=_cparams("parallel", "arbitrary"),
        name="in_proj_lora",
    )(x2, g, mod3, mod3, w_bf, w_lora_bf)


def _out_proj_kernel(m_ref, w_ref, x_ref, gate_ref, o_ref):
    y = jnp.dot(m_ref[...], w_ref[...], preferred_element_type=F32)
    o_ref[...] = x_ref[...] + gate_ref[0] * y


def _out_proj_final_kernel(m_ref, w_ref, x_ref, gate_ref, fg_ref, o_ref):
    y = jnp.dot(m_ref[...], w_ref[...], preferred_element_type=F32)
    xn = x_ref[...] + gate_ref[0] * y
    o_ref[...] = xn * lax.rsqrt(jnp.mean(xn * xn, axis=-1, keepdims=True) + RMS_EPS) * fg_ref[...]


def _out_proj(mix, w_bf, wl, x2, mod3, final_g=None, *, tm=512):
    n_tok = x2.shape[0]
    per_b = SEQ // tm
    in_specs = [pl.BlockSpec((tm, D), lambda i: (i, 0)),
                pl.BlockSpec((None, D, D), lambda i: (wl, 0, 0)),
                pl.BlockSpec((tm, D), lambda i: (i, 0)),
                pl.BlockSpec((1, 1, D), lambda i: (i // per_b, 0, 2))]
    args = [mix, w_bf, x2, mod3]
    kern = _out_proj_kernel
    if final_g is not None:
        in_specs.append(pl.BlockSpec((1, D), lambda i: (0, 0)))
        args.append(final_g)
        kern = _out_proj_final_kernel
    return pl.pallas_call(
        kern,
        out_shape=jax.ShapeDtypeStruct((n_tok, D), F32),
        grid=(n_tok // tm,),
        in_specs=in_specs,
        out_specs=pl.BlockSpec((tm, D), lambda i: (i, 0)),
        compiler_params=_cparams("parallel"),
        name="out_proj_final" if final_g is not None else "out_proj",
    )(*args)


def _sg_mix_kernel(u_ref, v_ref, z_ref, lng_ref, lnb_ref, ws_ref, bs_ref, o_ref, *, tm):
    ti = lax.broadcasted_iota(jnp.int32, (SG_CHUNK, SG_CHUNK), 0)
    si = lax.broadcasted_iota(jnp.int32, (SG_CHUNK, SG_CHUNK), 1)
    causal = ti >= si
    ws = [jnp.where(causal, ws_ref[g], 0.0).astype(BF16) for g in range(SG_GROUPS)]
    for c in range(tm // SG_CHUNK):
        rows = pl.ds(c * SG_CHUNK, SG_CHUNK)
        v = v_ref[rows, :].astype(F32)
        mean = jnp.mean(v, axis=-1, keepdims=True)
        d = v - mean
        var = jnp.mean(d * d, axis=-1, keepdims=True)
        vn = (d * lax.rsqrt(var + LN_EPS) * lng_ref[...] + lnb_ref[...]).astype(BF16)
        f = [jnp.dot(ws[g], vn[:, g * LANES:(g + 1) * LANES], preferred_element_type=F32) for g in range(SG_GROUPS)]
        for g in range(SG_GROUPS):
            cols = pl.ds(g * LANES, LANES)
            uz = u_ref[rows, cols].astype(F32) * z_ref[rows, cols].astype(F32)
            o_ref[rows, cols] = (uz * (f[g] + bs_ref[g])).astype(BF16)


def _sg_mix(p, ln_g, ln_b, w_s, b_full, *, tm=512):
    n_tok = p.shape[0]
    return pl.pallas_call(
        functools.partial(_sg_mix_kernel, tm=tm),
        out_shape=jax.ShapeDtypeStruct((n_tok, D), BF16),
        grid=(n_tok // tm,),
        in_specs=[pl.BlockSpec((tm, D), lambda i: (i, 0)),
                  pl.BlockSpec((tm, D), lambda i: (i, 1)),
                  pl.BlockSpec((tm, D), lambda i: (i, 2)),
                  pl.BlockSpec((1, D), lambda i: (0, 0)),
                  pl.BlockSpec((1, D), lambda i: (0, 0)),
                  pl.BlockSpec((SG_GROUPS, SG_CHUNK, SG_CHUNK), lambda i: (0, 0, 0)),
                  pl.BlockSpec((SG_GROUPS, SG_CHUNK, LANES), lambda i: (0, 0, 0))],
        out_specs=pl.BlockSpec((tm, D), lambda i: (i, 0)),
        compiler_params=_cparams("parallel"),
        name="sg_mix",
    )(p, p, p, ln_g, ln_b, w_s, b_full)


def _first_half(shape):
    return (lax.broadcasted_iota(jnp.int32, shape, 1) % HEAD) < (HEAD // 2)


def _rope_table_kernel(pos_ref, invf_ref, cos_ref, sin_ref):
    ang = pos_ref[...].astype(F32) * invf_ref[...]
    cos_ref[...] = jnp.cos(ang)
    sin_ref[...] = jnp.where(_first_half(ang.shape), -jnp.sin(ang), jnp.sin(ang))


def _rope_tables(pos2, invf, *, tm=1024):
    n_tok = pos2.shape[0]
    spec = pl.BlockSpec((tm, LANES), lambda i: (i, 0))
    return pl.pallas_call(
        _rope_table_kernel,
        out_shape=(jax.ShapeDtypeStruct((n_tok, LANES), F32),) * 2,
        grid=(n_tok // tm,),
        in_specs=[pl.BlockSpec((tm, 1), lambda i: (i, 0)), pl.BlockSpec((1, LANES), lambda i: (0, 0))],
        out_specs=(spec, spec),
        compiler_params=_cparams("parallel"),
        name="rope_tables",
    )(pos2, invf)


def _rope(x, cos, sin_signed, first):
    partner = jnp.where(first, pltpu.roll(x, LANES - HEAD // 2, 1), pltpu.roll(x, HEAD // 2, 1))
    return x * cos + partner * sin_signed


def _swa_mix_kernel(sink_ref, p_ref, kvp_ref, cos_ref, sin_ref, cosp_ref, sinp_ref, o_ref):
    i = pl.program_id(0)
    first_key = jnp.where((i % (SEQ // SWA_BLOCK)) != 0, 0, SWA_BLOCK)
    cos, sin_s, cosp, sinp_s = cos_ref[...], sin_ref[...], cosp_ref[...], sinp_ref[...]
    first = _first_half(cos.shape)
    lane = lax.broadcasted_iota(jnp.int32, (2 * SWA_BLOCK, LANES), 1)
    low = lane < HEAD

    qi = lax.broadcasted_iota(jnp.int32, (SWA_BLOCK, 2 * SWA_BLOCK), 0)
    kj = lax.broadcasted_iota(jnp.int32, (SWA_BLOCK, 2 * SWA_BLOCK), 1)
    rel = qi + SWA_BLOCK - kj
    mask = (rel >= 0) & (rel < SWA_BLOCK) & (kj >= first_key)

    k_lo, k_hi, v_lo, v_hi = [], [], [], []
    for kp in range(SWA_KV // 2):
        ksl = pl.ds(kp * LANES, LANES)
        vsl = pl.ds(SWA_KVW + kp * LANES, LANES)
        k2 = jnp.concatenate([_rope(kvp_ref[:, ksl], cosp, sinp_s, first),
                              _rope(p_ref[:, pl.ds(D + kp * LANES, LANES)], cos, sin_s, first)], axis=0)
        v2 = jnp.concatenate([kvp_ref[:, vsl], p_ref[:, pl.ds(D + SWA_KVW + kp * LANES, LANES)]], axis=0)
        k2r = pltpu.roll(k2, HEAD, 1)
        v2r = pltpu.roll(v2, HEAD, 1)
        for sub in range(2):
            src_k, src_kr = (k2, k2r) if sub == 0 else (k2r, k2)
            src_v, src_vr = (v2, v2r) if sub == 0 else (v2r, v2)
            k_lo.append(jnp.where(low, src_k, 0.0).astype(BF16))
            k_hi.append(jnp.where(low, 0.0, src_kr).astype(BF16))
            v_lo.append(jnp.where(low, src_v, 0.0).astype(BF16))
            v_hi.append(jnp.where(low, 0.0, src_vr).astype(BF16))

    npair = SWA_HEADS // 2
    cols = [pl.ds(pr * LANES, LANES) for pr in range(npair)]
    q2 = [(_rope(p_ref[:, cols[pr]], cos, sin_s, first) * (HEAD ** -0.5)).astype(BF16) for pr in range(npair)]
    s = [lax.dot_general(q2[h // 2], (k_hi if h % 2 else k_lo)[h // 8], (((1,), (1,)), ((), ())),
                         preferred_element_type=F32) for h in range(SWA_HEADS)]
    hs = range(SWA_HEADS)
    s = [jnp.where(mask, s[h], -jnp.inf) for h in hs]
    m = [jnp.maximum(jnp.max(s[h], axis=-1, keepdims=True), sink_ref[h]) for h in hs]
    e = [jnp.exp(s[h] - m[h]) for h in hs]
    denom = [jnp.sum(e[h], axis=-1, keepdims=True) + jnp.exp(sink_ref[h] - m[h]) for h in hs]
    prob = [(e[h] * (1.0 / denom[h])).astype(BF16) for h in hs]
    o = [jnp.dot(prob[h], (v_hi if h % 2 else v_lo)[h // 8], preferred_element_type=F32) for h in range(SWA_HEADS)]
    zoff = D + 2 * SWA_KVW
    for pr in range(npair):
        gate = p_ref[:, pl.ds(zoff + pr * LANES, LANES)]
        o_ref[:, cols[pr]] = ((o[2 * pr] + o[2 * pr + 1]) * gate).astype(BF16)


def _swa_mix(p, cos_t, sin_t, sinks):
    n_tok, ncols = p.shape
    tb = SWA_BLOCK
    kvb = D // (2 * SWA_KVW)
    cur = pl.BlockSpec((tb, LANES), lambda i: (i, 0))
    prev = pl.BlockSpec((tb, LANES), lambda i: (jnp.maximum(i - 1, 0), 0))
    return pl.pallas_call(
        _swa_mix_kernel,
        out_shape=jax.ShapeDtypeStruct((n_tok, D), BF16),
        grid=(n_tok // tb,),
        in_specs=[pl.BlockSpec(memory_space=pltpu.SMEM),
                  pl.BlockSpec((tb, ncols), lambda i: (i, 0)),
                  pl.BlockSpec((tb, 2 * SWA_KVW), lambda i: (jnp.maximum(i - 1, 0), kvb)),
                  cur, cur, prev, prev],
        out_specs=pl.BlockSpec((tb, D), lambda i: (i, 0)),
        compiler_params=_cparams("parallel"),
        name="swa_mix",
    )(sinks, p, p, cos_t, sin_t, cos_t, sin_t)


def _head_sums(xs, seg):
    out = jnp.dot(jnp.concatenate([x.astype(BF16) for x in xs], axis=0), seg, preferred_element_type=F32)
    n = xs[0].shape[0]
    return [out[i * n:(i + 1) * n] for i in range(len(xs))]


def _cumsum_rows(x, row):
    shift = 1
    while shift < x.shape[0]:
        x = x + jnp.where(row >= shift, pltpu.roll(x, shift, 0), 0.0)
        shift *= 2
    return x


def _bd(x, same_head):
    t = jnp.concatenate([x] * RW_HG, axis=0)
    return jnp.where(same_head, t, 0.0).astype(BF16)


def _diag_blocks(o, lane_head):
    acc = o[0:HEAD]
    for h in range(1, RW_HG):
        acc = jnp.where(lane_head == h, o[h * HEAD:(h + 1) * HEAD], acc)
    return acc


def _dot(a, b):
    return jnp.dot(a.astype(BF16), b.astype(BF16), preferred_element_type=F32)


def _dot_nt(a, b):
    return lax.dot_general(a, b, (((1,), (1,)), ((), ())), preferred_element_type=F32)


def _dot_tn(a, b):
    return lax.dot_general(a, b, (((0,), (0,)), ((), ())), preferred_element_type=F32)


def _rwkv_groups(r, lw, cum, k, v, kap, bet, h0, masks):
    incl, strict, eye, same_head, lane_head = masks
    c = RW_CHUNK
    gr = range(len(r))
    bd = lambda t: _bd(t, same_head)
    cref = [cum[g][c // 2 - 1:c // 2] for g in gr]
    cend = [cum[g][c - 1:c] for g in gr]
    r_true = [r[g] * jnp.exp(cum[g]) for g in gr]
    kap_true = [kap[g] * jnp.exp(cum[g] - lw[g]) for g in gr]
    r_s = [r[g] * jnp.exp(cum[g] - cref[g]) for g in gr]
    kap_s = [kap[g] * jnp.exp(cum[g] - lw[g] - cref[g]) for g in gr]
    ginv = [jnp.exp(cref[g] - cum[g]) for g in gr]
    eend = [jnp.exp(cend[g] - cum[g]) for g in gr]

    s = [_dot_nt(jnp.concatenate([r_s[g], kap_s[g]], axis=0).astype(BF16),
                 jnp.concatenate([bd(k[g] * ginv[g]), bd(bet[g] * ginv[g])], axis=0)) for g in gr]
    a_rk = [jnp.where(incl, s[g][:c, :RW_GW], 0.0) for g in gr]
    a_rb = [jnp.where(incl, s[g][:c, RW_GW:], 0.0) for g in gr]
    a_kk = [jnp.where(strict, s[g][c:, :RW_GW], 0.0) for g in gr]
    a_kb = [jnp.where(strict, s[g][c:, RW_GW:], 0.0) for g in gr]

    x = [eye - a_kb[g] for g in gr]
    p = [_dot(a_kb[g], bd(a_kb[g])) for g in gr]
    akv = [_dot(a_kk[g], bd(v[g])) for g in gr]
    for _ in range(4):
        xp = [_dot(jnp.concatenate([x[g], p[g]], axis=0), bd(p[g])) for g in gr]
        x = [x[g] + xp[g][:c] for g in gr]
        p = [xp[g][c:] for g in gr]
    t_inv = [x[g] + _dot(x[g], bd(p[g])) for g in gr]

    wu = [-_dot(t_inv[g], jnp.concatenate([bd(kap_true[g]), bd(akv[g])], axis=1)) for g in gr]

    hw = [_dot(jnp.concatenate([wu[g][:, :RW_GW], r_true[g], eye * jnp.exp(cend[g])], axis=0), bd(h0[g]))
          for g in gr]
    u = [hw[g][:c] + wu[g][:, RW_GW:] for g in gr]
    y = [hw[g][c:2 * c] + _dot(jnp.concatenate([a_rk[g], a_rb[g]], axis=1),
                               jnp.concatenate([bd(v[g]), bd(u[g])], axis=0)) for g in gr]
    o2 = [_dot_tn(jnp.concatenate([k[g] * eend[g], bet[g] * eend[g]], axis=0).astype(BF16),
                  jnp.concatenate([v[g], u[g]], axis=0).astype(BF16)) for g in gr]
    h_new = [hw[g][2 * c:] + _diag_blocks(o2[g], lane_head) for g in gr]
    return y, h_new


def _rwkv_mix_kernel(p_ref, pp_ref, pl_ref, plp_ref, mu_ref, mul_ref, w0_ref, a0_ref, wl_ref, al_ref,
                     kk_ref, ka_ref, rk_ref, gng_ref, gnb_ref, o_ref, h_scr):
    ci = pl.program_id(1)
    c = RW_CHUNK

    @pl.when(ci == 0)
    def _():
        h_scr[...] = jnp.zeros_like(h_scr)

    not_first = (ci != 0).astype(F32)
    row = lax.broadcasted_iota(jnp.int32, (c, 1), 0)

    def shifted(cur, prev_ref):
        prev_row = prev_ref[7:8, :] * not_first
        rolled = pltpu.roll(cur, 1, 0)
        return jnp.where(row == 0, prev_row, rolled)

    lcur = pl_ref[...]
    lcur = lcur + (shifted(lcur, plp_ref) - lcur) * mul_ref[...]
    dw = jnp.tanh(lcur[:, :RW_LP // 2])
    da = lcur[:, RW_LP // 2:]

    ti = lax.broadcasted_iota(jnp.int32, (c, RW_GW), 0)
    li = lax.broadcasted_iota(jnp.int32, (c, RW_GW), 1)
    si = li % HEAD
    lane_head = li // HEAD
    rb = lax.broadcasted_iota(jnp.int32, (RW_GW, RW_GW), 0) // HEAD
    cb = lax.broadcasted_iota(jnp.int32, (RW_GW, RW_GW), 1) // HEAD
    same_head = rb == cb
    seg = same_head.astype(BF16)
    masks = (ti >= si, ti > si, (ti == si).astype(F32), same_head, lane_head)

    gr = range(RW_GROUPS)
    gsl = [pl.ds(g * RW_GW, RW_GW) for g in gr]

    def lerp(off, g):
        cols = pl.ds(off + g * RW_GW, RW_GW)
        cur = p_ref[:, cols]
        return cur + (shifted(cur, pp_ref.at[:, cols]) - cur) * mu_ref[:, cols]

    r = [lerp(0, g) for g in gr]
    k = [lerp(D, g) for g in gr]
    v = [lerp(2 * D, g) for g in gr]
    lw = [-DECAY_SCALE * jax.nn.sigmoid(w0_ref[:, gsl[g]] + _dot(dw, wl_ref[:, gsl[g]])) for g in gr]
    a = [jax.nn.sigmoid(a0_ref[:, gsl[g]] + _dot(da, al_ref[:, gsl[g]])) for g in gr]
    kap = [k[g] * kk_ref[:, gsl[g]] for g in gr]
    k = [k[g] * (1.0 + (a[g] - 1.0) * ka_ref[:, gsl[g]]) for g in gr]
    sums = _head_sums([kap[g] * kap[g] for g in gr] + [r[g] * k[g] * rk_ref[:, gsl[g]] for g in gr], seg)
    kap = [kap[g] / jnp.maximum(jnp.sqrt(sums[g]), 1e-12) for g in gr]
    bonus = sums[RW_GROUPS:]
    bet = [kap[g] * a[g] for g in gr]
    cum = [_cumsum_rows(lw[g], row) for g in gr]

    y, h_new = [], []
    for lo in range(0, RW_GROUPS, RW_STAGE):
        sl = slice(lo, lo + RW_STAGE)
        ys, hs = _rwkv_groups(r[sl], lw[sl], cum[sl], k[sl], v[sl], kap[sl], bet[sl],
                              [h_scr[g] for g in range(lo, lo + RW_STAGE)], masks)
        y += ys
        h_new += hs
    for g in gr:
        h_scr[g] = h_new[g]

    ysum = _head_sums(y, seg)
    dlt = [y[g] - ysum[g] * (1.0 / HEAD) for g in gr]
    dsum = _head_sums([dlt[g] * dlt[g] for g in gr], seg)
    for g in gr:
        rstd = lax.rsqrt(dsum[g] * (1.0 / HEAD) + GN_EPS)
        yn = dlt[g] * rstd * gng_ref[:, gsl[g]] + gnb_ref[:, gsl[g]] + bonus[g] * v[g]
        o_ref[:, gsl[g]] = (yn * jax.nn.silu(lerp(3 * D, g))).astype(BF16)


def _rwkv_mix(p, p_lora, mu, mu_l, w0, a0, w_lora, a_lora, k_k, k_a, r_k, gn_g, gn_b):
    n_tok = p.shape[0]
    c = RW_CHUNK
    nc = SEQ // c
    ncol = p.shape[1]
    nl = p_lora.shape[1]
    row = lambda b, ci: (b * nc + ci, 0)
    prev = lambda b, ci: (jnp.maximum((b * nc + ci) * (c // 8) - 1, 0), 0)
    vec = pl.BlockSpec((1, D), lambda b, ci: (0, 0))
    lora = pl.BlockSpec((RW_LP // 2, D), lambda b, ci: (0, 0))
    return pl.pallas_call(
        _rwkv_mix_kernel,
        out_shape=jax.ShapeDtypeStruct((n_tok, D), BF16),
        grid=(n_tok // SEQ, nc),
        in_specs=[pl.BlockSpec((c, ncol), row),
                  pl.BlockSpec((8, ncol), prev),
                  pl.BlockSpec((c, nl), row),
                  pl.BlockSpec((8, nl), prev),
                  pl.BlockSpec((1, ncol), lambda b, ci: (0, 0)),
                  pl.BlockSpec((1, nl), lambda b, ci: (0, 0)),
                  vec, vec, lora, lora,
                  vec, vec, vec, vec, vec],
        out_specs=pl.BlockSpec((c, D), row),
        scratch_shapes=[pltpu.VMEM((RW_GROUPS, HEAD, RW_GW), F32)],
        compiler_params=_cparams("parallel", "arbitrary"),
        name="rwkv_mix",
    )(p, p, p_lora, p_lora, mu, mu_l, w0, a0, w_lora, a_lora, k_k, k_a, r_k, gn_g, gn_b)


def kernel(x, c, positions, norm_g, mod_w, mod_b, final_norm_g, sg_w_in, sg_w_out, sg_ln_g, sg_ln_b, sg_w_spatial,
           sg_b_spatial, swa_w_in, swa_w_out, swa_sinks, rwkv_w_in, rwkv_w_out, rwkv_mu, rwkv_w0, rwkv_w_lora,
           rwkv_a0, rwkv_a_lora, rwkv_k_k, rwkv_k_a, rwkv_r_k, rwkv_gn_g, rwkv_gn_b):
    batch, seq, d = x.shape
    assert (seq, d) == (SEQ, D) and norm_g.shape[0] == DEPTH
    n_tok = batch * seq
    x2 = x.reshape(n_tok, D)
    c8 = jnp.zeros((8, D), F32).at[:batch].set(c)
    mod = _modulation(c8, mod_w, mod_b)
    pos2 = positions.reshape(n_tok, 1)
    half = HEAD // 2
    inv_freq = ROPE_THETA ** (-jnp.arange(half, dtype=F32) / half)
    invf = jnp.tile(inv_freq, LANES // half).reshape(1, LANES)
    row = lambda a: a.reshape(1, -1)

    def act_rows(spans):
        rows = [jnp.concatenate([jnp.full((w,), val[t], F32) for w, *val in spans]) for t in range(3)]
        return jnp.zeros((8, rows[0].shape[0]), F32).at[:3].set(jnp.stack(rows))

    gelu_span = (GELU_A, GELU_B, 1.0)
    silu_span = (0.5, 0.0, 1.0)
    sg_in_bf, swa_in_bf, rwkv_in_bf = sg_w_in.astype(BF16), swa_w_in.astype(BF16), rwkv_w_in.astype(BF16)
    w_out_bf =(sg_w_out.astype(BF16), swa_w_out.astype(BF16), rwkv_w_out.astype(BF16))
    cos_t, sin_t = _rope_tables(pos2, invf)
    for i in range(DEPTH):
        kind, j = i % 3, i // 3
        mod3 = mod[i, :batch].reshape(batch, 1, 3 * D)
        g = row(norm_g[i])
        if kind == 0:
            act = act_rows([(2 * D, *gelu_span), (D, *silu_span)])
            p = _in_proj_act(x2, g, mod3, sg_in_bf, j, 3 * D, act, False, BF16)
            mix = _sg_mix(p, row(sg_ln_g[j]), row(sg_ln_b[j]), sg_w_spatial[j],
                          jnp.broadcast_to(sg_b_spatial[j][:, :, None], (SG_GROUPS, SG_CHUNK, LANES)))
        elif kind == 1:
            act = act_rows([(D + 2 * SWA_KVW, 0.0, 0.0, 0.0), (D, *silu_span)])
            p = _in_proj_act(x2, g, mod3, swa_in_bf, j, 2 * D + 2 * SWA_KVW, act, True, F32)
            mix = _swa_mix(p, cos_t, sin_t, swa_sinks[j])
        else:
            lp = RW_LP // 2 - RW_LORA
            w_in = rwkv_in_bf[j]
            w_l = jnp.concatenate([jnp.pad(w_in[:, 4 * D:4 * D + RW_LORA], ((0, 0), (0, lp))),
                                   jnp.pad(w_in[:, 4 * D + RW_LORA:], ((0, 0), (0, lp)))], axis=1)
            p, p_lora = _in_proj(x2, g, mod3, rwkv_in_bf, j, 4 * D, w_l)
            mu = rwkv_mu[j]
            mu_l = jnp.concatenate([jnp.pad(mu[4 * D:4 * D + RW_LORA], (0, lp)), jnp.pad(mu[4 * D + RW_LORA:], (0, lp))])
            wl_pad = jnp.pad(rwkv_w_lora[j], ((0, lp), (0, 0))).astype(BF16)
            al_pad = jnp.pad(rwkv_a_lora[j], ((0, lp), (0, 0))).astype(BF16)
            mix = _rwkv_mix(p, p_lora, row(mu[:4 * D]), row(mu_l), row(rwkv_w0[j]),
                            row(rwkv_a0[j]), wl_pad, al_pad, row(rwkv_k_k[j]),
                            row(rwkv_k_a[j]), row(rwkv_r_k[j]), row(rwkv_gn_g[j]), row(rwkv_gn_b[j]))
        x2 = _out_proj(mix, w_out_bf[kind], j, x2, mod3, row(final_norm_g) if i == DEPTH - 1 else None)
    return x2.reshape(batch, seq, D)
```

```python
import functools
import math

import jax
import jax.numpy as jnp
from jax import lax
from jax.experimental import pallas as pl
from jax.experimental.pallas import tpu as pltpu

F32 = jnp.float32
BF16 = jnp.bfloat16

D = 2048
SEQ = 4096
DEPTH = 4
HEAD = 64
LANES = 128
SG_CHUNK = 128
SG_GROUPS = 16
SWA_HEADS = D // HEAD
SWA_KV = SWA_HEADS // 8
SWA_KVW = SWA_KV * HEAD
SWA_BLOCK = 128
ROPE_THETA = 10000.0
RW_HEADS = D // HEAD
RW_LORA = 96
RW_LP = 256
RW_CHUNK = 64
RW_HG = 4
RW_GW = RW_HG * HEAD
RW_GROUPS = D // RW_GW
RW_STAGE = 8
DECAY_SCALE = math.exp(-0.5)
GN_EPS = 64e-5
RMS_EPS = 1e-6
LN_EPS = 1e-5

VMEM_LIMIT = 48 * 1024 * 1024


def _cparams(*sem, vmem=VMEM_LIMIT):
    return pltpu.CompilerParams(dimension_semantics=sem, vmem_limit_bytes=vmem)


def _mod_kernel(c_ref, w_ref, b_ref, pos_ref, invf_ref, o_ref, cos_ref, sin_ref):
    cond = jax.nn.silu(c_ref[...]).astype(BF16)
    o_ref[0] = jnp.dot(cond, w_ref[0].astype(BF16), preferred_element_type=F32) + b_ref[0]
    _rope_table_kernel(pos_ref, invf_ref, cos_ref, sin_ref)


def _modulation(c8, mod_w, mod_b, pos2, invf):
    tn = 768
    n = mod_w.shape[2]
    nj = n // tn
    n_tok = pos2.shape[0]
    rows = n_tok // (DEPTH * nj)
    assert n % tn == 0 and rows * DEPTH * nj == n_tok and rows % 8 == 0
    table = pl.BlockSpec((rows, LANES), lambda l, j: (l * nj + j, 0))
    return pl.pallas_call(
        _mod_kernel,
        out_shape=(jax.ShapeDtypeStruct((DEPTH, 8, n), F32),
                   jax.ShapeDtypeStruct((n_tok, LANES), F32), jax.ShapeDtypeStruct((n_tok, LANES), F32)),
        grid=(DEPTH, nj),
        in_specs=[pl.BlockSpec((8, D), lambda l, j: (0, 0)),
                  pl.BlockSpec((1, D, tn), lambda l, j: (l, 0, j)),
                  pl.BlockSpec((1, 1, tn), lambda l, j: (l, 0, j)),
                  pl.BlockSpec((rows, 1), lambda l, j: (l * nj + j, 0)),
                  pl.BlockSpec((1, LANES), lambda l, j: (0, 0))],
        out_specs=(pl.BlockSpec((1, 8, tn), lambda l, j: (l, 0, j)), table, table),
        compiler_params=_cparams("parallel", "parallel"),
        name="modulation",
    )(c8, mod_w, mod_b.reshape(DEPTH, 1, n), pos2, invf)


NORM_ROWS = 128


def _row_rsqrt(x_ref, rs_scr):
    def body(i, carry):
        rows = pl.ds(pl.multiple_of(i * NORM_ROWS, NORM_ROWS), NORM_ROWS)
        x = x_ref[rows, :]
        rs_scr[rows, :] = lax.rsqrt(jnp.mean(x * x, axis=-1, keepdims=True) + RMS_EPS)
        return carry
    lax.fori_loop(0, x_ref.shape[0] // NORM_ROWS, body, 0)


def _modulated(x_ref, rs_scr, g_ref, sc_ref, sh_ref):
    gain = g_ref[...] * (1.0 + sc_ref[0])
    return (x_ref[...] * rs_scr[...] * gain + sh_ref[0]).astype(BF16)


GELU_A = math.sqrt(2.0 / math.pi)
GELU_B = GELU_A * 0.044715
ACT_COLS = 256


def _in_proj_act_kernel(x_ref, g_ref, sc_ref, sh_ref, w_ref, act_ref, o_ref, rs_scr, *, select):
    @pl.when(pl.program_id(1) == 0)
    def _():
        _row_rsqrt(x_ref, rs_scr)

    h = _modulated(x_ref, rs_scr, g_ref, sc_ref, sh_ref)
    for n in range(o_ref.shape[1] // ACT_COLS):
        cols = pl.ds(n * ACT_COLS, ACT_COLS)
        p = jnp.dot(h, w_ref[:, cols], preferred_element_type=F32)
        gated = (0.5 * p) * (1.0 + jnp.tanh(p * (act_ref[0:1, cols] + act_ref[1:2, cols] * (p * p))))
        o_ref[:, cols] = (jnp.where(act_ref[2:3, cols] > 0.0, gated, p) if select else gated).astype(o_ref.dtype)


def _in_proj_lora_kernel(x_ref, g_ref, sc_ref, sh_ref, w_ref, wl_ref, o_ref, ol_ref, rs_scr):
    @pl.when(pl.program_id(1) == 0)
    def _():
        _row_rsqrt(x_ref, rs_scr)
        ol_ref[...] = jnp.dot(_modulated(x_ref, rs_scr, g_ref, sc_ref, sh_ref), wl_ref[...],
                              preferred_element_type=F32)

    h = _modulated(x_ref, rs_scr, g_ref, sc_ref, sh_ref)
    o_ref[...] = jnp.dot(h, w_ref[...], preferred_element_type=F32)


def _in_proj_act(x2, g, mod3, w_bf, wl, ncols, act, select, out_dtype, *, tm=1024):
    n_tok = x2.shape[0]
    tn = 2048 if (out_dtype == BF16 and ncols % 2048 == 0) else (1024 if ncols % 1024 == 0 else 1536)
    assert ncols % tn == 0
    per_b = SEQ // tm
    return pl.pallas_call(
        functools.partial(_in_proj_act_kernel, select=select),
        out_shape=jax.ShapeDtypeStruct((n_tok, ncols), out_dtype),
        grid=(n_tok // tm, ncols // tn),
        in_specs=[pl.BlockSpec((tm, D), lambda i, j: (i, 0)),
                  pl.BlockSpec((1, D), lambda i, j: (0, 0)),
                  pl.BlockSpec((1, 1, D), lambda i, j: (i // per_b, 0, 1)),
                  pl.BlockSpec((1, 1, D), lambda i, j: (i // per_b, 0, 0)),
                  pl.BlockSpec((None, D, tn), lambda i, j: (wl, 0, j)),
                  pl.BlockSpec((8, tn), lambda i, j: (0, j))],
        out_specs=pl.BlockSpec((tm, tn), lambda i, j: (i, j)),
        scratch_shapes=[pltpu.VMEM((tm, 1), F32)],
        compiler_params=_cparams("parallel", "arbitrary"),
        name="in_proj_act",
    )(x2, g, mod3, mod3, w_bf, act)


def _in_proj(x2, g, mod3, w_bf, wl, ncols, w_lora_bf, *, tm=1024):
    n_tok = x2.shape[0]
    tn = 2048
    assert ncols % tn == 0
    per_b = SEQ // tm
    x_spec = pl.BlockSpec((tm, D), lambda i, j: (i, 0))
    g_spec = pl.BlockSpec((1, D), lambda i, j: (0, 0))
    sh_spec = pl.BlockSpec((1, 1, D), lambda i, j: (i // per_b, 0, 0))
    sc_spec = pl.BlockSpec((1, 1, D), lambda i, j: (i // per_b, 0, 1))
    w_spec = pl.BlockSpec((None, D, tn), lambda i, j: (wl, 0, j))
    o_spec = pl.BlockSpec((tm, tn), lambda i, j: (i, j))
    grid = (n_tok // tm, ncols // tn)
    scratch = [pltpu.VMEM((tm, 1), F32)]
    nl = w_lora_bf.shape[1]
    vmem = 2 * (4 * tm * D + 2 * D * tn + 4 * tm * tn + 2 * D * nl + 4 * tm * nl) + 4 * 1024 * 1024
    return pl.pallas_call(
        _in_proj_lora_kernel,
        out_shape=(jax.ShapeDtypeStruct((n_tok, ncols), F32), jax.ShapeDtypeStruct((n_tok, nl), F32)),
        grid=grid,
        in_specs=[x_spec, g_spec, sc_spec, sh_spec, w_spec, pl.BlockSpec((D, nl), lambda i, j: (0, 0))],
        out_specs=(o_spec, pl.BlockSpec((tm, nl), lambda i, j: (i, 0))),
        scratch_shapes=scratch,
        compiler_params=_cparams("parallel", "arbitrary", vmem=vmem),
        name="in_proj_lora",
    )(x2, g, mod3, mod3, w_bf, w_lora_bf)


CAST_ROWS = 256


def _round_weight(w_ref, w_scr):
    @pl.when(pl.program_id(0) == 0)
    def _():
        def body(i, carry):
            rows = pl.ds(pl.multiple_of(i * CAST_ROWS, CAST_ROWS), CAST_ROWS)
            w_scr[rows, :] = w_ref[rows, :].astype(BF16)
            return carry
        lax.fori_loop(0, w_ref.shape[0] // CAST_ROWS, body, 0)


def _out_proj_kernel(m_ref, w_ref, x_ref, gate_ref, o_ref, w_scr):
    _round_weight(w_ref, w_scr)
    y = jnp.dot(m_ref[...], w_scr[...], preferred_element_type=F32)
    o_ref[...] = x_ref[...] + gate_ref[0] * y


def _out_proj_final_kernel(m_ref, w_ref, x_ref, gate_ref, fg_ref, o_ref, w_scr):
    _round_weight(w_ref, w_scr)
    y = jnp.dot(m_ref[...], w_scr[...], preferred_element_type=F32)
    xn = x_ref[...] + gate_ref[0] * y
    o_ref[...] = xn * lax.rsqrt(jnp.mean(xn * xn, axis=-1, keepdims=True) + RMS_EPS) * fg_ref[...]


def _out_proj(mix, w, wl, x2, mod3, final_g=None, *, tm=512):
    n_tok = x2.shape[0]
    per_b = SEQ // tm
    in_specs = [pl.BlockSpec((tm, D), lambda i: (i, 0)),
                pl.BlockSpec((None, D, D), lambda i: (wl, 0, 0), pipeline_mode=pl.Buffered(1)),
                pl.BlockSpec((tm, D), lambda i: (i, 0)),
                pl.BlockSpec((1, 1, D), lambda i: (i // per_b, 0, 2))]
    args = [mix, w, x2, mod3]
    kern = _out_proj_kernel
    if final_g is not None:
        in_specs.append(pl.BlockSpec((1, D), lambda i: (0, 0)))
        args.append(final_g)
        kern = _out_proj_final_kernel
    return pl.pallas_call(
        kern,
        out_shape=jax.ShapeDtypeStruct((n_tok, D), F32),
        grid=(n_tok // tm,),
        in_specs=in_specs,
        out_specs=pl.BlockSpec((tm, D), lambda i: (i, 0)),
        scratch_shapes=[pltpu.VMEM((D, D), BF16)],
        compiler_params=_cparams("arbitrary", vmem=6 * D * D + 2 * (2 + 4 + 4) * tm * D + 8 * 1024 * 1024),
        name="out_proj_final" if final_g is not None else "out_proj",
    )(*args)


def _sg_mix_kernel(u_ref, v_ref, z_ref, lng_ref, lnb_ref, ws_ref, bs_ref, o_ref, *, tm):
    ti = lax.broadcasted_iota(jnp.int32, (SG_CHUNK, SG_CHUNK), 0)
    si = lax.broadcasted_iota(jnp.int32, (SG_CHUNK, SG_CHUNK), 1)
    causal = ti >= si
    ws = [jnp.where(causal, ws_ref[g], 0.0).astype(BF16) for g in range(SG_GROUPS)]
    for c in range(tm // SG_CHUNK):
        rows = pl.ds(c * SG_CHUNK, SG_CHUNK)
        v = v_ref[rows, :].astype(F32)
        mean = jnp.mean(v, axis=-1, keepdims=True)
        d = v - mean
        var = jnp.mean(d * d, axis=-1, keepdims=True)
        vn = (d * lax.rsqrt(var + LN_EPS) * lng_ref[...] + lnb_ref[...]).astype(BF16)
        f = [jnp.dot(ws[g], vn[:, g * LANES:(g + 1) * LANES], preferred_element_type=F32) for g in range(SG_GROUPS)]
        for g in range(SG_GROUPS):
            cols = pl.ds(g * LANES, LANES)
            uz = u_ref[rows, cols].astype(F32) * z_ref[rows, cols].astype(F32)
            o_ref[rows, cols] = (uz * (f[g] + bs_ref[g])).astype(BF16)


def _sg_mix(p, ln_g, ln_b, w_s, b_full, *, tm=512):
    n_tok = p.shape[0]
    return pl.pallas_call(
        functools.partial(_sg_mix_kernel, tm=tm),
        out_shape=jax.ShapeDtypeStruct((n_tok, D), BF16),
        grid=(n_tok // tm,),
        in_specs=[pl.BlockSpec((tm, D), lambda i: (i, 0)),
                  pl.BlockSpec((tm, D), lambda i: (i, 1)),
                  pl.BlockSpec((tm, D), lambda i: (i, 2)),
                  pl.BlockSpec((1, D), lambda i: (0, 0)),
                  pl.BlockSpec((1, D), lambda i: (0, 0)),
                  pl.BlockSpec((SG_GROUPS, SG_CHUNK, SG_CHUNK), lambda i: (0, 0, 0)),
                  pl.BlockSpec((SG_GROUPS, SG_CHUNK, LANES), lambda i: (0, 0, 0))],
        out_specs=pl.BlockSpec((tm, D), lambda i: (i, 0)),
        compiler_params=_cparams("parallel"),
        name="sg_mix",
    )(p, p, p, ln_g, ln_b, w_s, b_full)


def _first_half(shape):
    return (lax.broadcasted_iota(jnp.int32, shape, 1) % HEAD) < (HEAD // 2)


def _rope_table_kernel(pos_ref, invf_ref, cos_ref, sin_ref):
    ang = pos_ref[...].astype(F32) * invf_ref[...]
    cos_ref[...] = jnp.cos(ang)
    sin_ref[...] = jnp.where(_first_half(ang.shape), -jnp.sin(ang), jnp.sin(ang))


def _rope(x, cos, sin_signed, first):
    partner = jnp.where(first, pltpu.roll(x, LANES - HEAD // 2, 1), pltpu.roll(x, HEAD // 2, 1))
    return x * cos + partner * sin_signed


def _swa_mix_kernel(sink_ref, p_ref, kvp_ref, cos_ref, sin_ref, cosp_ref, sinp_ref, o_ref):
    i = pl.program_id(0)
    first_key = jnp.where((i % (SEQ // SWA_BLOCK)) != 0, 0, SWA_BLOCK)
    cos, sin_s, cosp, sinp_s = cos_ref[...], sin_ref[...], cosp_ref[...], sinp_ref[...]
    first = _first_half(cos.shape)
    lane = lax.broadcasted_iota(jnp.int32, (2 * SWA_BLOCK, LANES), 1)
    low = lane < HEAD

    qi = lax.broadcasted_iota(jnp.int32, (SWA_BLOCK, 2 * SWA_BLOCK), 0)
    kj = lax.broadcasted_iota(jnp.int32, (SWA_BLOCK, 2 * SWA_BLOCK), 1)
    rel = qi + SWA_BLOCK - kj
    mask = (rel >= 0) & (rel < SWA_BLOCK) & (kj >= first_key)

    k_lo, k_hi, v_lo, v_hi = [], [], [], []
    for kp in range(SWA_KV // 2):
        ksl = pl.ds(kp * LANES, LANES)
        vsl = pl.ds(SWA_KVW + kp * LANES, LANES)
        k2 = jnp.concatenate([_rope(kvp_ref[:, ksl], cosp, sinp_s, first),
                              _rope(p_ref[:, pl.ds(D + kp * LANES, LANES)], cos, sin_s, first)], axis=0)
        v2 = jnp.concatenate([kvp_ref[:, vsl], p_ref[:, pl.ds(D + SWA_KVW + kp * LANES, LANES)]], axis=0)
        k2r = pltpu.roll(k2, HEAD, 1)
        v2r = pltpu.roll(v2, HEAD, 1)
        for sub in range(2):
            src_k, src_kr = (k2, k2r) if sub == 0 else (k2r, k2)
            src_v, src_vr = (v2, v2r) if sub == 0 else (v2r, v2)
            k_lo.append(jnp.where(low, src_k, 0.0).astype(BF16))
            k_hi.append(jnp.where(low, 0.0, src_kr).astype(BF16))
            v_lo.append(jnp.where(low, src_v, 0.0).astype(BF16))
            v_hi.append(jnp.where(low, 0.0, src_vr).astype(BF16))

    npair = SWA_HEADS // 2
    cols = [pl.ds(pr * LANES, LANES) for pr in range(npair)]
    q2 = [(_rope(p_ref[:, cols[pr]], cos, sin_s, first) * (HEAD ** -0.5)).astype(BF16) for pr in range(npair)]
    s = [lax.dot_general(q2[h // 2], (k_hi if h % 2 else k_lo)[h // 8], (((1,), (1,)), ((), ())),
                         preferred_element_type=F32) for h in range(SWA_HEADS)]
    hs = range(SWA_HEADS)
    s = [jnp.where(mask, s[h], -jnp.inf) for h in hs]
    m = [jnp.maximum(jnp.max(s[h], axis=-1, keepdims=True), sink_ref[h]) for h in hs]
    e = [jnp.exp(s[h] - m[h]) for h in hs]
    denom = [jnp.sum(e[h], axis=-1, keepdims=True) + jnp.exp(sink_ref[h] - m[h]) for h in hs]
    prob = [(e[h] * (1.0 / denom[h])).astype(BF16) for h in hs]
    o = [jnp.dot(prob[h], (v_hi if h % 2 else v_lo)[h // 8], preferred_element_type=F32) for h in range(SWA_HEADS)]
    zoff = D + 2 * SWA_KVW
    for pr in range(npair):
        gate = p_ref[:, pl.ds(zoff + pr * LANES, LANES)]
        o_ref[:, cols[pr]] = ((o[2 * pr] + o[2 * pr + 1]) * gate).astype(BF16)


def _swa_mix(p, cos_t, sin_t, sinks):
    n_tok, ncols = p.shape
    tb = SWA_BLOCK
    kvb = D // (2 * SWA_KVW)
    cur = pl.BlockSpec((tb, LANES), lambda i: (i, 0))
    prev = pl.BlockSpec((tb, LANES), lambda i: (jnp.maximum(i - 1, 0), 0))
    return pl.pallas_call(
        _swa_mix_kernel,
        out_shape=jax.ShapeDtypeStruct((n_tok, D), BF16),
        grid=(n_tok // tb,),
        in_specs=[pl.BlockSpec(memory_space=pltpu.SMEM),
                  pl.BlockSpec((tb, ncols), lambda i: (i, 0)),
                  pl.BlockSpec((tb, 2 * SWA_KVW), lambda i: (jnp.maximum(i - 1, 0), kvb)),
                  cur, cur, prev, prev],
        out_specs=pl.BlockSpec((tb, D), lambda i: (i, 0)),
        compiler_params=_cparams("parallel"),
        name="swa_mix",
    )(sinks, p, p, cos_t, sin_t, cos_t, sin_t)


def _head_sums(xs, seg):
    out = jnp.dot(jnp.concatenate([x.astype(BF16) for x in xs], axis=0), seg, preferred_element_type=F32)
    n = xs[0].shape[0]
    return [out[i * n:(i + 1) * n] for i in range(len(xs))]


def _cumsum_rows(x, row):
    shift = 1
    while shift < x.shape[0]:
        x = x + jnp.where(row >= shift, pltpu.roll(x, shift, 0), 0.0)
        shift *= 2
    return x


def _bd(x, same_head):
    t = jnp.concatenate([x] * RW_HG, axis=0)
    return jnp.where(same_head, t, 0.0).astype(BF16)


def _diag_blocks(o, lane_head):
    acc = o[0:HEAD]
    for h in range(1, RW_HG):
        acc = jnp.where(lane_head == h, o[h * HEAD:(h + 1) * HEAD], acc)
    return acc


def _dot(a, b):
    return jnp.dot(a.astype(BF16), b.astype(BF16), preferred_element_type=F32)


def _dot_nt(a, b):
    return lax.dot_general(a, b, (((1,), (1,)), ((), ())), preferred_element_type=F32)


def _dot_tn(a, b):
    return lax.dot_general(a, b, (((0,), (0,)), ((), ())), preferred_element_type=F32)


def _rwkv_groups(r, lw, cum, k, v, kap, bet, h0, masks):
    incl, strict, eye, same_head, lane_head = masks
    c = RW_CHUNK
    gr = range(len(r))
    bd = lambda t: _bd(t, same_head)
    cref = [cum[g][c // 2 - 1:c // 2] for g in gr]
    cend = [cum[g][c - 1:c] for g in gr]
    r_true = [r[g] * jnp.exp(cum[g]) for g in gr]
    kap_true = [kap[g] * jnp.exp(cum[g] - lw[g]) for g in gr]
    r_s = [r[g] * jnp.exp(cum[g] - cref[g]) for g in gr]
    kap_s = [kap[g] * jnp.exp(cum[g] - lw[g] - cref[g]) for g in gr]
    ginv = [jnp.exp(cref[g] - cum[g]) for g in gr]
    eend = [jnp.exp(cend[g] - cum[g]) for g in gr]

    s = [_dot_nt(jnp.concatenate([r_s[g], kap_s[g]], axis=0).astype(BF16),
                 jnp.concatenate([bd(k[g] * ginv[g]), bd(bet[g] * ginv[g])], axis=0)) for g in gr]
    a_rk = [jnp.where(incl, s[g][:c, :RW_GW], 0.0) for g in gr]
    a_rb = [jnp.where(incl, s[g][:c, RW_GW:], 0.0) for g in gr]
    a_kk = [jnp.where(strict, s[g][c:, :RW_GW], 0.0) for g in gr]
    a_kb = [jnp.where(strict, s[g][c:, RW_GW:], 0.0) for g in gr]

    x = [eye - a_kb[g] for g in gr]
    p = [_dot(a_kb[g], bd(a_kb[g])) for g in gr]
    akv = [_dot(a_kk[g], bd(v[g])) for g in gr]
    for _ in range(4):
        xp = [_dot(jnp.concatenate([x[g], p[g]], axis=0), bd(p[g])) for g in gr]
        x = [x[g] + xp[g][:c] for g in gr]
        p = [xp[g][c:] for g in gr]
    t_inv = [x[g] + _dot(x[g], bd(p[g])) for g in gr]

    wu = [-_dot(t_inv[g], jnp.concatenate([bd(kap_true[g]), bd(akv[g])], axis=1)) for g in gr]

    hw = [_dot(jnp.concatenate([wu[g][:, :RW_GW], r_true[g], eye * jnp.exp(cend[g])], axis=0), bd(h0[g]))
          for g in gr]
    u = [hw[g][:c] + wu[g][:, RW_GW:] for g in gr]
    y = [hw[g][c:2 * c] + _dot(jnp.concatenate([a_rk[g], a_rb[g]], axis=1),
                               jnp.concatenate([bd(v[g]), bd(u[g])], axis=0)) for g in gr]
    o2 = [_dot_tn(jnp.concatenate([k[g] * eend[g], bet[g] * eend[g]], axis=0).astype(BF16),
                  jnp.concatenate([v[g], u[g]], axis=0).astype(BF16)) for g in gr]
    h_new = [hw[g][2 * c:] + _diag_blocks(o2[g], lane_head) for g in gr]
    return y, h_new


def _rwkv_mix_kernel(p_ref, pp_ref, pl_ref, plp_ref, mu_ref, mul_ref, w0_ref, a0_ref, wl_ref, al_ref,
                     kk_ref, ka_ref, rk_ref, gng_ref, gnb_ref, o_ref, h_scr):
    ci = pl.program_id(1)
    c = RW_CHUNK

    @pl.when(ci == 0)
    def _():
        h_scr[...] = jnp.zeros_like(h_scr)

    not_first = (ci != 0).astype(F32)
    row = lax.broadcasted_iota(jnp.int32, (c, 1), 0)

    def shifted(cur, prev_ref):
        prev_row = prev_ref[7:8, :] * not_first
        rolled = pltpu.roll(cur, 1, 0)
        return jnp.where(row == 0, prev_row, rolled)

    lcur = pl_ref[...]
    lcur = lcur + (shifted(lcur, plp_ref) - lcur) * mul_ref[...]
    dw = jnp.tanh(lcur[:, :RW_LP // 2])
    da = lcur[:, RW_LP // 2:]

    ti = lax.broadcasted_iota(jnp.int32, (c, RW_GW), 0)
    li = lax.broadcasted_iota(jnp.int32, (c, RW_GW), 1)
    si = li % HEAD
    lane_head = li // HEAD
    rb = lax.broadcasted_iota(jnp.int32, (RW_GW, RW_GW), 0) // HEAD
    cb = lax.broadcasted_iota(jnp.int32, (RW_GW, RW_GW), 1) // HEAD
    same_head = rb == cb
    seg = same_head.astype(BF16)
    masks = (ti >= si, ti > si, (ti == si).astype(F32), same_head, lane_head)

    gr = range(RW_GROUPS)
    gsl = [pl.ds(g * RW_GW, RW_GW) for g in gr]

    def lerp(off, g):
        cols = pl.ds(off + g * RW_GW, RW_GW)
        cur = p_ref[:, cols]
        return cur + (shifted(cur, pp_ref.at[:, cols]) - cur) * mu_ref[:, cols]

    r = [lerp(0, g) for g in gr]
    k = [lerp(D, g) for g in gr]
    v = [lerp(2 * D, g) for g in gr]
    lw = [-DECAY_SCALE * jax.nn.sigmoid(w0_ref[:, gsl[g]] + _dot(dw, wl_ref[:, gsl[g]])) for g in gr]
    a = [jax.nn.sigmoid(a0_ref[:, gsl[g]] + _dot(da, al_ref[:, gsl[g]])) for g in gr]
    kap = [k[g] * kk_ref[:, gsl[g]] for g in gr]
    k = [k[g] * (1.0 + (a[g] - 1.0) * ka_ref[:, gsl[g]]) for g in gr]
    sums = _head_sums([kap[g] * kap[g] for g in gr] + [r[g] * k[g] * rk_ref[:, gsl[g]] for g in gr], seg)
    kap = [kap[g] / jnp.maximum(jnp.sqrt(sums[g]), 1e-12) for g in gr]
    bonus = sums[RW_GROUPS:]
    bet = [kap[g] * a[g] for g in gr]
    cum = [_cumsum_rows(lw[g], row) for g in gr]

    y, h_new = [], []
    for lo in range(0, RW_GROUPS, RW_STAGE):
        sl = slice(lo, lo + RW_STAGE)
        ys, hs = _rwkv_groups(r[sl], lw[sl], cum[sl], k[sl], v[sl], kap[sl], bet[sl],
                              [h_scr[g] for g in range(lo, lo + RW_STAGE)], masks)
        y += ys
        h_new += hs
    for g in gr:
        h_scr[g] = h_new[g]

    ysum = _head_sums(y, seg)
    dlt = [y[g] - ysum[g] * (1.0 / HEAD) for g in gr]
    dsum = _head_sums([dlt[g] * dlt[g] for g in gr], seg)
    for g in gr:
        rstd = lax.rsqrt(dsum[g] * (1.0 / HEAD) + GN_EPS)
        yn = dlt[g] * rstd * gng_ref[:, gsl[g]] + gnb_ref[:, gsl[g]] + bonus[g] * v[g]
        o_ref[:, gsl[g]] = (yn * jax.nn.silu(lerp(3 * D, g))).astype(BF16)


def _rwkv_mix(p, p_lora, mu, mu_l, w0, a0, w_lora, a_lora, k_k, k_a, r_k, gn_g, gn_b):
    n_tok = p.shape[0]
    c = RW_CHUNK
    nc = SEQ // c
    ncol = p.shape[1]
    nl = p_lora.shape[1]
    row = lambda b, ci: (b * nc + ci, 0)
    prev = lambda b, ci: (jnp.maximum((b * nc + ci) * (c // 8) - 1, 0), 0)
    vec = pl.BlockSpec((1, D), lambda b, ci: (0, 0))
    lora = pl.BlockSpec((RW_LP // 2, D), lambda b, ci: (0, 0))
    return pl.pallas_call(
        _rwkv_mix_kernel,
        out_shape=jax.ShapeDtypeStruct((n_tok, D), BF16),
        grid=(n_tok // SEQ, nc),
        in_specs=[pl.BlockSpec((c, ncol), row),
                  pl.BlockSpec((8, ncol), prev),
                  pl.BlockSpec((c, nl), row),
                  pl.BlockSpec((8, nl), prev),
                  pl.BlockSpec((1, ncol), lambda b, ci: (0, 0)),
                  pl.BlockSpec((1, nl), lambda b, ci: (0, 0)),
                  vec, vec, lora, lora,
                  vec, vec, vec, vec, vec],
        out_specs=pl.BlockSpec((c, D), row),
        scratch_shapes=[pltpu.VMEM((RW_GROUPS, HEAD, RW_GW), F32)],
        compiler_params=_cparams("parallel", "arbitrary"),
        name="rwkv_mix",
    )(p, p, p_lora, p_lora, mu, mu_l, w0, a0, w_lora, a_lora, k_k, k_a, r_k, gn_g, gn_b)


def kernel(x, c, positions, norm_g, mod_w, mod_b, final_norm_g, sg_w_in, sg_w_out, sg_ln_g, sg_ln_b, sg_w_spatial,
           sg_b_spatial, swa_w_in, swa_w_out, swa_sinks, rwkv_w_in, rwkv_w_out, rwkv_mu, rwkv_w0, rwkv_w_lora,
           rwkv_a0, rwkv_a_lora, rwkv_k_k, rwkv_k_a, rwkv_r_k, rwkv_gn_g, rwkv_gn_b):
    batch, seq, d = x.shape
    assert (seq, d) == (SEQ, D) and norm_g.shape[0] == DEPTH
    n_tok = batch * seq
    x2 = x.reshape(n_tok, D)
    c8 = jnp.zeros((8, D), F32).at[:batch].set(c)
    pos2 = positions.reshape(n_tok, 1)
    half = HEAD // 2
    inv_freq = ROPE_THETA ** (-jnp.arange(half, dtype=F32) / half)
    invf = jnp.tile(inv_freq, LANES // half).reshape(1, LANES)
    mod, cos_t, sin_t = _modulation(c8, mod_w, mod_b, pos2, invf)
    row = lambda a: a.reshape(1, -1)

    def act_rows(spans):
        rows = [jnp.concatenate([jnp.full((w,), val[t], F32) for w, *val in spans]) for t in range(3)]
        return jnp.zeros((8, rows[0].shape[0]), F32).at[:3].set(jnp.stack(rows))

    gelu_span = (GELU_A, GELU_B, 1.0)
    silu_span = (0.5, 0.0, 1.0)
    sg_in_bf, swa_in_bf, rwkv_in_bf = sg_w_in.astype(BF16), swa_w_in.astype(BF16), rwkv_w_in.astype(BF16)
    w_out = (sg_w_out, swa_w_out, rwkv_w_out)
    for i in range(DEPTH):
        kind, j = i % 3, i // 3
        mod3 = mod[i, :batch].reshape(batch, 1, 3 * D)
        g = row(norm_g[i])
        if kind == 0:
            act = act_rows([(2 * D, *gelu_span), (D, *silu_span)])
            p = _in_proj_act(x2, g, mod3, sg_in_bf, j, 3 * D, act, False, BF16)
            mix = _sg_mix(p, row(sg_ln_g[j]), row(sg_ln_b[j]), sg_w_spatial[j],
                          jnp.broadcast_to(sg_b_spatial[j][:, :, None], (SG_GROUPS, SG_CHUNK, LANES)))
        elif kind == 1:
            act = act_rows([(D + 2 * SWA_KVW, 0.0, 0.0, 0.0), (D, *silu_span)])
            p = _in_proj_act(x2, g, mod3, swa_in_bf, j, 2 * D + 2 * SWA_KVW, act, True, F32)
            mix = _swa_mix(p, cos_t, sin_t, swa_sinks[j])
        else:
            lp = RW_LP // 2 - RW_LORA
            w_in = rwkv_in_bf[j]
            w_l = jnp.concatenate([jnp.pad(w_in[:, 4 * D:4 * D + RW_LORA], ((0, 0), (0, lp))),
                                   jnp.pad(w_in[:, 4 * D + RW_LORA:], ((0, 0), (0, lp)))], axis=1)
            p, p_lora = _in_proj(x2, g, mod3, rwkv_in_bf, j, 4 * D, w_l)
            mu = rwkv_mu[j]
            mu_l = jnp.concatenate([jnp.pad(mu[4 * D:4 * D + RW_LORA], (0, lp)), jnp.pad(mu[4 * D + RW_LORA:], (0, lp))])
            wl_pad = jnp.pad(rwkv_w_lora[j], ((0, lp), (0, 0))).astype(BF16)
            al_pad = jnp.pad(rwkv_a_lora[j], ((0, lp), (0, 0))).astype(BF16)
            mix = _rwkv_mix(p, p_lora, row(mu[:4 * D]), row(mu_l), row(rwkv_w0[j]),
                            row(rwkv_a0[j]), wl_pad, al_pad, row(rwkv_k_k[j]),
                            row(rwkv_k_a[j]), row(rwkv_r_k[j]), row(rwkv_gn_g[j]), row(rwkv_gn_b[j]))
        x2 = _out_proj(mix, w_out[kind], j, x2, mod3, row(final_norm_g) if i == DEPTH - 1 else None)
    return x2.reshape(batch, seq, D)
```

```python
import functools
import math

import jax
import jax.numpy as jnp
from jax import lax
from jax.experimental import pallas as pl
from jax.experimental.pallas import tpu as pltpu

F32 = jnp.float32
BF16 = jnp.bfloat16

D = 2048
SEQ = 4096
DEPTH = 4
HEAD = 64
LANES = 128
SG_CHUNK = 128
SG_GROUPS = 16
SWA_HEADS = D // HEAD
SWA_KV = SWA_HEADS // 8
SWA_KVW = SWA_KV * HEAD
SWA_BLOCK = 128
SWA_STAGE = 16
ROPE_THETA = 10000.0
RW_HEADS = D // HEAD
RW_LORA = 96
RW_LP = 256
RW_CHUNK = 64
RW_HG = 4
RW_GW = RW_HG * HEAD
RW_GROUPS = D // RW_GW
RW_STAGE = 8
DECAY_SCALE = math.exp(-0.5)
GN_EPS = 64e-5
RMS_EPS = 1e-6
LN_EPS = 1e-5

VMEM_LIMIT = 48 * 1024 * 1024


def _cparams(*sem, vmem=VMEM_LIMIT):
    return pltpu.CompilerParams(dimension_semantics=sem, vmem_limit_bytes=vmem)


def _mod_kernel(c_ref, w_ref, b_ref, pos_ref, invf_ref, o_ref, cos_ref, sin_ref):
    cond = jax.nn.silu(c_ref[...]).astype(BF16)
    o_ref[0] = jnp.dot(cond, w_ref[0].astype(BF16), preferred_element_type=F32) + b_ref[0]
    _rope_table_kernel(pos_ref, invf_ref, cos_ref, sin_ref)


def _modulation(c8, mod_w, mod_b, pos2, invf):
    tn = 768
    n = mod_w.shape[2]
    nj = n // tn
    n_tok = pos2.shape[0]
    rows = n_tok // (DEPTH * nj)
    assert n % tn == 0 and rows * DEPTH * nj == n_tok and rows % 8 == 0
    table = pl.BlockSpec((rows, LANES), lambda l, j: (l * nj + j, 0))
    return pl.pallas_call(
        _mod_kernel,
        out_shape=(jax.ShapeDtypeStruct((DEPTH, 8, n), F32),
                   jax.ShapeDtypeStruct((n_tok, LANES), F32), jax.ShapeDtypeStruct((n_tok, LANES), F32)),
        grid=(DEPTH, nj),
        in_specs=[pl.BlockSpec((8, D), lambda l, j: (0, 0)),
                  pl.BlockSpec((1, D, tn), lambda l, j: (l, 0, j)),
                  pl.BlockSpec((1, 1, tn), lambda l, j: (l, 0, j)),
                  pl.BlockSpec((rows, 1), lambda l, j: (l * nj + j, 0)),
                  pl.BlockSpec((1, LANES), lambda l, j: (0, 0))],
        out_specs=(pl.BlockSpec((1, 8, tn), lambda l, j: (l, 0, j)), table, table),
        compiler_params=_cparams("parallel", "parallel"),
        name="modulation",
    )(c8, mod_w, mod_b.reshape(DEPTH, 1, n), pos2, invf)


NORM_ROWS = 128


def _row_rsqrt(x_ref, rs_scr):
    def body(i, carry):
        rows = pl.ds(pl.multiple_of(i * NORM_ROWS, NORM_ROWS), NORM_ROWS)
        x = x_ref[rows, :]
        rs_scr[rows, :] = lax.rsqrt(jnp.mean(x * x, axis=-1, keepdims=True) + RMS_EPS)
        return carry
    lax.fori_loop(0, x_ref.shape[0] // NORM_ROWS, body, 0)


def _modulated(x_ref, rs_scr, g_ref, sc_ref, sh_ref):
    gain = g_ref[...] * (1.0 + sc_ref[0])
    return (x_ref[...] * rs_scr[...] * gain + sh_ref[0]).astype(BF16)


GELU_A = math.sqrt(2.0 / math.pi)
GELU_B = GELU_A * 0.044715
ACT_COLS = 256


def _in_proj_act_kernel(x_ref, g_ref, sc_ref, sh_ref, w_ref, act_ref, o_ref, rs_scr, *, select):
    @pl.when(pl.program_id(1) == 0)
    def _():
        _row_rsqrt(x_ref, rs_scr)

    h = _modulated(x_ref, rs_scr, g_ref, sc_ref, sh_ref)
    for n in range(o_ref.shape[1] // ACT_COLS):
        cols = pl.ds(n * ACT_COLS, ACT_COLS)
        p = jnp.dot(h, w_ref[:, cols], preferred_element_type=F32)
        gated = (0.5 * p) * (1.0 + jnp.tanh(p * (act_ref[0:1, cols] + act_ref[1:2, cols] * (p * p))))
        o_ref[:, cols] = (jnp.where(act_ref[2:3, cols] > 0.0, gated, p) if select else gated).astype(o_ref.dtype)


def _in_proj_lora_kernel(x_ref, g_ref, sc_ref, sh_ref, w_ref, wl_ref, o_ref, ol_ref, rs_scr):
    @pl.when(pl.program_id(1) == 0)
    def _():
        _row_rsqrt(x_ref, rs_scr)
        ol_ref[...] = jnp.dot(_modulated(x_ref, rs_scr, g_ref, sc_ref, sh_ref), wl_ref[...],
                              preferred_element_type=F32)

    h = _modulated(x_ref, rs_scr, g_ref, sc_ref, sh_ref)
    o_ref[...] = jnp.dot(h, w_ref[...], preferred_element_type=F32)


def _in_proj_act(x2, g, mod3, w_bf, wl, ncols, act, select, out_dtype, *, tm=1024):
    n_tok = x2.shape[0]
    tn = 2048 if (out_dtype == BF16 and ncols % 2048 == 0) else (1024 if ncols % 1024 == 0 else 1536)
    assert ncols % tn == 0
    per_b = SEQ // tm
    return pl.pallas_call(
        functools.partial(_in_proj_act_kernel, select=select),
        out_shape=jax.ShapeDtypeStruct((n_tok, ncols), out_dtype),
        grid=(n_tok // tm, ncols // tn),
        in_specs=[pl.BlockSpec((tm, D), lambda i, j: (i, 0)),
                  pl.BlockSpec((1, D), lambda i, j: (0, 0)),
                  pl.BlockSpec((1, 1, D), lambda i, j: (i // per_b, 0, 1)),
                  pl.BlockSpec((1, 1, D), lambda i, j: (i // per_b, 0, 0)),
                  pl.BlockSpec((None, D, tn), lambda i, j: (wl, 0, j)),
                  pl.BlockSpec((8, tn), lambda i, j: (0, j))],
        out_specs=pl.BlockSpec((tm, tn), lambda i, j: (i, j)),
        scratch_shapes=[pltpu.VMEM((tm, 1), F32)],
        compiler_params=_cparams("parallel", "arbitrary"),
        name="in_proj_act",
    )(x2, g, mod3, mod3, w_bf, act)


def _in_proj(x2, g, mod3, w_bf, wl, ncols, w_lora_bf, *, tm=1024):
    n_tok = x2.shape[0]
    tn = 2048
    assert ncols % tn == 0
    per_b = SEQ // tm
    x_spec = pl.BlockSpec((tm, D), lambda i, j: (i, 0))
    g_spec = pl.BlockSpec((1, D), lambda i, j: (0, 0))
    sh_spec = pl.BlockSpec((1, 1, D), lambda i, j: (i // per_b, 0, 0))
    sc_spec = pl.BlockSpec((1, 1, D), lambda i, j: (i // per_b, 0, 1))
    w_spec = pl.BlockSpec((None, D, tn), lambda i, j: (wl, 0, j))
    o_spec = pl.BlockSpec((tm, tn), lambda i, j: (i, j))
    grid = (n_tok // tm, ncols // tn)
    scratch = [pltpu.VMEM((tm, 1), F32)]
    nl = w_lora_bf.shape[1]
    vmem = 2 * (4 * tm * D + 2 * D * tn + 4 * tm * tn + 2 * D * nl + 4 * tm * nl) + 4 * 1024 * 1024
    return pl.pallas_call(
        _in_proj_lora_kernel,
        out_shape=(jax.ShapeDtypeStruct((n_tok, ncols), F32), jax.ShapeDtypeStruct((n_tok, nl), F32)),
        grid=grid,
        in_specs=[x_spec, g_spec, sc_spec, sh_spec, w_spec, pl.BlockSpec((D, nl), lambda i, j: (0, 0))],
        out_specs=(o_spec, pl.BlockSpec((tm, nl), lambda i, j: (i, 0))),
        scratch_shapes=scratch,
        compiler_params=_cparams("parallel", "arbitrary", vmem=vmem),
        name="in_proj_lora",
    )(x2, g, mod3, mod3, w_bf, w_lora_bf)


CAST_ROWS = 256


def _round_weight(w_ref, w_scr):
    @pl.when(pl.program_id(0) == 0)
    def _():
        def body(i, carry):
            rows = pl.ds(pl.multiple_of(i * CAST_ROWS, CAST_ROWS), CAST_ROWS)
            w_scr[rows, :] = w_ref[rows, :].astype(BF16)
            return carry
        lax.fori_loop(0, w_ref.shape[0] // CAST_ROWS, body, 0)


def _out_proj_kernel(m_ref, w_ref, x_ref, gate_ref, o_ref, w_scr):
    _round_weight(w_ref, w_scr)
    y = jnp.dot(m_ref[...], w_scr[...], preferred_element_type=F32)
    o_ref[...] = x_ref[...] + gate_ref[0] * y


def _out_proj_final_kernel(m_ref, w_ref, x_ref, gate_ref, fg_ref, o_ref, w_scr):
    _round_weight(w_ref, w_scr)
    y = jnp.dot(m_ref[...], w_scr[...], preferred_element_type=F32)
    xn = x_ref[...] + gate_ref[0] * y
    o_ref[...] = xn * lax.rsqrt(jnp.mean(xn * xn, axis=-1, keepdims=True) + RMS_EPS) * fg_ref[...]


def _out_proj(mix, w, wl, x2, mod3, final_g=None, *, tm=512):
    n_tok = x2.shape[0]
    per_b = SEQ // tm
    in_specs = [pl.BlockSpec((tm, D), lambda i: (i, 0)),
                pl.BlockSpec((None, D, D), lambda i: (wl, 0, 0), pipeline_mode=pl.Buffered(1)),
                pl.BlockSpec((tm, D), lambda i: (i, 0)),
                pl.BlockSpec((1, 1, D), lambda i: (i // per_b, 0, 2))]
    args = [mix, w, x2, mod3]
    kern = _out_proj_kernel
    if final_g is not None:
        in_specs.append(pl.BlockSpec((1, D), lambda i: (0, 0)))
        args.append(final_g)
        kern = _out_proj_final_kernel
    return pl.pallas_call(
        kern,
        out_shape=jax.ShapeDtypeStruct((n_tok, D), F32),
        grid=(n_tok // tm,),
        in_specs=in_specs,
        out_specs=pl.BlockSpec((tm, D), lambda i: (i, 0)),
        scratch_shapes=[pltpu.VMEM((D, D), BF16)],
        compiler_params=_cparams("arbitrary", vmem=6 * D * D + 2 * (2 + 4 + 4) * tm * D + 8 * 1024 * 1024),
        name="out_proj_final" if final_g is not None else "out_proj",
    )(*args)


def _sg_mix_kernel(u_ref, v_ref, z_ref, lng_ref, lnb_ref, ws_ref, bs_ref, o_ref, *, tm):
    ti = lax.broadcasted_iota(jnp.int32, (SG_CHUNK, SG_CHUNK), 0)
    si = lax.broadcasted_iota(jnp.int32, (SG_CHUNK, SG_CHUNK), 1)
    causal = ti >= si
    ws = [jnp.where(causal, ws_ref[g], 0.0).astype(BF16) for g in range(SG_GROUPS)]
    for c in range(tm // SG_CHUNK):
        rows = pl.ds(c * SG_CHUNK, SG_CHUNK)
        v = v_ref[rows, :].astype(F32)
        mean = jnp.mean(v, axis=-1, keepdims=True)
        d = v - mean
        var = jnp.mean(d * d, axis=-1, keepdims=True)
        vn = (d * lax.rsqrt(var + LN_EPS) * lng_ref[...] + lnb_ref[...]).astype(BF16)
        f = [jnp.dot(ws[g], vn[:, g * LANES:(g + 1) * LANES], preferred_element_type=F32) for g in range(SG_GROUPS)]
        for g in range(SG_GROUPS):
            cols = pl.ds(g * LANES, LANES)
            uz = u_ref[rows, cols].astype(F32) * z_ref[rows, cols].astype(F32)
            o_ref[rows, cols] = (uz * (f[g] + bs_ref[g])).astype(BF16)


def _sg_mix(p, ln_g, ln_b, w_s, b_full, *, tm=512):
    n_tok = p.shape[0]
    return pl.pallas_call(
        functools.partial(_sg_mix_kernel, tm=tm),
        out_shape=jax.ShapeDtypeStruct((n_tok, D), BF16),
        grid=(n_tok // tm,),
        in_specs=[pl.BlockSpec((tm, D), lambda i: (i, 0)),
                  pl.BlockSpec((tm, D), lambda i: (i, 1)),
                  pl.BlockSpec((tm, D), lambda i: (i, 2)),
                  pl.BlockSpec((1, D), lambda i: (0, 0)),
                  pl.BlockSpec((1, D), lambda i: (0, 0)),
                  pl.BlockSpec((SG_GROUPS, SG_CHUNK, SG_CHUNK), lambda i: (0, 0, 0)),
                  pl.BlockSpec((SG_GROUPS, SG_CHUNK, LANES), lambda i: (0, 0, 0))],
        out_specs=pl.BlockSpec((tm, D), lambda i: (i, 0)),
        compiler_params=_cparams("parallel"),
        name="sg_mix",
    )(p, p, p, ln_g, ln_b, w_s, b_full)


def _first_half(shape):
    return (lax.broadcasted_iota(jnp.int32, shape, 1) % HEAD) < (HEAD // 2)


def _rope_table_kernel(pos_ref, invf_ref, cos_ref, sin_ref):
    ang = pos_ref[...].astype(F32) * invf_ref[...]
    cos_ref[...] = jnp.cos(ang)
    sin_ref[...] = jnp.where(_first_half(ang.shape), -jnp.sin(ang), jnp.sin(ang))


def _rope(x, cos, sin_signed, first):
    partner = jnp.where(first, pltpu.roll(x, LANES - HEAD // 2, 1), pltpu.roll(x, HEAD // 2, 1))
    return x * cos + partner * sin_signed


def _swa_mix_kernel(sink_ref, p_ref, kvp_ref, cos_ref, sin_ref, cosp_ref, sinp_ref, o_ref):
    i = pl.program_id(0)
    first_key = jnp.where((i % (SEQ // SWA_BLOCK)) != 0, 0, SWA_BLOCK)
    cos, sin_s, cosp, sinp_s = cos_ref[...], sin_ref[...], cosp_ref[...], sinp_ref[...]
    first = _first_half(cos.shape)
    lane = lax.broadcasted_iota(jnp.int32, (2 * SWA_BLOCK, LANES), 1)
    low = lane < HEAD

    qi = lax.broadcasted_iota(jnp.int32, (SWA_BLOCK, 2 * SWA_BLOCK), 0)
    kj = lax.broadcasted_iota(jnp.int32, (SWA_BLOCK, 2 * SWA_BLOCK), 1)
    rel = qi + SWA_BLOCK - kj
    mask = (rel >= 0) & (rel < SWA_BLOCK) & (kj >= first_key)

    k_lo, k_hi, v_lo, v_hi = [], [], [], []
    for kp in range(SWA_KV // 2):
        ksl = pl.ds(kp * LANES, LANES)
        vsl = pl.ds(SWA_KVW + kp * LANES, LANES)
        k2 = jnp.concatenate([_rope(kvp_ref[:, ksl], cosp, sinp_s, first),
                              _rope(p_ref[:, pl.ds(D + kp * LANES, LANES)], cos, sin_s, first)], axis=0)
        v2 = jnp.concatenate([kvp_ref[:, vsl], p_ref[:, pl.ds(D + SWA_KVW + kp * LANES, LANES)]], axis=0)
        k2r = pltpu.roll(k2, HEAD, 1)
        v2r = pltpu.roll(v2, HEAD, 1)
        for sub in range(2):
            src_k, src_kr = (k2, k2r) if sub == 0 else (k2r, k2)
            src_v, src_vr = (v2, v2r) if sub == 0 else (v2r, v2)
            k_lo.append(jnp.where(low, src_k, 0.0).astype(BF16))
            k_hi.append(jnp.where(low, 0.0, src_kr).astype(BF16))
            v_lo.append(jnp.where(low, src_v, 0.0).astype(BF16))
            v_hi.append(jnp.where(low, 0.0, src_vr).astype(BF16))

    zoff = D + 2 * SWA_KVW
    for h0 in range(0, SWA_HEADS, SWA_STAGE):
        hs = range(h0, h0 + SWA_STAGE)
        prs = range(h0 // 2, (h0 + SWA_STAGE) // 2)
        cols = {pr: pl.ds(pr * LANES, LANES) for pr in prs}
        q2 = {pr: (_rope(p_ref[:, cols[pr]], cos, sin_s, first) * (HEAD ** -0.5)).astype(BF16) for pr in prs}
        s = {h: _dot_nt(q2[h // 2], (k_hi if h % 2 else k_lo)[h // 8]) for h in hs}
        s = {h: jnp.where(mask, s[h], -jnp.inf) for h in hs}
        m = {h: jnp.maximum(jnp.max(s[h], axis=-1, keepdims=True), sink_ref[h]) for h in hs}
        e = {h: jnp.exp(s[h] - m[h]) for h in hs}
        denom = {h: jnp.sum(e[h], axis=-1, keepdims=True) + jnp.exp(sink_ref[h] - m[h]) for h in hs}
        prob = {h: (e[h] * (1.0 / denom[h])).astype(BF16) for h in hs}
        o = {h: jnp.dot(prob[h], (v_hi if h % 2 else v_lo)[h // 8], preferred_element_type=F32) for h in hs}
        for pr in prs:
            gate = p_ref[:, pl.ds(zoff + pr * LANES, LANES)]
            o_ref[:, cols[pr]] = ((o[2 * pr] + o[2 * pr + 1]) * gate).astype(BF16)


def _swa_mix(p, cos_t, sin_t, sinks):
    n_tok, ncols = p.shape
    tb = SWA_BLOCK
    kvb = D // (2 * SWA_KVW)
    cur = pl.BlockSpec((tb, LANES), lambda i: (i, 0))
    prev = pl.BlockSpec((tb, LANES), lambda i: (jnp.maximum(i - 1, 0), 0))
    return pl.pallas_call(
        _swa_mix_kernel,
        out_shape=jax.ShapeDtypeStruct((n_tok, D), BF16),
        grid=(n_tok // tb,),
        in_specs=[pl.BlockSpec(memory_space=pltpu.SMEM),
                  pl.BlockSpec((tb, ncols), lambda i: (i, 0)),
                  pl.BlockSpec((tb, 2 * SWA_KVW), lambda i: (jnp.maximum(i - 1, 0), kvb)),
                  cur, cur, prev, prev],
        out_specs=pl.BlockSpec((tb, D), lambda i: (i, 0)),
        compiler_params=_cparams("parallel"),
        name="swa_mix",
    )(sinks, p, p, cos_t, sin_t, cos_t, sin_t)


def _head_sums(xs, seg):
    out = jnp.dot(jnp.concatenate([x.astype(BF16) for x in xs], axis=0), seg, preferred_element_type=F32)
    n = xs[0].shape[0]
    return [out[i * n:(i + 1) * n] for i in range(len(xs))]


def _cumsum_rows(x, tri):
    hi = x.astype(BF16)
    lo = (x - hi.astype(F32)).astype(BF16)
    return jnp.dot(tri, hi, preferred_element_type=F32) + jnp.dot(tri, lo, preferred_element_type=F32)


def _bd(x, same_head):
    t = jnp.concatenate([x] * RW_HG, axis=0)
    return jnp.where(same_head, t, 0.0).astype(BF16)


def _diag_blocks(o, lane_head):
    acc = o[0:HEAD]
    for h in range(1, RW_HG):
        acc = jnp.where(lane_head == h, o[h * HEAD:(h + 1) * HEAD], acc)
    return acc


def _dot(a, b):
    return jnp.dot(a.astype(BF16), b.astype(BF16), preferred_element_type=F32)


def _dot_nt(a, b):
    return lax.dot_general(a, b, (((1,), (1,)), ((), ())), preferred_element_type=F32)


def _dot_tn(a, b):
    return lax.dot_general(a, b, (((0,), (0,)), ((), ())), preferred_element_type=F32)


def _rwkv_groups(r, lw, cum, k, v, kap, bet, h0, masks):
    incl, strict, eye, same_head, lane_head = masks
    c = RW_CHUNK
    gr = range(len(r))
    bd = lambda t: _bd(t, same_head)
    g_t = [jnp.exp(cum[g]) for g in gr]
    g_inv = [jnp.exp(-cum[g]) for g in gr]
    cref = [cum[g][c // 2 - 1:c // 2] for g in gr]
    cend = [cum[g][c - 1:c] for g in gr]
    to_ref = [jnp.exp(-cref[g]) for g in gr]
    r_true = [r[g] * g_t[g] for g in gr]
    kap_true = [kap[g] * (g_t[g] * jnp.exp(-lw[g])) for g in gr]
    r_s = [r_true[g] * to_ref[g] for g in gr]
    kap_s = [kap_true[g] * to_ref[g] for g in gr]
    ginv = [g_inv[g] * jnp.exp(cref[g]) for g in gr]
    eend = [g_inv[g] * jnp.exp(cend[g]) for g in gr]

    s = [_dot_nt(jnp.concatenate([r_s[g], kap_s[g]], axis=0).astype(BF16),
                 jnp.concatenate([bd(k[g] * ginv[g]), bd(bet[g] * ginv[g])], axis=0)) for g in gr]
    a_rk = [jnp.where(incl, s[g][:c, :RW_GW], 0.0) for g in gr]
    a_rb = [jnp.where(incl, s[g][:c, RW_GW:], 0.0) for g in gr]
    a_kk = [jnp.where(strict, s[g][c:, :RW_GW], 0.0) for g in gr]
    a_kb = [jnp.where(strict, s[g][c:, RW_GW:], 0.0) for g in gr]

    x = [eye - a_kb[g] for g in gr]
    p = [_dot(a_kb[g], bd(a_kb[g])) for g in gr]
    akv = [_dot(a_kk[g], bd(v[g])) for g in gr]
    for _ in range(4):
        xp = [_dot(jnp.concatenate([x[g], p[g]], axis=0), bd(p[g])) for g in gr]
        x = [x[g] + xp[g][:c] for g in gr]
        p = [xp[g][c:] for g in gr]
    t_inv = [x[g] + _dot(x[g], bd(p[g])) for g in gr]

    wu = [-_dot(t_inv[g], jnp.concatenate([bd(kap_true[g]), bd(akv[g])], axis=1)) for g in gr]

    hw = [_dot(jnp.concatenate([wu[g][:, :RW_GW], r_true[g], eye * jnp.exp(cend[g])], axis=0), bd(h0[g]))
          for g in gr]
    u = [hw[g][:c] + wu[g][:, RW_GW:] for g in gr]
    y = [hw[g][c:2 * c] + _dot(jnp.concatenate([a_rk[g], a_rb[g]], axis=1),
                               jnp.concatenate([bd(v[g]), bd(u[g])], axis=0)) for g in gr]
    o2 = [_dot_tn(jnp.concatenate([k[g] * eend[g], bet[g] * eend[g]], axis=0).astype(BF16),
                  jnp.concatenate([v[g], u[g]], axis=0).astype(BF16)) for g in gr]
    h_new = [hw[g][2 * c:] + _diag_blocks(o2[g], lane_head) for g in gr]
    return y, h_new


def _rwkv_mix_kernel(p_ref, pp_ref, pl_ref, plp_ref, mu_ref, mul_ref, w0_ref, a0_ref, wl_ref, al_ref,
                     kk_ref, ka_ref, rk_ref, gng_ref, gnb_ref, o_ref, h_scr):
    ci = pl.program_id(1)
    c = RW_CHUNK

    @pl.when(ci == 0)
    def _():
        h_scr[...] = jnp.zeros_like(h_scr)

    not_first = (ci != 0).astype(F32)
    row = lax.broadcasted_iota(jnp.int32, (c, 1), 0)

    def shifted(cur, prev_ref):
        prev_row = prev_ref[7:8, :] * not_first
        rolled = pltpu.roll(cur, 1, 0)
        return jnp.where(row == 0, prev_row, rolled)

    lcur = pl_ref[...]
    lcur = lcur + (shifted(lcur, plp_ref) - lcur) * mul_ref[...]
    dw = jnp.tanh(lcur[:, :RW_LP // 2])
    da = lcur[:, RW_LP // 2:]

    ti = lax.broadcasted_iota(jnp.int32, (c, RW_GW), 0)
    li = lax.broadcasted_iota(jnp.int32, (c, RW_GW), 1)
    si = li % HEAD
    lane_head = li // HEAD
    rb = lax.broadcasted_iota(jnp.int32, (RW_GW, RW_GW), 0) // HEAD
    cb = lax.broadcasted_iota(jnp.int32, (RW_GW, RW_GW), 1) // HEAD
    same_head = rb == cb
    seg = same_head.astype(BF16)
    masks = (ti >= si, ti > si, (ti == si).astype(F32), same_head, lane_head)

    gr = range(RW_GROUPS)
    gsl = [pl.ds(g * RW_GW, RW_GW) for g in gr]

    def lerp(off, g):
        cols = pl.ds(off + g * RW_GW, RW_GW)
        cur = p_ref[:, cols]
        return cur + (shifted(cur, pp_ref.at[:, cols]) - cur) * mu_ref[:, cols]

    r = [lerp(0, g) for g in gr]
    k = [lerp(D, g) for g in gr]
    v = [lerp(2 * D, g) for g in gr]
    lw = [-DECAY_SCALE * jax.nn.sigmoid(w0_ref[:, gsl[g]] + _dot(dw, wl_ref[:, gsl[g]])) for g in gr]
    a = [jax.nn.sigmoid(a0_ref[:, gsl[g]] + _dot(da, al_ref[:, gsl[g]])) for g in gr]
    kap = [k[g] * kk_ref[:, gsl[g]] for g in gr]
    k = [k[g] * (1.0 + (a[g] - 1.0) * ka_ref[:, gsl[g]]) for g in gr]
    sums = _head_sums([kap[g] * kap[g] for g in gr] + [r[g] * k[g] * rk_ref[:, gsl[g]] for g in gr], seg)
    kap = [kap[g] / jnp.maximum(jnp.sqrt(sums[g]), 1e-12) for g in gr]
    bonus = sums[RW_GROUPS:]
    bet = [kap[g] * a[g] for g in gr]
    tri = (lax.broadcasted_iota(jnp.int32, (c, c), 0) >= lax.broadcasted_iota(jnp.int32, (c, c), 1)).astype(BF16)
    cum = [_cumsum_rows(lw[g], tri) for g in gr]

    y, h_new = [], []
    for lo in range(0, RW_GROUPS, RW_STAGE):
        sl = slice(lo, lo + RW_STAGE)
        ys, hs = _rwkv_groups(r[sl], lw[sl], cum[sl], k[sl], v[sl], kap[sl], bet[sl],
                              [h_scr[g] for g in range(lo, lo + RW_STAGE)], masks)
        y += ys
        h_new += hs
    for g in gr:
        h_scr[g] = h_new[g]

    ysum = _head_sums(y, seg)
    dlt = [y[g] - ysum[g] * (1.0 / HEAD) for g in gr]
    dsum = _head_sums([dlt[g] * dlt[g] for g in gr], seg)
    for g in gr:
        rstd = lax.rsqrt(dsum[g] * (1.0 / HEAD) + GN_EPS)
        yn = dlt[g] * rstd * gng_ref[:, gsl[g]] + gnb_ref[:, gsl[g]] + bonus[g] * v[g]
        o_ref[:, gsl[g]] = (yn * jax.nn.silu(lerp(3 * D, g))).astype(BF16)


def _rwkv_mix(p, p_lora, mu, mu_l, w0, a0, w_lora, a_lora, k_k, k_a, r_k, gn_g, gn_b):
    n_tok = p.shape[0]
    c = RW_CHUNK
    nc = SEQ // c
    ncol = p.shape[1]
    nl = p_lora.shape[1]
    row = lambda b, ci: (b * nc + ci, 0)
    prev = lambda b, ci: (jnp.maximum((b * nc + ci) * (c // 8) - 1, 0), 0)
    vec = pl.BlockSpec((1, D), lambda b, ci: (0, 0))
    lora = pl.BlockSpec((RW_LP // 2, D), lambda b, ci: (0, 0))
    return pl.pallas_call(
        _rwkv_mix_kernel,
        out_shape=jax.ShapeDtypeStruct((n_tok, D), BF16),
        grid=(n_tok // SEQ, nc),
        in_specs=[pl.BlockSpec((c, ncol), row),
                  pl.BlockSpec((8, ncol), prev),
                  pl.BlockSpec((c, nl), row),
                  pl.BlockSpec((8, nl), prev),
                  pl.BlockSpec((1, ncol), lambda b, ci: (0, 0)),
                  pl.BlockSpec((1, nl), lambda b, ci: (0, 0)),
                  vec, vec, lora, lora,
                  vec, vec, vec, vec, vec],
        out_specs=pl.BlockSpec((c, D), row),
        scratch_shapes=[pltpu.VMEM((RW_GROUPS, HEAD, RW_GW), F32)],
        compiler_params=_cparams("parallel", "arbitrary"),
        name="rwkv_mix",
    )(p, p, p_lora, p_lora, mu, mu_l, w0, a0, w_lora, a_lora, k_k, k_a, r_k, gn_g, gn_b)


def kernel(x, c, positions, norm_g, mod_w, mod_b, final_norm_g, sg_w_in, sg_w_out, sg_ln_g, sg_ln_b, sg_w_spatial,
           sg_b_spatial, swa_w_in, swa_w_out, swa_sinks, rwkv_w_in, rwkv_w_out, rwkv_mu, rwkv_w0, rwkv_w_lora,
           rwkv_a0, rwkv_a_lora, rwkv_k_k, rwkv_k_a, rwkv_r_k, rwkv_gn_g, rwkv_gn_b):
    batch, seq, d = x.shape
    assert (seq, d) == (SEQ, D) and norm_g.shape[0] == DEPTH
    n_tok = batch * seq
    x2 = x.reshape(n_tok, D)
    c8 = jnp.zeros((8, D), F32).at[:batch].set(c)
    pos2 = positions.reshape(n_tok, 1)
    half = HEAD // 2
    inv_freq = ROPE_THETA ** (-jnp.arange(half, dtype=F32) / half)
    invf = jnp.tile(inv_freq, LANES // half).reshape(1, LANES)
    mod, cos_t, sin_t = _modulation(c8, mod_w, mod_b, pos2, invf)
    row = lambda a: a.reshape(1, -1)

    def act_rows(spans):
        rows = [jnp.concatenate([jnp.full((w,), val[t], F32) for w, *val in spans]) for t in range(3)]
        return jnp.zeros((8, rows[0].shape[0]), F32).at[:3].set(jnp.stack(rows))

    gelu_span = (GELU_A, GELU_B, 1.0)
    silu_span = (0.5, 0.0, 1.0)
    sg_in_bf, swa_in_bf, rwkv_in_bf = sg_w_in.astype(BF16), swa_w_in.astype(BF16), rwkv_w_in.astype(BF16)
    w_out = (sg_w_out, swa_w_out, rwkv_w_out)
    for i in range(DEPTH):
        kind, j = i % 3, i // 3
        mod3 = mod[i, :batch].reshape(batch, 1, 3 * D)
        g = row(norm_g[i])
        if kind == 0:
            act = act_rows([(2 * D, *gelu_span), (D, *silu_span)])
            p = _in_proj_act(x2, g, mod3, sg_in_bf, j, 3 * D, act, False, BF16)
            mix = _sg_mix(p, row(sg_ln_g[j]), row(sg_ln_b[j]), sg_w_spatial[j],
                          jnp.broadcast_to(sg_b_spatial[j][:, :, None], (SG_GROUPS, SG_CHUNK, LANES)))
        elif kind == 1:
            act = act_rows([(D + 2 * SWA_KVW, 0.0, 0.0, 0.0), (D, *silu_span)])
            p = _in_proj_act(x2, g, mod3, swa_in_bf, j, 2 * D + 2 * SWA_KVW, act, True, F32)
            mix = _swa_mix(p, cos_t, sin_t, swa_sinks[j])
        else:
            lp = RW_LP // 2 - RW_LORA
            w_in = rwkv_in_bf[j]
            w_l = jnp.concatenate([jnp.pad(w_in[:, 4 * D:4 * D + RW_LORA], ((0, 0), (0, lp))),
                                   jnp.pad(w_in[:, 4 * D + RW_LORA:], ((0, 0), (0, lp)))], axis=1)
            p, p_lora = _in_proj(x2, g, mod3, rwkv_in_bf, j, 4 * D, w_l)
            mu = rwkv_mu[j]
            mu_l = jnp.concatenate([jnp.pad(mu[4 * D:4 * D + RW_LORA], (0, lp)), jnp.pad(mu[4 * D + RW_LORA:], (0, lp))])
            wl_pad = jnp.pad(rwkv_w_lora[j], ((0, lp), (0, 0))).astype(BF16)
            al_pad = jnp.pad(rwkv_a_lora[j], ((0, lp), (0, 0))).astype(BF16)
            mix = _rwkv_mix(p, p_lora, row(mu[:4 * D]), row(mu_l), row(rwkv_w0[j]),
                            row(rwkv_a0[j]), wl_pad, al_pad, row(rwkv_k_k[j]),
                            row(rwkv_k_a[j]), row(rwkv_r_k[j]), row(rwkv_gn_g[j]), row(rwkv_gn_b[j]))
        x2 = _out_proj(mix, w_out[kind], j, x2, mod3, row(final_norm_g) if i == DEPTH - 1 else None)
    return x2.reshape(batch, seq, D)
```

```python
import functools
import math

import jax
import jax.numpy as jnp
from jax import lax
from jax.experimental import pallas as pl
from jax.experimental.pallas import tpu as pltpu

F32 = jnp.float32
BF16 = jnp.bfloat16

D = 2048
SEQ = 4096
DEPTH = 4
HEAD = 64
LANES = 128
SG_CHUNK = 128
SG_GROUPS = 16
SWA_HEADS = D // HEAD
SWA_KV = SWA_HEADS // 8
SWA_KVW = SWA_KV * HEAD
SWA_BLOCK = 128
SWA_STAGE = 16
ROPE_THETA = 10000.0
RW_HEADS = D // HEAD
RW_LORA = 96
RW_LP = 256
RW_CHUNK = 64
RW_HG = 4
RW_GW = RW_HG * HEAD
RW_GROUPS = D // RW_GW
RW_STAGE = 8
DECAY_SCALE = math.exp(-0.5)
GN_EPS = 64e-5
RMS_EPS = 1e-6
LN_EPS = 1e-5

VMEM_LIMIT = 48 * 1024 * 1024


def _cparams(*sem, vmem=VMEM_LIMIT):
    return pltpu.CompilerParams(dimension_semantics=sem, vmem_limit_bytes=vmem)


def _mod_kernel(c_ref, w_ref, b_ref, pos_ref, invf_ref, o_ref, cos_ref, sin_ref):
    cond = jax.nn.silu(c_ref[...]).astype(BF16)
    o_ref[0] = jnp.dot(cond, w_ref[0].astype(BF16), preferred_element_type=F32) + b_ref[0]
    _rope_table_kernel(pos_ref, invf_ref, cos_ref, sin_ref)


def _modulation(c8, mod_w, mod_b, pos2, invf):
    tn = 768
    n = mod_w.shape[2]
    nj = n // tn
    n_tok = pos2.shape[0]
    rows = n_tok // (DEPTH * nj)
    assert n % tn == 0 and rows * DEPTH * nj == n_tok and rows % 8 == 0
    table = pl.BlockSpec((rows, LANES), lambda l, j: (l * nj + j, 0))
    return pl.pallas_call(
        _mod_kernel,
        out_shape=(jax.ShapeDtypeStruct((DEPTH, 8, n), F32),
                   jax.ShapeDtypeStruct((n_tok, LANES), F32), jax.ShapeDtypeStruct((n_tok, LANES), F32)),
        grid=(DEPTH, nj),
        in_specs=[pl.BlockSpec((8, D), lambda l, j: (0, 0)),
                  pl.BlockSpec((1, D, tn), lambda l, j: (l, 0, j)),
                  pl.BlockSpec((1, 1, tn), lambda l, j: (l, 0, j)),
                  pl.BlockSpec((rows, 1), lambda l, j: (l * nj + j, 0)),
                  pl.BlockSpec((1, LANES), lambda l, j: (0, 0))],
        out_specs=(pl.BlockSpec((1, 8, tn), lambda l, j: (l, 0, j)), table, table),
        compiler_params=_cparams("parallel", "parallel"),
        name="modulation",
    )(c8, mod_w, mod_b.reshape(DEPTH, 1, n), pos2, invf)


NORM_ROWS = 128


def _row_rsqrt(x_ref, rs_scr):
    def body(i, carry):
        rows = pl.ds(pl.multiple_of(i * NORM_ROWS, NORM_ROWS), NORM_ROWS)
        x = x_ref[rows, :]
        rs_scr[rows, :] = lax.rsqrt(jnp.mean(x * x, axis=-1, keepdims=True) + RMS_EPS)
        return carry
    lax.fori_loop(0, x_ref.shape[0] // NORM_ROWS, body, 0)


def _modulated(x_ref, rs_scr, g_ref, sc_ref, sh_ref):
    gain = g_ref[...] * (1.0 + sc_ref[0])
    return (x_ref[...] * rs_scr[...] * gain + sh_ref[0]).astype(BF16)


GELU_A = math.sqrt(2.0 / math.pi)
GELU_B = GELU_A * 0.044715
ACT_COLS = 256


def _in_proj_act_kernel(x_ref, g_ref, sc_ref, sh_ref, w_ref, act_ref, *rest, select):
    if len(rest) == 4:
        side_ref, o_ref, side_out_ref, rs_scr = rest
        side_out_ref[...] = side_ref[...].astype(BF16)
    else:
        o_ref, rs_scr = rest

    @pl.when(pl.program_id(1) == 0)
    def _():
        _row_rsqrt(x_ref, rs_scr)

    h = _modulated(x_ref, rs_scr, g_ref, sc_ref, sh_ref)
    for n in range(o_ref.shape[1] // ACT_COLS):
        cols = pl.ds(n * ACT_COLS, ACT_COLS)
        p = jnp.dot(h, w_ref[:, cols], preferred_element_type=F32)
        gated = (0.5 * p) * (1.0 + jnp.tanh(p * (act_ref[0:1, cols] + act_ref[1:2, cols] * (p * p))))
        o_ref[:, cols] = (jnp.where(act_ref[2:3, cols] > 0.0, gated, p) if select else gated).astype(o_ref.dtype)


def _in_proj_lora_kernel(x_ref, g_ref, sc_ref, sh_ref, w_ref, wl_ref, o_ref, ol_ref, rs_scr):
    @pl.when(pl.program_id(1) == 0)
    def _():
        _row_rsqrt(x_ref, rs_scr)
        ol_ref[...] = jnp.dot(_modulated(x_ref, rs_scr, g_ref, sc_ref, sh_ref), wl_ref[...],
                              preferred_element_type=F32)

    h = _modulated(x_ref, rs_scr, g_ref, sc_ref, sh_ref)
    o_ref[...] = jnp.dot(h, w_ref[...], preferred_element_type=F32)


SIDE_ROWS = 128


def _in_proj_act(x2, g, mod3, w_bf, wl, ncols, act, select, out_dtype, side=None, *, tm=1024):
    n_tok = x2.shape[0]
    tn = 2048 if (out_dtype == BF16 and ncols % 2048 == 0) else (1024 if ncols % 1024 == 0 else 1536)
    assert ncols % tn == 0
    per_b = SEQ // tm
    nj = ncols // tn
    in_specs = [pl.BlockSpec((tm, D), lambda i, j: (i, 0)),
                pl.BlockSpec((1, D), lambda i, j: (0, 0)),
                pl.BlockSpec((1, 1, D), lambda i, j: (i // per_b, 0, 1)),
                pl.BlockSpec((1, 1, D), lambda i, j: (i // per_b, 0, 0)),
                pl.BlockSpec((None, D, tn), lambda i, j: (wl, 0, j)),
                pl.BlockSpec((8, tn), lambda i, j: (0, j))]
    out_shape = jax.ShapeDtypeStruct((n_tok, ncols), out_dtype)
    out_specs = pl.BlockSpec((tm, tn), lambda i, j: (i, j))
    args = [x2, g, mod3, mod3, w_bf, act]
    osize = jnp.dtype(out_dtype).itemsize
    vmem = 2 * (4 * tm * D + 2 * D * tn + osize * tm * tn) + 8 * 1024 * 1024
    if side is not None:
        side_w, sl = side
        _, srows, scols = side_w.shape
        nblk = srows // SIDE_ROWS
        assert srows % SIDE_ROWS == 0 and nblk <= (n_tok // tm) * nj
        blk = lambda i, j: jnp.minimum(i * nj + j, nblk - 1)
        in_specs.append(pl.BlockSpec((None, SIDE_ROWS, scols), lambda i, j: (sl, blk(i, j), 0)))
        args.append(side_w)
        out_shape = (out_shape, jax.ShapeDtypeStruct((srows, scols), BF16))
        out_specs = (out_specs, pl.BlockSpec((SIDE_ROWS, scols), lambda i, j: (blk(i, j), 0)))
        vmem += 2 * (4 + 2) * SIDE_ROWS * scols
    return pl.pallas_call(
        functools.partial(_in_proj_act_kernel, select=select),
        out_shape=out_shape,
        grid=(n_tok // tm, nj),
        in_specs=in_specs,
        out_specs=out_specs,
        scratch_shapes=[pltpu.VMEM((tm, 1), F32)],
        compiler_params=_cparams("arbitrary", "arbitrary", vmem=vmem),
        name="in_proj_act",
    )(*args)


def _in_proj(x2, g, mod3, w_bf, wl, ncols, w_lora_bf, *, tm=1024):
    n_tok = x2.shape[0]
    tn = 2048
    assert ncols % tn == 0
    per_b = SEQ // tm
    x_spec = pl.BlockSpec((tm, D), lambda i, j: (i, 0))
    g_spec = pl.BlockSpec((1, D), lambda i, j: (0, 0))
    sh_spec = pl.BlockSpec((1, 1, D), lambda i, j: (i // per_b, 0, 0))
    sc_spec = pl.BlockSpec((1, 1, D), lambda i, j: (i // per_b, 0, 1))
    w_spec = pl.BlockSpec((None, D, tn), lambda i, j: (wl, 0, j))
    o_spec = pl.BlockSpec((tm, tn), lambda i, j: (i, j))
    grid = (n_tok // tm, ncols // tn)
    scratch = [pltpu.VMEM((tm, 1), F32)]
    nl = w_lora_bf.shape[1]
    vmem = 2 * (4 * tm * D + 2 * D * tn + 4 * tm * tn + 2 * D * nl + 4 * tm * nl) + 4 * 1024 * 1024
    return pl.pallas_call(
        _in_proj_lora_kernel,
        out_shape=(jax.ShapeDtypeStruct((n_tok, ncols), F32), jax.ShapeDtypeStruct((n_tok, nl), F32)),
        grid=grid,
        in_specs=[x_spec, g_spec, sc_spec, sh_spec, w_spec, pl.BlockSpec((D, nl), lambda i, j: (0, 0))],
        out_specs=(o_spec, pl.BlockSpec((tm, nl), lambda i, j: (i, 0))),
        scratch_shapes=scratch,
        compiler_params=_cparams("parallel", "arbitrary", vmem=vmem),
        name="in_proj_lora",
    )(x2, g, mod3, mod3, w_bf, w_lora_bf)


CAST_ROWS = 256


def _round_weight(w_ref, w_scr):
    @pl.when(pl.program_id(0) == 0)
    def _():
        def body(i, carry):
            rows = pl.ds(pl.multiple_of(i * CAST_ROWS, CAST_ROWS), CAST_ROWS)
            w_scr[rows, :] = w_ref[rows, :].astype(BF16)
            return carry
        lax.fori_loop(0, w_ref.shape[0] // CAST_ROWS, body, 0)


def _out_proj_kernel(m_ref, w_ref, x_ref, gate_ref, o_ref, w_scr):
    _round_weight(w_ref, w_scr)
    y = jnp.dot(m_ref[...], w_scr[...], preferred_element_type=F32)
    o_ref[...] = x_ref[...] + gate_ref[0] * y


def _out_proj_final_kernel(m_ref, w_ref, x_ref, gate_ref, fg_ref, o_ref, w_scr):
    _round_weight(w_ref, w_scr)
    y = jnp.dot(m_ref[...], w_scr[...], preferred_element_type=F32)
    xn = x_ref[...] + gate_ref[0] * y
    o_ref[...] = xn * lax.rsqrt(jnp.mean(xn * xn, axis=-1, keepdims=True) + RMS_EPS) * fg_ref[...]


def _out_proj(mix, w, wl, x2, mod3, final_g=None, *, tm=512):
    n_tok = x2.shape[0]
    per_b = SEQ // tm
    in_specs = [pl.BlockSpec((tm, D), lambda i: (i, 0)),
                pl.BlockSpec((None, D, D), lambda i: (wl, 0, 0), pipeline_mode=pl.Buffered(1)),
                pl.BlockSpec((tm, D), lambda i: (i, 0)),
                pl.BlockSpec((1, 1, D), lambda i: (i // per_b, 0, 2))]
    args = [mix, w, x2, mod3]
    kern = _out_proj_kernel
    if final_g is not None:
        in_specs.append(pl.BlockSpec((1, D), lambda i: (0, 0)))
        args.append(final_g)
        kern = _out_proj_final_kernel
    return pl.pallas_call(
        kern,
        out_shape=jax.ShapeDtypeStruct((n_tok, D), F32),
        grid=(n_tok // tm,),
        in_specs=in_specs,
        out_specs=pl.BlockSpec((tm, D), lambda i: (i, 0)),
        scratch_shapes=[pltpu.VMEM((D, D), BF16)],
        compiler_params=_cparams("arbitrary", vmem=6 * D * D + 2 * (2 + 4 + 4) * tm * D + 8 * 1024 * 1024),
        name="out_proj_final" if final_g is not None else "out_proj",
    )(*args)


def _sg_mix_kernel(u_ref, v_ref, z_ref, lng_ref, lnb_ref, ws_ref, bs_ref, o_ref, *, tm):
    ti = lax.broadcasted_iota(jnp.int32, (SG_CHUNK, SG_CHUNK), 0)
    si = lax.broadcasted_iota(jnp.int32, (SG_CHUNK, SG_CHUNK), 1)
    causal = ti >= si
    ws = [jnp.where(causal, ws_ref[g], 0.0).astype(BF16) for g in range(SG_GROUPS)]
    for c in range(tm // SG_CHUNK):
        rows = pl.ds(c * SG_CHUNK, SG_CHUNK)
        v = v_ref[rows, :].astype(F32)
        mean = jnp.mean(v, axis=-1, keepdims=True)
        d = v - mean
        var = jnp.mean(d * d, axis=-1, keepdims=True)
        vn = (d * lax.rsqrt(var + LN_EPS) * lng_ref[...] + lnb_ref[...]).astype(BF16)
        f = [jnp.dot(ws[g], vn[:, g * LANES:(g + 1) * LANES], preferred_element_type=F32) for g in range(SG_GROUPS)]
        for g in range(SG_GROUPS):
            cols = pl.ds(g * LANES, LANES)
            uz = u_ref[rows, cols].astype(F32) * z_ref[rows, cols].astype(F32)
            o_ref[rows, cols] = (uz * (f[g] + bs_ref[g])).astype(BF16)


def _sg_mix(p, ln_g, ln_b, w_s, b_full, *, tm=512):
    n_tok = p.shape[0]
    return pl.pallas_call(
        functools.partial(_sg_mix_kernel, tm=tm),
        out_shape=jax.ShapeDtypeStruct((n_tok, D), BF16),
        grid=(n_tok // tm,),
        in_specs=[pl.BlockSpec((tm, D), lambda i: (i, 0)),
                  pl.BlockSpec((tm, D), lambda i: (i, 1)),
                  pl.BlockSpec((tm, D), lambda i: (i, 2)),
                  pl.BlockSpec((1, D), lambda i: (0, 0)),
                  pl.BlockSpec((1, D), lambda i: (0, 0)),
                  pl.BlockSpec((SG_GROUPS, SG_CHUNK, SG_CHUNK), lambda i: (0, 0, 0)),
                  pl.BlockSpec((SG_GROUPS, SG_CHUNK, LANES), lambda i: (0, 0, 0))],
        out_specs=pl.BlockSpec((tm, D), lambda i: (i, 0)),
        compiler_params=_cparams("parallel"),
        name="sg_mix",
    )(p, p, p, ln_g, ln_b, w_s, b_full)


def _first_half(shape):
    return (lax.broadcasted_iota(jnp.int32, shape, 1) % HEAD) < (HEAD // 2)


def _rope_table_kernel(pos_ref, invf_ref, cos_ref, sin_ref):
    ang = pos_ref[...].astype(F32) * invf_ref[...]
    cos_ref[...] = jnp.cos(ang)
    sin_ref[...] = jnp.where(_first_half(ang.shape), -jnp.sin(ang), jnp.sin(ang))


def _rope(x, cos, sin_signed, first):
    partner = jnp.where(first, pltpu.roll(x, LANES - HEAD // 2, 1), pltpu.roll(x, HEAD // 2, 1))
    return x * cos + partner * sin_signed


def _swa_mix_kernel(sink_ref, p_ref, kvp_ref, cos_ref, sin_ref, cosp_ref, sinp_ref, o_ref):
    i = pl.program_id(0)
    first_key = jnp.where((i % (SEQ // SWA_BLOCK)) != 0, 0, SWA_BLOCK)
    cos, sin_s, cosp, sinp_s = cos_ref[...], sin_ref[...], cosp_ref[...], sinp_ref[...]
    first = _first_half(cos.shape)
    lane = lax.broadcasted_iota(jnp.int32, (2 * SWA_BLOCK, LANES), 1)
    low = lane < HEAD

    qi = lax.broadcasted_iota(jnp.int32, (SWA_BLOCK, 2 * SWA_BLOCK), 0)
    kj = lax.broadcasted_iota(jnp.int32, (SWA_BLOCK, 2 * SWA_BLOCK), 1)
    rel = qi + SWA_BLOCK - kj
    mask = (rel >= 0) & (rel < SWA_BLOCK) & (kj >= first_key)

    k_lo, k_hi, v_lo, v_hi = [], [], [], []
    for kp in range(SWA_KV // 2):
        ksl = pl.ds(kp * LANES, LANES)
        vsl = pl.ds(SWA_KVW + kp * LANES, LANES)
        k2 = jnp.concatenate([_rope(kvp_ref[:, ksl], cosp, sinp_s, first),
                              _rope(p_ref[:, pl.ds(D + kp * LANES, LANES)], cos, sin_s, first)], axis=0)
        v2 = jnp.concatenate([kvp_ref[:, vsl], p_ref[:, pl.ds(D + SWA_KVW + kp * LANES, LANES)]], axis=0)
        k2r = pltpu.roll(k2, HEAD, 1)
        v2r = pltpu.roll(v2, HEAD, 1)
        for sub in range(2):
            src_k, src_kr = (k2, k2r) if sub == 0 else (k2r, k2)
            src_v, src_vr = (v2, v2r) if sub == 0 else (v2r, v2)
            k_lo.append(jnp.where(low, src_k, 0.0).astype(BF16))
            k_hi.append(jnp.where(low, 0.0, src_kr).astype(BF16))
            v_lo.append(jnp.where(low, src_v, 0.0).astype(BF16))
            v_hi.append(jnp.where(low, 0.0, src_vr).astype(BF16))

    zoff = D + 2 * SWA_KVW
    for h0 in range(0, SWA_HEADS, SWA_STAGE):
        hs = range(h0, h0 + SWA_STAGE)
        prs = range(h0 // 2, (h0 + SWA_STAGE) // 2)
        cols = {pr: pl.ds(pr * LANES, LANES) for pr in prs}
        q2 = {pr: (_rope(p_ref[:, cols[pr]], cos, sin_s, first) * (HEAD ** -0.5)).astype(BF16) for pr in prs}
        s = {h: _dot_nt(q2[h // 2], (k_hi if h % 2 else k_lo)[h // 8]) for h in hs}
        s = {h: jnp.where(mask, s[h], -jnp.inf) for h in hs}
        m = {h: jnp.maximum(jnp.max(s[h], axis=-1, keepdims=True), sink_ref[h]) for h in hs}
        e = {h: jnp.exp(s[h] - m[h]) for h in hs}
        denom = {h: jnp.sum(e[h], axis=-1, keepdims=True) + jnp.exp(sink_ref[h] - m[h]) for h in hs}
        prob = {h: (e[h] * (1.0 / denom[h])).astype(BF16) for h in hs}
        o = {h: jnp.dot(prob[h], (v_hi if h % 2 else v_lo)[h // 8], preferred_element_type=F32) for h in hs}
        for pr in prs:
            gate = p_ref[:, pl.ds(zoff + pr * LANES, LANES)]
            o_ref[:, cols[pr]] = ((o[2 * pr] + o[2 * pr + 1]) * gate).astype(BF16)


def _swa_mix(p, cos_t, sin_t, sinks):
    n_tok, ncols = p.shape
    tb = SWA_BLOCK
    kvb = D // (2 * SWA_KVW)
    cur = pl.BlockSpec((tb, LANES), lambda i: (i, 0))
    prev = pl.BlockSpec((tb, LANES), lambda i: (jnp.maximum(i - 1, 0), 0))
    return pl.pallas_call(
        _swa_mix_kernel,
        out_shape=jax.ShapeDtypeStruct((n_tok, D), BF16),
        grid=(n_tok // tb,),
        in_specs=[pl.BlockSpec(memory_space=pltpu.SMEM),
                  pl.BlockSpec((tb, ncols), lambda i: (i, 0)),
                  pl.BlockSpec((tb, 2 * SWA_KVW), lambda i: (jnp.maximum(i - 1, 0), kvb)),
                  cur, cur, prev, prev],
        out_specs=pl.BlockSpec((tb, D), lambda i: (i, 0)),
        compiler_params=_cparams("parallel"),
        name="swa_mix",
    )(sinks, p, p, cos_t, sin_t, cos_t, sin_t)


def _head_sums(xs, seg):
    out = jnp.dot(jnp.concatenate([x.astype(BF16) for x in xs], axis=0), seg, preferred_element_type=F32)
    n = xs[0].shape[0]
    return [out[i * n:(i + 1) * n] for i in range(len(xs))]


def _cumsum_rows(x, tri):
    hi = x.astype(BF16)
    lo = (x - hi.astype(F32)).astype(BF16)
    return jnp.dot(tri, hi, preferred_element_type=F32) + jnp.dot(tri, lo, preferred_element_type=F32)


def _bd(x, same_head):
    t = jnp.concatenate([x] * RW_HG, axis=0)
    return jnp.where(same_head, t, 0.0).astype(BF16)


def _diag_blocks(o, lane_head):
    acc = o[0:HEAD]
    for h in range(1, RW_HG):
        acc = jnp.where(lane_head == h, o[h * HEAD:(h + 1) * HEAD], acc)
    return acc


def _dot(a, b):
    return jnp.dot(a.astype(BF16), b.astype(BF16), preferred_element_type=F32)


def _dot_nt(a, b):
    return lax.dot_general(a, b, (((1,), (1,)), ((), ())), preferred_element_type=F32)


def _dot_tn(a, b):
    return lax.dot_general(a, b, (((0,), (0,)), ((), ())), preferred_element_type=F32)


def _rwkv_groups(r, lw, cum, k, v, kap, bet, h0, masks):
    incl, strict, eye, same_head, lane_head = masks
    c = RW_CHUNK
    gr = range(len(r))
    bd = lambda t: _bd(t, same_head)
    g_t = [jnp.exp(cum[g]) for g in gr]
    g_inv = [jnp.exp(-cum[g]) for g in gr]
    cref = [cum[g][c // 2 - 1:c // 2] for g in gr]
    cend = [cum[g][c - 1:c] for g in gr]
    to_ref = [jnp.exp(-cref[g]) for g in gr]
    r_true = [r[g] * g_t[g] for g in gr]
    kap_true = [kap[g] * (g_t[g] * jnp.exp(-lw[g])) for g in gr]
    r_s = [r_true[g] * to_ref[g] for g in gr]
    kap_s = [kap_true[g] * to_ref[g] for g in gr]
    ginv = [g_inv[g] * jnp.exp(cref[g]) for g in gr]
    eend = [g_inv[g] * jnp.exp(cend[g]) for g in gr]

    s = [_dot_nt(jnp.concatenate([r_s[g], kap_s[g]], axis=0).astype(BF16),
                 jnp.concatenate([bd(k[g] * ginv[g]), bd(bet[g] * ginv[g])], axis=0)) for g in gr]
    a_rk = [jnp.where(incl, s[g][:c, :RW_GW], 0.0) for g in gr]
    a_rb = [jnp.where(incl, s[g][:c, RW_GW:], 0.0) for g in gr]
    a_kk = [jnp.where(strict, s[g][c:, :RW_GW], 0.0) for g in gr]
    a_kb = [jnp.where(strict, s[g][c:, RW_GW:], 0.0) for g in gr]

    x = [eye - a_kb[g] for g in gr]
    p = [_dot(a_kb[g], bd(a_kb[g])) for g in gr]
    akv = [_dot(a_kk[g], bd(v[g])) for g in gr]
    for _ in range(4):
        xp = [_dot(jnp.concatenate([x[g], p[g]], axis=0), bd(p[g])) for g in gr]
        x = [x[g] + xp[g][:c] for g in gr]
        p = [xp[g][c:] for g in gr]
    t_inv = [x[g] + _dot(x[g], bd(p[g])) for g in gr]

    wu = [-_dot(t_inv[g], jnp.concatenate([bd(kap_true[g]), bd(akv[g])], axis=1)) for g in gr]

    hw = [_dot(jnp.concatenate([wu[g][:, :RW_GW], r_true[g], eye * jnp.exp(cend[g])], axis=0), bd(h0[g]))
          for g in gr]
    u = [hw[g][:c] + wu[g][:, RW_GW:] for g in gr]
    y = [hw[g][c:2 * c] + _dot(jnp.concatenate([a_rk[g], a_rb[g]], axis=1),
                               jnp.concatenate([bd(v[g]), bd(u[g])], axis=0)) for g in gr]
    o2 = [_dot_tn(jnp.concatenate([k[g] * eend[g], bet[g] * eend[g]], axis=0).astype(BF16),
                  jnp.concatenate([v[g], u[g]], axis=0).astype(BF16)) for g in gr]
    h_new = [hw[g][2 * c:] + _diag_blocks(o2[g], lane_head) for g in gr]
    return y, h_new


def _rwkv_mix_kernel(p_ref, pp_ref, pl_ref, plp_ref, mu_ref, mul_ref, w0_ref, a0_ref, wl_ref, al_ref,
                     kk_ref, ka_ref, rk_ref, gng_ref, gnb_ref, *rest):
    if len(rest) == 4:
        side_ref, o_ref, side_out_ref, h_scr = rest
        side_out_ref[...] = side_ref[...].astype(BF16)
    else:
        o_ref, h_scr = rest
    ci = pl.program_id(1)
    c = RW_CHUNK

    @pl.when(ci == 0)
    def _():
        h_scr[...] = jnp.zeros_like(h_scr)

    not_first = (ci != 0).astype(F32)
    row = lax.broadcasted_iota(jnp.int32, (c, 1), 0)

    def shifted(cur, prev_ref):
        prev_row = prev_ref[7:8, :] * not_first
        rolled = pltpu.roll(cur, 1, 0)
        return jnp.where(row == 0, prev_row, rolled)

    lcur = pl_ref[...]
    lcur = lcur + (shifted(lcur, plp_ref) - lcur) * mul_ref[...]
    dw = jnp.tanh(lcur[:, :RW_LP // 2])
    da = lcur[:, RW_LP // 2:]

    ti = lax.broadcasted_iota(jnp.int32, (c, RW_GW), 0)
    li = lax.broadcasted_iota(jnp.int32, (c, RW_GW), 1)
    si = li % HEAD
    lane_head = li // HEAD
    rb = lax.broadcasted_iota(jnp.int32, (RW_GW, RW_GW), 0) // HEAD
    cb = lax.broadcasted_iota(jnp.int32, (RW_GW, RW_GW), 1) // HEAD
    same_head = rb == cb
    seg = same_head.astype(BF16)
    masks = (ti >= si, ti > si, (ti == si).astype(F32), same_head, lane_head)

    gr = range(RW_GROUPS)
    gsl = [pl.ds(g * RW_GW, RW_GW) for g in gr]

    def lerp(off, g):
        cols = pl.ds(off + g * RW_GW, RW_GW)
        cur = p_ref[:, cols]
        return cur + (shifted(cur, pp_ref.at[:, cols]) - cur) * mu_ref[:, cols]

    r = [lerp(0, g) for g in gr]
    k = [lerp(D, g) for g in gr]
    v = [lerp(2 * D, g) for g in gr]
    lw = [-DECAY_SCALE * jax.nn.sigmoid(w0_ref[:, gsl[g]] + _dot(dw, wl_ref[:, gsl[g]])) for g in gr]
    a = [jax.nn.sigmoid(a0_ref[:, gsl[g]] + _dot(da, al_ref[:, gsl[g]])) for g in gr]
    kap = [k[g] * kk_ref[:, gsl[g]] for g in gr]
    k = [k[g] * (1.0 + (a[g] - 1.0) * ka_ref[:, gsl[g]]) for g in gr]
    sums = _head_sums([kap[g] * kap[g] for g in gr] + [r[g] * k[g] * rk_ref[:, gsl[g]] for g in gr], seg)
    kap = [kap[g] / jnp.maximum(jnp.sqrt(sums[g]), 1e-12) for g in gr]
    bonus = sums[RW_GROUPS:]
    bet = [kap[g] * a[g] for g in gr]
    tri = (lax.broadcasted_iota(jnp.int32, (c, c), 0) >= lax.broadcasted_iota(jnp.int32, (c, c), 1)).astype(BF16)
    cum = [_cumsum_rows(lw[g], tri) for g in gr]

    y, h_new = [], []
    for lo in range(0, RW_GROUPS, RW_STAGE):
        sl = slice(lo, lo + RW_STAGE)
        ys, hs = _rwkv_groups(r[sl], lw[sl], cum[sl], k[sl], v[sl], kap[sl], bet[sl],
                              [h_scr[g] for g in range(lo, lo + RW_STAGE)], masks)
        y += ys
        h_new += hs
    for g in gr:
        h_scr[g] = h_new[g]

    ysum = _head_sums(y, seg)
    dlt = [y[g] - ysum[g] * (1.0 / HEAD) for g in gr]
    dsum = _head_sums([dlt[g] * dlt[g] for g in gr], seg)
    for g in gr:
        rstd = lax.rsqrt(dsum[g] * (1.0 / HEAD) + GN_EPS)
        yn = dlt[g] * rstd * gng_ref[:, gsl[g]] + gnb_ref[:, gsl[g]] + bonus[g] * v[g]
        o_ref[:, gsl[g]] = (yn * jax.nn.silu(lerp(3 * D, g))).astype(BF16)


def _rwkv_mix(p, p_lora, mu, mu_l, w0, a0, w_lora, a_lora, k_k, k_a, r_k, gn_g, gn_b, side=None):
    n_tok = p.shape[0]
    c = RW_CHUNK
    nc = SEQ // c
    ncol = p.shape[1]
    nl = p_lora.shape[1]
    row = lambda b, ci: (b * nc + ci, 0)
    prev = lambda b, ci: (jnp.maximum((b * nc + ci) * (c // 8) - 1, 0), 0)
    vec = pl.BlockSpec((1, D), lambda b, ci: (0, 0))
    lora = pl.BlockSpec((RW_LP // 2, D), lambda b, ci: (0, 0))
    in_specs = [pl.BlockSpec((c, ncol), row),
                pl.BlockSpec((8, ncol), prev),
                pl.BlockSpec((c, nl), row),
                pl.BlockSpec((8, nl), prev),
                pl.BlockSpec((1, ncol), lambda b, ci: (0, 0)),
                pl.BlockSpec((1, nl), lambda b, ci: (0, 0)),
                vec, vec, lora, lora,
                vec, vec, vec, vec, vec]
    args = [p, p, p_lora, p_lora, mu, mu_l, w0, a0, w_lora, a_lora, k_k, k_a, r_k, gn_g, gn_b]
    out_shape = jax.ShapeDtypeStruct((n_tok, D), BF16)
    out_specs = pl.BlockSpec((c, D), row)
    if side is not None:
        side_w, sl = side
        steps = (n_tok // SEQ) * nc
        _, srows, scols = side_w.shape
        srow = srows // steps
        assert srows % steps == 0 and srow % 16 == 0
        in_specs.append(pl.BlockSpec((None, srow, scols), lambda b, ci: (sl, b * nc + ci, 0)))
        args.append(side_w)
        out_shape = (out_shape, jax.ShapeDtypeStruct((srows, scols), BF16))
        out_specs = (out_specs, pl.BlockSpec((srow, scols), row))
    return pl.pallas_call(
        _rwkv_mix_kernel,
        out_shape=out_shape,
        grid=(n_tok // SEQ, nc),
        in_specs=in_specs,
        out_specs=out_specs,
        scratch_shapes=[pltpu.VMEM((RW_GROUPS, HEAD, RW_GW), F32)],
        compiler_params=_cparams("parallel", "arbitrary"),
        name="rwkv_mix",
    )(*args)


def kernel(x, c, positions, norm_g, mod_w, mod_b, final_norm_g, sg_w_in, sg_w_out, sg_ln_g, sg_ln_b, sg_w_spatial,
           sg_b_spatial, swa_w_in, swa_w_out, swa_sinks, rwkv_w_in, rwkv_w_out, rwkv_mu, rwkv_w0, rwkv_w_lora,
           rwkv_a0, rwkv_a_lora, rwkv_k_k, rwkv_k_a, rwkv_r_k, rwkv_gn_g, rwkv_gn_b):
    batch, seq, d = x.shape
    assert (seq, d) == (SEQ, D) and norm_g.shape[0] == DEPTH
    n_tok = batch * seq
    x2 = x.reshape(n_tok, D)
    c8 = jnp.zeros((8, D), F32).at[:batch].set(c)
    pos2 = positions.reshape(n_tok, 1)
    half = HEAD // 2
    inv_freq = ROPE_THETA ** (-jnp.arange(half, dtype=F32) / half)
    invf = jnp.tile(inv_freq, LANES // half).reshape(1, LANES)
    mod, cos_t, sin_t = _modulation(c8, mod_w, mod_b, pos2, invf)
    row = lambda a: a.reshape(1, -1)

    def act_rows(spans):
        rows = [jnp.concatenate([jnp.full((w,), val[t], F32) for w, *val in spans]) for t in range(3)]
        return jnp.zeros((8, rows[0].shape[0]), F32).at[:3].set(jnp.stack(rows))

    gelu_span = (GELU_A, GELU_B, 1.0)
    silu_span = (0.5, 0.0, 1.0)
    sg_first_bf, rwkv_in_bf = sg_w_in[:1].astype(BF16), rwkv_w_in.astype(BF16)
    w_out = (sg_w_out, swa_w_out, rwkv_w_out)
    side_bf = None
    for i in range(DEPTH):
        kind, j = i % 3, i // 3
        mod3 = mod[i, :batch].reshape(batch, 1, 3 * D)
        g = row(norm_g[i])
        has_next = i + 1 < DEPTH
        if kind == 0:
            act = act_rows([(2 * D, *gelu_span), (D, *silu_span)])
            w = sg_first_bf if i == 0 else side_bf[None]
            p = _in_proj_act(x2, g, mod3, w, 0, 3 * D, act, False, BF16, (swa_w_in, j) if has_next else None)
            p, side_bf = p if has_next else (p, None)
            mix = _sg_mix(p, row(sg_ln_g[j]), row(sg_ln_b[j]), sg_w_spatial[j],
                          jnp.broadcast_to(sg_b_spatial[j][:, :, None], (SG_GROUPS, SG_CHUNK, LANES)))
        elif kind == 1:
            act = act_rows([(D + 2 * SWA_KVW, 0.0, 0.0, 0.0), (D, *silu_span)])
            p = _in_proj_act(x2, g, mod3, side_bf[None], 0, 2 * D + 2 * SWA_KVW, act, True, F32)
            mix = _swa_mix(p, cos_t, sin_t, swa_sinks[j])
        else:
            lp = RW_LP // 2 - RW_LORA
            w_in = rwkv_in_bf[j]
            w_l = jnp.concatenate([jnp.pad(w_in[:, 4 * D:4 * D + RW_LORA], ((0, 0), (0, lp))),
                                   jnp.pad(w_in[:, 4 * D + RW_LORA:], ((0, 0), (0, lp)))], axis=1)
            p, p_lora = _in_proj(x2, g, mod3, rwkv_in_bf, j, 4 * D, w_l)
            mu = rwkv_mu[j]
            mu_l = jnp.concatenate([jnp.pad(mu[4 * D:4 * D + RW_LORA], (0, lp)), jnp.pad(mu[4 * D + RW_LORA:], (0, lp))])
            wl_pad = jnp.pad(rwkv_w_lora[j], ((0, lp), (0, 0))).astype(BF16)
            al_pad = jnp.pad(rwkv_a_lora[j], ((0, lp), (0, 0))).astype(BF16)
            mix = _rwkv_mix(p, p_lora, row(mu[:4 * D]), row(mu_l), row(rwkv_w0[j]),
                            row(rwkv_a0[j]), wl_pad, al_pad, row(rwkv_k_k[j]),
                            row(rwkv_k_a[j]), row(rwkv_r_k[j]), row(rwkv_gn_g[j]), row(rwkv_gn_b[j]),
                            (sg_w_in, j + 1) if has_next else None)
            mix, side_bf = mix if has_next else (mix, None)
        x2 = _out_proj(mix, w_out[kind], j, x2, mod3, row(final_norm_g) if i == DEPTH - 1 else None)
    return x2.reshape(batch, seq, D)
```

```python
import functools
import math

import jax
import jax.numpy as jnp
from jax import lax
from jax.experimental import pallas as pl
from jax.experimental.pallas import tpu as pltpu

F32 = jnp.float32
BF16 = jnp.bfloat16

D = 2048
SEQ = 4096
DEPTH = 4
HEAD = 64
LANES = 128
SG_CHUNK = 128
SG_GROUPS = 16
SWA_HEADS = D // HEAD
SWA_KV = SWA_HEADS // 8
SWA_KVW = SWA_KV * HEAD
SWA_BLOCK = 128
SWA_STAGE = 16
ROPE_THETA = 10000.0
RW_HEADS = D // HEAD
RW_LORA = 96
RW_LP = 256
RW_CHUNK = 64
RW_HG = 4
RW_GW = RW_HG * HEAD
RW_GROUPS = D // RW_GW
RW_STAGE = 8
DECAY_SCALE = math.exp(-0.5)
GN_EPS = 64e-5
RMS_EPS = 1e-6
LN_EPS = 1e-5

VMEM_LIMIT = 48 * 1024 * 1024


def _cparams(*sem, vmem=VMEM_LIMIT):
    return pltpu.CompilerParams(dimension_semantics=sem, vmem_limit_bytes=vmem)


def _mod_kernel(c_ref, w_ref, b_ref, pos_ref, invf_ref, o_ref, cos_ref, sin_ref):
    cond = jax.nn.silu(c_ref[...]).astype(BF16)
    o_ref[0] = jnp.dot(cond, w_ref[0].astype(BF16), preferred_element_type=F32) + b_ref[0]
    _rope_table_kernel(pos_ref, invf_ref, cos_ref, sin_ref)


def _modulation(c8, mod_w, mod_b, pos2, invf):
    tn = 768
    n = mod_w.shape[2]
    nj = n // tn
    n_tok = pos2.shape[0]
    rows = n_tok // (DEPTH * nj)
    assert n % tn == 0 and rows * DEPTH * nj == n_tok and rows % 8 == 0
    table = pl.BlockSpec((rows, LANES), lambda l, j: (l * nj + j, 0))
    return pl.pallas_call(
        _mod_kernel,
        out_shape=(jax.ShapeDtypeStruct((DEPTH, 8, n), F32),
                   jax.ShapeDtypeStruct((n_tok, LANES), F32), jax.ShapeDtypeStruct((n_tok, LANES), F32)),
        grid=(DEPTH, nj),
        in_specs=[pl.BlockSpec((8, D), lambda l, j: (0, 0)),
                  pl.BlockSpec((1, D, tn), lambda l, j: (l, 0, j)),
                  pl.BlockSpec((1, 1, tn), lambda l, j: (l, 0, j)),
                  pl.BlockSpec((rows, 1), lambda l, j: (l * nj + j, 0)),
                  pl.BlockSpec((1, LANES), lambda l, j: (0, 0))],
        out_specs=(pl.BlockSpec((1, 8, tn), lambda l, j: (l, 0, j)), table, table),
        compiler_params=_cparams("parallel", "parallel"),
        name="modulation",
    )(c8, mod_w, mod_b.reshape(DEPTH, 1, n), pos2, invf)


NORM_ROWS = 128


def _row_rsqrt(x_ref, rs_scr):
    def body(i, carry):
        rows = pl.ds(pl.multiple_of(i * NORM_ROWS, NORM_ROWS), NORM_ROWS)
        x = x_ref[rows, :]
        rs_scr[rows, :] = lax.rsqrt(jnp.mean(x * x, axis=-1, keepdims=True) + RMS_EPS)
        return carry
    lax.fori_loop(0, x_ref.shape[0] // NORM_ROWS, body, 0, unroll=True)


def _modulated(x_ref, rs_scr, g_ref, sc_ref, sh_ref):
    gain = g_ref[...] * (1.0 + sc_ref[0])
    return (x_ref[...] * rs_scr[...] * gain + sh_ref[0]).astype(BF16)


GELU_A = math.sqrt(2.0 / math.pi)
GELU_B = GELU_A * 0.044715
ACT_COLS = 256


def _in_proj_act_kernel(x_ref, g_ref, sc_ref, sh_ref, w_ref, act_ref, *rest, select):
    if len(rest) == 4:
        side_ref, o_ref, side_out_ref, rs_scr = rest
        side_out_ref[...] = side_ref[...].astype(BF16)
    else:
        o_ref, rs_scr = rest

    @pl.when(pl.program_id(1) == 0)
    def _():
        _row_rsqrt(x_ref, rs_scr)

    h = _modulated(x_ref, rs_scr, g_ref, sc_ref, sh_ref)
    for n in range(o_ref.shape[1] // ACT_COLS):
        cols = pl.ds(n * ACT_COLS, ACT_COLS)
        p = jnp.dot(h, w_ref[:, cols], preferred_element_type=F32)
        gated = (0.5 * p) * (1.0 + jnp.tanh(p * (act_ref[0:1, cols] + act_ref[1:2, cols] * (p * p))))
        o_ref[:, cols] = (jnp.where(act_ref[2:3, cols] > 0.0, gated, p) if select else gated).astype(o_ref.dtype)


def _in_proj_lora_kernel(x_ref, g_ref, sc_ref, sh_ref, w_ref, wl_ref, o_ref, ol_ref, rs_scr):
    @pl.when(pl.program_id(1) == 0)
    def _():
        _row_rsqrt(x_ref, rs_scr)
        ol_ref[...] = jnp.dot(_modulated(x_ref, rs_scr, g_ref, sc_ref, sh_ref), wl_ref[...],
                              preferred_element_type=F32)

    h = _modulated(x_ref, rs_scr, g_ref, sc_ref, sh_ref)
    o_ref[...] = jnp.dot(h, w_ref[...], preferred_element_type=F32)


SIDE_ROWS = 128


def _in_proj_act(x2, g, mod3, w_bf, wl, ncols, act, select, out_dtype, side=None, *, tm=1024):
    n_tok = x2.shape[0]
    tn = 2048 if (out_dtype == BF16 and ncols % 2048 == 0) else (1024 if ncols % 1024 == 0 else 1536)
    assert ncols % tn == 0
    per_b = SEQ // tm
    nj = ncols // tn
    in_specs = [pl.BlockSpec((tm, D), lambda i, j: (i, 0)),
                pl.BlockSpec((1, D), lambda i, j: (0, 0)),
                pl.BlockSpec((1, 1, D), lambda i, j: (i // per_b, 0, 1)),
                pl.BlockSpec((1, 1, D), lambda i, j: (i // per_b, 0, 0)),
                pl.BlockSpec((None, D, tn), lambda i, j: (wl, 0, j)),
                pl.BlockSpec((8, tn), lambda i, j: (0, j))]
    out_shape = jax.ShapeDtypeStruct((n_tok, ncols), out_dtype)
    out_specs = pl.BlockSpec((tm, tn), lambda i, j: (i, j))
    args = [x2, g, mod3, mod3, w_bf, act]
    osize = jnp.dtype(out_dtype).itemsize
    vmem = 2 * (4 * tm * D + 2 * D * tn + osize * tm * tn) + 8 * 1024 * 1024
    if side is not None:
        side_w, sl = side
        _, srows, scols = side_w.shape
        nblk = srows // SIDE_ROWS
        assert srows % SIDE_ROWS == 0 and nblk <= (n_tok // tm) * nj
        blk = lambda i, j: jnp.minimum(i * nj + j, nblk - 1)
        in_specs.append(pl.BlockSpec((None, SIDE_ROWS, scols), lambda i, j: (sl, blk(i, j), 0)))
        args.append(side_w)
        out_shape = (out_shape, jax.ShapeDtypeStruct((srows, scols), BF16))
        out_specs = (out_specs, pl.BlockSpec((SIDE_ROWS, scols), lambda i, j: (blk(i, j), 0)))
        vmem += 2 * (4 + 2) * SIDE_ROWS * scols
    return pl.pallas_call(
        functools.partial(_in_proj_act_kernel, select=select),
        out_shape=out_shape,
        grid=(n_tok // tm, nj),
        in_specs=in_specs,
        out_specs=out_specs,
        scratch_shapes=[pltpu.VMEM((tm, 1), F32)],
        compiler_params=_cparams("arbitrary", "arbitrary", vmem=vmem),
        name="in_proj_act",
    )(*args)


def _in_proj(x2, g, mod3, w_bf, wl, ncols, w_lora_bf, *, tm=1024):
    n_tok = x2.shape[0]
    tn = 2048
    assert ncols % tn == 0
    per_b = SEQ // tm
    x_spec = pl.BlockSpec((tm, D), lambda i, j: (i, 0))
    g_spec = pl.BlockSpec((1, D), lambda i, j: (0, 0))
    sh_spec = pl.BlockSpec((1, 1, D), lambda i, j: (i // per_b, 0, 0))
    sc_spec = pl.BlockSpec((1, 1, D), lambda i, j: (i // per_b, 0, 1))
    w_spec = pl.BlockSpec((None, D, tn), lambda i, j: (wl, 0, j))
    o_spec = pl.BlockSpec((tm, tn), lambda i, j: (i, j))
    grid = (n_tok // tm, ncols // tn)
    scratch = [pltpu.VMEM((tm, 1), F32)]
    nl = w_lora_bf.shape[1]
    vmem = 2 * (4 * tm * D + 2 * D * tn + 4 * tm * tn + 2 * D * nl + 4 * tm * nl) + 4 * 1024 * 1024
    return pl.pallas_call(
        _in_proj_lora_kernel,
        out_shape=(jax.ShapeDtypeStruct((n_tok, ncols), F32), jax.ShapeDtypeStruct((n_tok, nl), F32)),
        grid=grid,
        in_specs=[x_spec, g_spec, sc_spec, sh_spec, w_spec, pl.BlockSpec((D, nl), lambda i, j: (0, 0))],
        out_specs=(o_spec, pl.BlockSpec((tm, nl), lambda i, j: (i, 0))),
        scratch_shapes=scratch,
        compiler_params=_cparams("parallel", "arbitrary", vmem=vmem),
        name="in_proj_lora",
    )(x2, g, mod3, mod3, w_bf, w_lora_bf)


CAST_ROWS = 256


def _round_weight(w_ref, w_scr):
    @pl.when(pl.program_id(0) == 0)
    def _():
        def body(i, carry):
            rows = pl.ds(pl.multiple_of(i * CAST_ROWS, CAST_ROWS), CAST_ROWS)
            w_scr[rows, :] = w_ref[rows, :].astype(BF16)
            return carry
        lax.fori_loop(0, w_ref.shape[0] // CAST_ROWS, body, 0, unroll=True)


def _out_proj_kernel(m_ref, w_ref, x_ref, gate_ref, o_ref, w_scr):
    _round_weight(w_ref, w_scr)
    y = jnp.dot(m_ref[...], w_scr[...], preferred_element_type=F32)
    o_ref[...] = x_ref[...] + gate_ref[0] * y


def _out_proj_final_kernel(m_ref, w_ref, x_ref, gate_ref, fg_ref, o_ref, w_scr):
    _round_weight(w_ref, w_scr)
    y = jnp.dot(m_ref[...], w_scr[...], preferred_element_type=F32)
    xn = x_ref[...] + gate_ref[0] * y
    o_ref[...] = xn * lax.rsqrt(jnp.mean(xn * xn, axis=-1, keepdims=True) + RMS_EPS) * fg_ref[...]


def _out_proj(mix, w, wl, x2, mod3, final_g=None, *, tm=512):
    n_tok = x2.shape[0]
    per_b = SEQ // tm
    in_specs = [pl.BlockSpec((tm, D), lambda i: (i, 0)),
                pl.BlockSpec((None, D, D), lambda i: (wl, 0, 0), pipeline_mode=pl.Buffered(1)),
                pl.BlockSpec((tm, D), lambda i: (i, 0)),
                pl.BlockSpec((1, 1, D), lambda i: (i // per_b, 0, 2))]
    args = [mix, w, x2, mod3]
    kern = _out_proj_kernel
    if final_g is not None:
        in_specs.append(pl.BlockSpec((1, D), lambda i: (0, 0)))
        args.append(final_g)
        kern = _out_proj_final_kernel
    return pl.pallas_call(
        kern,
        out_shape=jax.ShapeDtypeStruct((n_tok, D), F32),
        grid=(n_tok // tm,),
        in_specs=in_specs,
        out_specs=pl.BlockSpec((tm, D), lambda i: (i, 0)),
        scratch_shapes=[pltpu.VMEM((D, D), BF16)],
        compiler_params=_cparams("arbitrary", vmem=6 * D * D + 2 * (2 + 4 + 4) * tm * D + 8 * 1024 * 1024),
        name="out_proj_final" if final_g is not None else "out_proj",
    )(*args)


def _sg_mix_kernel(u_ref, v_ref, z_ref, lng_ref, lnb_ref, ws_ref, bs_ref, o_ref, *, tm):
    ti = lax.broadcasted_iota(jnp.int32, (SG_CHUNK, SG_CHUNK), 0)
    si = lax.broadcasted_iota(jnp.int32, (SG_CHUNK, SG_CHUNK), 1)
    causal = ti >= si
    ws = [jnp.where(causal, ws_ref[g], 0.0).astype(BF16) for g in range(SG_GROUPS)]
    for c in range(tm // SG_CHUNK):
        rows = pl.ds(c * SG_CHUNK, SG_CHUNK)
        v = v_ref[rows, :].astype(F32)
        mean = jnp.mean(v, axis=-1, keepdims=True)
        d = v - mean
        var = jnp.mean(d * d, axis=-1, keepdims=True)
        vn = (d * lax.rsqrt(var + LN_EPS) * lng_ref[...] + lnb_ref[...]).astype(BF16)
        f = [jnp.dot(ws[g], vn[:, g * LANES:(g + 1) * LANES], preferred_element_type=F32) for g in range(SG_GROUPS)]
        for g in range(SG_GROUPS):
            cols = pl.ds(g * LANES, LANES)
            uz = u_ref[rows, cols].astype(F32) * z_ref[rows, cols].astype(F32)
            o_ref[rows, cols] = (uz * (f[g] + bs_ref[g])).astype(BF16)


def _sg_mix(p, ln_g, ln_b, w_s, b_full, *, tm=512):
    n_tok = p.shape[0]
    return pl.pallas_call(
        functools.partial(_sg_mix_kernel, tm=tm),
        out_shape=jax.ShapeDtypeStruct((n_tok, D), BF16),
        grid=(n_tok // tm,),
        in_specs=[pl.BlockSpec((tm, D), lambda i: (i, 0)),
                  pl.BlockSpec((tm, D), lambda i: (i, 1)),
                  pl.BlockSpec((tm, D), lambda i: (i, 2)),
                  pl.BlockSpec((1, D), lambda i: (0, 0)),
                  pl.BlockSpec((1, D), lambda i: (0, 0)),
                  pl.BlockSpec((SG_GROUPS, SG_CHUNK, SG_CHUNK), lambda i: (0, 0, 0)),
                  pl.BlockSpec((SG_GROUPS, SG_CHUNK, LANES), lambda i: (0, 0, 0))],
        out_specs=pl.BlockSpec((tm, D), lambda i: (i, 0)),
        compiler_params=_cparams("parallel"),
        name="sg_mix",
    )(p, p, p, ln_g, ln_b, w_s, b_full)


def _first_half(shape):
    return (lax.broadcasted_iota(jnp.int32, shape, 1) % HEAD) < (HEAD // 2)


def _rope_table_kernel(pos_ref, invf_ref, cos_ref, sin_ref):
    ang = pos_ref[...].astype(F32) * invf_ref[...]
    cos_ref[...] = jnp.cos(ang)
    sin_ref[...] = jnp.where(_first_half(ang.shape), -jnp.sin(ang), jnp.sin(ang))


def _rope(x, cos, sin_signed, first):
    partner = jnp.where(first, pltpu.roll(x, LANES - HEAD // 2, 1), pltpu.roll(x, HEAD // 2, 1))
    return x * cos + partner * sin_signed


def _swa_mix_kernel(sink_ref, p_ref, kvp_ref, cos_ref, sin_ref, cosp_ref, sinp_ref, o_ref):
    i = pl.program_id(0)
    first_key = jnp.where((i % (SEQ // SWA_BLOCK)) != 0, 0, SWA_BLOCK)
    cos, sin_s, cosp, sinp_s = cos_ref[...], sin_ref[...], cosp_ref[...], sinp_ref[...]
    first = _first_half(cos.shape)
    lane = lax.broadcasted_iota(jnp.int32, (2 * SWA_BLOCK, LANES), 1)
    low = lane < HEAD

    qi = lax.broadcasted_iota(jnp.int32, (SWA_BLOCK, 2 * SWA_BLOCK), 0)
    kj = lax.broadcasted_iota(jnp.int32, (SWA_BLOCK, 2 * SWA_BLOCK), 1)
    rel = qi + SWA_BLOCK - kj
    mask = (rel >= 0) & (rel < SWA_BLOCK) & (kj >= first_key)

    k_lo, k_hi, v_lo, v_hi = [], [], [], []
    for kp in range(SWA_KV // 2):
        ksl = pl.ds(kp * LANES, LANES)
        vsl = pl.ds(SWA_KVW + kp * LANES, LANES)
        k2 = jnp.concatenate([_rope(kvp_ref[:, ksl], cosp, sinp_s, first),
                              _rope(p_ref[:, pl.ds(D + kp * LANES, LANES)], cos, sin_s, first)], axis=0)
        v2 = jnp.concatenate([kvp_ref[:, vsl], p_ref[:, pl.ds(D + SWA_KVW + kp * LANES, LANES)]], axis=0)
        k2r = pltpu.roll(k2, HEAD, 1)
        v2r = pltpu.roll(v2, HEAD, 1)
        for sub in range(2):
            src_k, src_kr = (k2, k2r) if sub == 0 else (k2r, k2)
            src_v, src_vr = (v2, v2r) if sub == 0 else (v2r, v2)
            k_lo.append(jnp.where(low, src_k, 0.0).astype(BF16))
            k_hi.append(jnp.where(low, 0.0, src_kr).astype(BF16))
            v_lo.append(jnp.where(low, src_v, 0.0).astype(BF16))
            v_hi.append(jnp.where(low, 0.0, src_vr).astype(BF16))

    zoff = D + 2 * SWA_KVW
    for h0 in range(0, SWA_HEADS, SWA_STAGE):
        hs = range(h0, h0 + SWA_STAGE)
        prs = range(h0 // 2, (h0 + SWA_STAGE) // 2)
        cols = {pr: pl.ds(pr * LANES, LANES) for pr in prs}
        q2 = {pr: (_rope(p_ref[:, cols[pr]], cos, sin_s, first) * (HEAD ** -0.5)).astype(BF16) for pr in prs}
        s = {h: _dot_nt(q2[h // 2], (k_hi if h % 2 else k_lo)[h // 8]) for h in hs}
        s = {h: jnp.where(mask, s[h], -jnp.inf) for h in hs}
        m = {h: jnp.maximum(jnp.max(s[h], axis=-1, keepdims=True), sink_ref[h]) for h in hs}
        e = {h: jnp.exp(s[h] - m[h]) for h in hs}
        denom = {h: jnp.sum(e[h], axis=-1, keepdims=True) + jnp.exp(sink_ref[h] - m[h]) for h in hs}
        prob = {h: (e[h] * (1.0 / denom[h])).astype(BF16) for h in hs}
        o = {h: jnp.dot(prob[h], (v_hi if h % 2 else v_lo)[h // 8], preferred_element_type=F32) for h in hs}
        for pr in prs:
            gate = p_ref[:, pl.ds(zoff + pr * LANES, LANES)]
            o_ref[:, cols[pr]] = ((o[2 * pr] + o[2 * pr + 1]) * gate).astype(BF16)


def _swa_mix(p, cos_t, sin_t, sinks):
    n_tok, ncols = p.shape
    tb = SWA_BLOCK
    kvb = D // (2 * SWA_KVW)
    cur = pl.BlockSpec((tb, LANES), lambda i: (i, 0))
    prev = pl.BlockSpec((tb, LANES), lambda i: (jnp.maximum(i - 1, 0), 0))
    return pl.pallas_call(
        _swa_mix_kernel,
        out_shape=jax.ShapeDtypeStruct((n_tok, D), BF16),
        grid=(n_tok // tb,),
        in_specs=[pl.BlockSpec(memory_space=pltpu.SMEM),
                  pl.BlockSpec((tb, ncols), lambda i: (i, 0)),
                  pl.BlockSpec((tb, 2 * SWA_KVW), lambda i: (jnp.maximum(i - 1, 0), kvb)),
                  cur, cur, prev, prev],
        out_specs=pl.BlockSpec((tb, D), lambda i: (i, 0)),
        compiler_params=_cparams("parallel"),
        name="swa_mix",
    )(sinks, p, p, cos_t, sin_t, cos_t, sin_t)


def _head_sums(xs, seg):
    out = jnp.dot(jnp.concatenate([x.astype(BF16) for x in xs], axis=0), seg, preferred_element_type=F32)
    n = xs[0].shape[0]
    return [out[i * n:(i + 1) * n] for i in range(len(xs))]


def _cumsum_rows(x, tri):
    hi = x.astype(BF16)
    lo = (x - hi.astype(F32)).astype(BF16)
    return jnp.dot(tri, hi, preferred_element_type=F32) + jnp.dot(tri, lo, preferred_element_type=F32)


def _bd(x, same_head):
    t = jnp.concatenate([x] * RW_HG, axis=0)
    return jnp.where(same_head, t, 0.0).astype(BF16)


def _diag_blocks(o, lane_head):
    acc = o[0:HEAD]
    for h in range(1, RW_HG):
        acc = jnp.where(lane_head == h, o[h * HEAD:(h + 1) * HEAD], acc)
    return acc


def _dot(a, b):
    return jnp.dot(a.astype(BF16), b.astype(BF16), preferred_element_type=F32)


def _dot_nt(a, b):
    return lax.dot_general(a, b, (((1,), (1,)), ((), ())), preferred_element_type=F32)


def _dot_tn(a, b):
    return lax.dot_general(a, b, (((0,), (0,)), ((), ())), preferred_element_type=F32)


def _rwkv_groups(r, lw, cum, k, v, kap, bet, h0, masks):
    incl, strict, eye, same_head, lane_head = masks
    c = RW_CHUNK
    gr = range(len(r))
    bd = lambda t: _bd(t, same_head)
    g_t = [jnp.exp(cum[g]) for g in gr]
    g_inv = [jnp.exp(-cum[g]) for g in gr]
    cref = [cum[g][c // 2 - 1:c // 2] for g in gr]
    cend = [cum[g][c - 1:c] for g in gr]
    to_ref = [jnp.exp(-cref[g]) for g in gr]
    r_true = [r[g] * g_t[g] for g in gr]
    kap_true = [kap[g] * (g_t[g] * jnp.exp(-lw[g])) for g in gr]
    r_s = [r_true[g] * to_ref[g] for g in gr]
    kap_s = [kap_true[g] * to_ref[g] for g in gr]
    ginv = [g_inv[g] * jnp.exp(cref[g]) for g in gr]
    eend = [g_inv[g] * jnp.exp(cend[g]) for g in gr]

    s = [_dot_nt(jnp.concatenate([r_s[g], kap_s[g]], axis=0).astype(BF16),
                 jnp.concatenate([bd(k[g] * ginv[g]), bd(bet[g] * ginv[g])], axis=0)) for g in gr]
    a_rk = [jnp.where(incl, s[g][:c, :RW_GW], 0.0) for g in gr]
    a_rb = [jnp.where(incl, s[g][:c, RW_GW:], 0.0) for g in gr]
    a_kk = [jnp.where(strict, s[g][c:, :RW_GW], 0.0) for g in gr]
    a_kb = [jnp.where(strict, s[g][c:, RW_GW:], 0.0) for g in gr]

    x = [eye - a_kb[g] for g in gr]
    p = [_dot(a_kb[g], bd(a_kb[g])) for g in gr]
    akv = [_dot(a_kk[g], bd(v[g])) for g in gr]
    for _ in range(4):
        xp = [_dot(jnp.concatenate([x[g], p[g]], axis=0), bd(p[g])) for g in gr]
        x = [x[g] + xp[g][:c] for g in gr]
        p = [xp[g][c:] for g in gr]
    t_inv = [x[g] + _dot(x[g], bd(p[g])) for g in gr]

    wu = [-_dot(t_inv[g], jnp.concatenate([bd(kap_true[g]), bd(akv[g])], axis=1)) for g in gr]

    hw = [_dot(jnp.concatenate([wu[g][:, :RW_GW], r_true[g], eye * jnp.exp(cend[g])], axis=0), bd(h0[g]))
          for g in gr]
    u = [hw[g][:c] + wu[g][:, RW_GW:] for g in gr]
    y = [hw[g][c:2 * c] + _dot(jnp.concatenate([a_rk[g], a_rb[g]], axis=1),
                               jnp.concatenate([bd(v[g]), bd(u[g])], axis=0)) for g in gr]
    o2 = [_dot_tn(jnp.concatenate([k[g] * eend[g], bet[g] * eend[g]], axis=0).astype(BF16),
                  jnp.concatenate([v[g], u[g]], axis=0).astype(BF16)) for g in gr]
    h_new = [hw[g][2 * c:] + _diag_blocks(o2[g], lane_head) for g in gr]
    return y, h_new


def _rwkv_mix_kernel(p_ref, pp_ref, pl_ref, plp_ref, mu_ref, mul_ref, w0_ref, a0_ref, wl_ref, al_ref,
                     kk_ref, ka_ref, rk_ref, gng_ref, gnb_ref, *rest):
    if len(rest) == 4:
        side_ref, o_ref, side_out_ref, h_scr = rest
        side_out_ref[...] = side_ref[...].astype(BF16)
    else:
        o_ref, h_scr = rest
    ci = pl.program_id(1)
    c = RW_CHUNK

    @pl.when(ci == 0)
    def _():
        h_scr[...] = jnp.zeros_like(h_scr)

    not_first = (ci != 0).astype(F32)
    row = lax.broadcasted_iota(jnp.int32, (c, 1), 0)

    def shifted(cur, prev_ref):
        prev_row = prev_ref[7:8, :] * not_first
        rolled = pltpu.roll(cur, 1, 0)
        return jnp.where(row == 0, prev_row, rolled)

    lcur = pl_ref[...]
    lcur = lcur + (shifted(lcur, plp_ref) - lcur) * mul_ref[...]
    dw = jnp.tanh(lcur[:, :RW_LP // 2])
    da = lcur[:, RW_LP // 2:]

    ti = lax.broadcasted_iota(jnp.int32, (c, RW_GW), 0)
    li = lax.broadcasted_iota(jnp.int32, (c, RW_GW), 1)
    si = li % HEAD
    lane_head = li // HEAD
    rb = lax.broadcasted_iota(jnp.int32, (RW_GW, RW_GW), 0) // HEAD
    cb = lax.broadcasted_iota(jnp.int32, (RW_GW, RW_GW), 1) // HEAD
    same_head = rb == cb
    seg = same_head.astype(BF16)
    masks = (ti >= si, ti > si, (ti == si).astype(F32), same_head, lane_head)

    gr = range(RW_GROUPS)
    gsl = [pl.ds(g * RW_GW, RW_GW) for g in gr]

    def lerp(off, g):
        cols = pl.ds(off + g * RW_GW, RW_GW)
        cur = p_ref[:, cols]
        return cur + (shifted(cur, pp_ref.at[:, cols]) - cur) * mu_ref[:, cols]

    r = [lerp(0, g) for g in gr]
    k = [lerp(D, g) for g in gr]
    v = [lerp(2 * D, g) for g in gr]
    lw = [-DECAY_SCALE * jax.nn.sigmoid(w0_ref[:, gsl[g]] + _dot(dw, wl_ref[:, gsl[g]])) for g in gr]
    a = [jax.nn.sigmoid(a0_ref[:, gsl[g]] + _dot(da, al_ref[:, gsl[g]])) for g in gr]
    kap = [k[g] * kk_ref[:, gsl[g]] for g in gr]
    k = [k[g] * (1.0 + (a[g] - 1.0) * ka_ref[:, gsl[g]]) for g in gr]
    sums = _head_sums([kap[g] * kap[g] for g in gr] + [r[g] * k[g] * rk_ref[:, gsl[g]] for g in gr], seg)
    kap = [kap[g] / jnp.maximum(jnp.sqrt(sums[g]), 1e-12) for g in gr]
    bonus = sums[RW_GROUPS:]
    bet = [kap[g] * a[g] for g in gr]
    tri = (lax.broadcasted_iota(jnp.int32, (c, c), 0) >= lax.broadcasted_iota(jnp.int32, (c, c), 1)).astype(BF16)
    cum = [_cumsum_rows(lw[g], tri) for g in gr]

    y, h_new = [], []
    for lo in range(0, RW_GROUPS, RW_STAGE):
        sl = slice(lo, lo + RW_STAGE)
        ys, hs = _rwkv_groups(r[sl], lw[sl], cum[sl], k[sl], v[sl], kap[sl], bet[sl],
                              [h_scr[g] for g in range(lo, lo + RW_STAGE)], masks)
        y += ys
        h_new += hs
    for g in gr:
        h_scr[g] = h_new[g]

    ysum = _head_sums(y, seg)
    dlt = [y[g] - ysum[g] * (1.0 / HEAD) for g in gr]
    dsum = _head_sums([dlt[g] * dlt[g] for g in gr], seg)
    for g in gr:
        rstd = lax.rsqrt(dsum[g] * (1.0 / HEAD) + GN_EPS)
        yn = dlt[g] * rstd * gng_ref[:, gsl[g]] + gnb_ref[:, gsl[g]] + bonus[g] * v[g]
        o_ref[:, gsl[g]] = (yn * jax.nn.silu(lerp(3 * D, g))).astype(BF16)


def _rwkv_mix(p, p_lora, mu, mu_l, w0, a0, w_lora, a_lora, k_k, k_a, r_k, gn_g, gn_b, side=None):
    n_tok = p.shape[0]
    c = RW_CHUNK
    nc = SEQ // c
    ncol = p.shape[1]
    nl = p_lora.shape[1]
    row = lambda b, ci: (b * nc + ci, 0)
    prev = lambda b, ci: (jnp.maximum((b * nc + ci) * (c // 8) - 1, 0), 0)
    vec = pl.BlockSpec((1, D), lambda b, ci: (0, 0))
    lora = pl.BlockSpec((RW_LP // 2, D), lambda b, ci: (0, 0))
    in_specs = [pl.BlockSpec((c, ncol), row),
                pl.BlockSpec((8, ncol), prev),
                pl.BlockSpec((c, nl), row),
                pl.BlockSpec((8, nl), prev),
                pl.BlockSpec((1, ncol), lambda b, ci: (0, 0)),
                pl.BlockSpec((1, nl), lambda b, ci: (0, 0)),
                vec, vec, lora, lora,
                vec, vec, vec, vec, vec]
    args = [p, p, p_lora, p_lora, mu, mu_l, w0, a0, w_lora, a_lora, k_k, k_a, r_k, gn_g, gn_b]
    out_shape = jax.ShapeDtypeStruct((n_tok, D), BF16)
    out_specs = pl.BlockSpec((c, D), row)
    if side is not None:
        side_w, sl = side
        steps = (n_tok // SEQ) * nc
        _, srows, scols = side_w.shape
        srow = srows // steps
        assert srows % steps == 0 and srow % 16 == 0
        in_specs.append(pl.BlockSpec((None, srow, scols), lambda b, ci: (sl, b * nc + ci, 0)))
        args.append(side_w)
        out_shape = (out_shape, jax.ShapeDtypeStruct((srows, scols), BF16))
        out_specs = (out_specs, pl.BlockSpec((srow, scols), row))
    return pl.pallas_call(
        _rwkv_mix_kernel,
        out_shape=out_shape,
        grid=(n_tok // SEQ, nc),
        in_specs=in_specs,
        out_specs=out_specs,
        scratch_shapes=[pltpu.VMEM((RW_GROUPS, HEAD, RW_GW), F32)],
        compiler_params=_cparams("parallel", "arbitrary"),
        name="rwkv_mix",
    )(*args)


def kernel(x, c, positions, norm_g, mod_w, mod_b, final_norm_g, sg_w_in, sg_w_out, sg_ln_g, sg_ln_b, sg_w_spatial,
           sg_b_spatial, swa_w_in, swa_w_out, swa_sinks, rwkv_w_in, rwkv_w_out, rwkv_mu, rwkv_w0, rwkv_w_lora,
           rwkv_a0, rwkv_a_lora, rwkv_k_k, rwkv_k_a, rwkv_r_k, rwkv_gn_g, rwkv_gn_b):
    batch, seq, d = x.shape
    assert (seq, d) == (SEQ, D) and norm_g.shape[0] == DEPTH
    n_tok = batch * seq
    x2 = x.reshape(n_tok, D)
    c8 = jnp.zeros((8, D), F32).at[:batch].set(c)
    pos2 = positions.reshape(n_tok, 1)
    half = HEAD // 2
    inv_freq = ROPE_THETA ** (-jnp.arange(half, dtype=F32) / half)
    invf = jnp.tile(inv_freq, LANES // half).reshape(1, LANES)
    mod, cos_t, sin_t = _modulation(c8, mod_w, mod_b, pos2, invf)
    row = lambda a: a.reshape(1, -1)

    def act_rows(spans):
        rows = [jnp.concatenate([jnp.full((w,), val[t], F32) for w, *val in spans]) for t in range(3)]
        return jnp.zeros((8, rows[0].shape[0]), F32).at[:3].set(jnp.stack(rows))

    gelu_span = (GELU_A, GELU_B, 1.0)
    silu_span = (0.5, 0.0, 1.0)
    sg_first_bf, rwkv_in_bf = sg_w_in[:1].astype(BF16), rwkv_w_in.astype(BF16)
    w_out = (sg_w_out, swa_w_out, rwkv_w_out)
    side_bf = None
    for i in range(DEPTH):
        kind, j = i % 3, i // 3
        mod3 = mod[i, :batch].reshape(batch, 1, 3 * D)
        g = row(norm_g[i])
        has_next = i + 1 < DEPTH
        if kind == 0:
            act = act_rows([(2 * D, *gelu_span), (D, *silu_span)])
            w = sg_first_bf if i == 0 else side_bf[None]
            p = _in_proj_act(x2, g, mod3, w, 0, 3 * D, act, False, BF16, (swa_w_in, j) if has_next else None)
            p, side_bf = p if has_next else (p, None)
            mix = _sg_mix(p, row(sg_ln_g[j]), row(sg_ln_b[j]), sg_w_spatial[j],
                          jnp.broadcast_to(sg_b_spatial[j][:, :, None], (SG_GROUPS, SG_CHUNK, LANES)))
        elif kind == 1:
            act = act_rows([(D + 2 * SWA_KVW, 0.0, 0.0, 0.0), (D, *silu_span)])
            p = _in_proj_act(x2, g, mod3, side_bf[None], 0, 2 * D + 2 * SWA_KVW, act, True, F32)
            mix = _swa_mix(p, cos_t, sin_t, swa_sinks[j])
        else:
            lp = RW_LP // 2 - RW_LORA
            w_in = rwkv_in_bf[j]
            w_l = jnp.concatenate([jnp.pad(w_in[:, 4 * D:4 * D + RW_LORA], ((0, 0), (0, lp))),
                                   jnp.pad(w_in[:, 4 * D + RW_LORA:], ((0, 0), (0, lp)))], axis=1)
            p, p_lora = _in_proj(x2, g, mod3, rwkv_in_bf, j, 4 * D, w_l)
            mu = rwkv_mu[j]
            mu_l = jnp.concatenate([jnp.pad(mu[4 * D:4 * D + RW_LORA], (0, lp)), jnp.pad(mu[4 * D + RW_LORA:], (0, lp))])
            wl_pad = jnp.pad(rwkv_w_lora[j], ((0, lp), (0, 0))).astype(BF16)
            al_pad = jnp.pad(rwkv_a_lora[j], ((0, lp), (0, 0))).astype(BF16)
            mix = _rwkv_mix(p, p_lora, row(mu[:4 * D]), row(mu_l), row(rwkv_w0[j]),
                            row(rwkv_a0[j]), wl_pad, al_pad, row(rwkv_k_k[j]),
                            row(rwkv_k_a[j]), row(rwkv_r_k[j]), row(rwkv_gn_g[j]), row(rwkv_gn_b[j]),
                            (sg_w_in, j + 1) if has_next else None)
            mix, side_bf = mix if has_next else (mix, None)
        x2 = _out_proj(mix, w_out[kind], j, x2, mod3, row(final_norm_g) if i == DEPTH - 1 else None)
    return x2.reshape(batch, seq, D)
```

```python
import functools
import math

import jax
import jax.numpy as jnp
from jax import lax
from jax.experimental import pallas as pl
from jax.experimental.pallas import tpu as pltpu

F32 = jnp.float32
BF16 = jnp.bfloat16

D = 2048
SEQ = 4096
DEPTH = 4
HEAD = 64
LANES = 128
SG_CHUNK = 128
SG_GROUPS = 16
SWA_HEADS = D // HEAD
SWA_KV = SWA_HEADS // 8
SWA_KVW = SWA_KV * HEAD
SWA_BLOCK = 128
SWA_STAGE = 16
ROPE_THETA = 10000.0
RW_HEADS = D // HEAD
RW_LORA = 96
RW_LP = 256
RW_CHUNK = 64
RW_HG = 4
RW_GW = RW_HG * HEAD
RW_GROUPS = D // RW_GW
RW_STAGE = 8
DECAY_SCALE = math.exp(-0.5)
GN_EPS = 64e-5
RMS_EPS = 1e-6
LN_EPS = 1e-5

VMEM_LIMIT = 48 * 1024 * 1024


def _cparams(*sem, vmem=VMEM_LIMIT):
    return pltpu.CompilerParams(dimension_semantics=sem, vmem_limit_bytes=vmem)


def _mod_kernel(c_ref, w_ref, b_ref, pos_ref, invf_ref, o_ref, cos_ref, sin_ref):
    cond = jax.nn.silu(c_ref[...]).astype(BF16)
    o_ref[0] = jnp.dot(cond, w_ref[0].astype(BF16), preferred_element_type=F32) + b_ref[0]
    _rope_table_kernel(pos_ref, invf_ref, cos_ref, sin_ref)


def _modulation(c8, mod_w, mod_b, pos2, invf):
    tn = 768
    n = mod_w.shape[2]
    nj = n // tn
    n_tok = pos2.shape[0]
    rows = n_tok // (DEPTH * nj)
    assert n % tn == 0 and rows * DEPTH * nj == n_tok and rows % 8 == 0
    table = pl.BlockSpec((rows, LANES), lambda l, j: (l * nj + j, 0))
    return pl.pallas_call(
        _mod_kernel,
        out_shape=(jax.ShapeDtypeStruct((DEPTH, 8, n), F32),
                   jax.ShapeDtypeStruct((n_tok, LANES), F32), jax.ShapeDtypeStruct((n_tok, LANES), F32)),
        grid=(DEPTH, nj),
        in_specs=[pl.BlockSpec((8, D), lambda l, j: (0, 0)),
                  pl.BlockSpec((1, D, tn), lambda l, j: (l, 0, j)),
                  pl.BlockSpec((1, 1, tn), lambda l, j: (l, 0, j)),
                  pl.BlockSpec((rows, 1), lambda l, j: (l * nj + j, 0)),
                  pl.BlockSpec((1, LANES), lambda l, j: (0, 0))],
        out_specs=(pl.BlockSpec((1, 8, tn), lambda l, j: (l, 0, j)), table, table),
        compiler_params=_cparams("parallel", "parallel"),
        name="modulation",
    )(c8, mod_w, mod_b.reshape(DEPTH, 1, n), pos2, invf)


NORM_ROWS = 128


def _row_rsqrt(x_ref, rs_scr):
    def body(i, carry):
        rows = pl.ds(pl.multiple_of(i * NORM_ROWS, NORM_ROWS), NORM_ROWS)
        x = x_ref[rows, :]
        rs_scr[rows, :] = lax.rsqrt(jnp.mean(x * x, axis=-1, keepdims=True) + RMS_EPS)
        return carry
    lax.fori_loop(0, x_ref.shape[0] // NORM_ROWS, body, 0, unroll=True)


def _modulated(x_ref, rs_scr, g_ref, sc_ref, sh_ref):
    gain = g_ref[...] * (1.0 + sc_ref[0])
    return (x_ref[...] * rs_scr[...] * gain + sh_ref[0]).astype(BF16)


GELU_A = math.sqrt(2.0 / math.pi)
GELU_B = GELU_A * 0.044715
ACT_COLS = 256


def _in_proj_act_kernel(x_ref, g_ref, sc_ref, sh_ref, w_ref, act_ref, *rest, select):
    if len(rest) == 4:
        side_ref, o_ref, side_out_ref, rs_scr = rest
        side_out_ref[...] = side_ref[...].astype(BF16)
    else:
        o_ref, rs_scr = rest

    @pl.when(pl.program_id(1) == 0)
    def _():
        _row_rsqrt(x_ref, rs_scr)

    h = _modulated(x_ref, rs_scr, g_ref, sc_ref, sh_ref)
    for n in range(o_ref.shape[1] // ACT_COLS):
        cols = pl.ds(n * ACT_COLS, ACT_COLS)
        p = jnp.dot(h, w_ref[:, cols], preferred_element_type=F32)
        gated = (0.5 * p) * (1.0 + jnp.tanh(p * (act_ref[0:1, cols] + act_ref[1:2, cols] * (p * p))))
        o_ref[:, cols] = (jnp.where(act_ref[2:3, cols] > 0.0, gated, p) if select else gated).astype(o_ref.dtype)


def _in_proj_lora_kernel(x_ref, g_ref, sc_ref, sh_ref, w_ref, wl_ref, o_ref, ol_ref, rs_scr):
    @pl.when(pl.program_id(1) == 0)
    def _():
        _row_rsqrt(x_ref, rs_scr)
        ol_ref[...] = jnp.dot(_modulated(x_ref, rs_scr, g_ref, sc_ref, sh_ref), wl_ref[...],
                              preferred_element_type=F32)

    h = _modulated(x_ref, rs_scr, g_ref, sc_ref, sh_ref)
    o_ref[...] = jnp.dot(h, w_ref[...], preferred_element_type=F32)


SIDE_ROWS = 128


def _in_proj_act(x2, g, mod3, w_bf, wl, ncols, act, select, out_dtype, side=None, *, tm=1024):
    n_tok = x2.shape[0]
    tn = 2048 if (out_dtype == BF16 and ncols % 2048 == 0) else (1024 if ncols % 1024 == 0 else 1536)
    assert ncols % tn == 0
    per_b = SEQ // tm
    nj = ncols // tn
    in_specs = [pl.BlockSpec((tm, D), lambda i, j: (i, 0)),
                pl.BlockSpec((1, D), lambda i, j: (0, 0)),
                pl.BlockSpec((1, 1, D), lambda i, j: (i // per_b, 0, 1)),
                pl.BlockSpec((1, 1, D), lambda i, j: (i // per_b, 0, 0)),
                pl.BlockSpec((None, D, tn), lambda i, j: (wl, 0, j)),
                pl.BlockSpec((8, tn), lambda i, j: (0, j))]
    out_shape = jax.ShapeDtypeStruct((n_tok, ncols), out_dtype)
    out_specs = pl.BlockSpec((tm, tn), lambda i, j: (i, j))
    args = [x2, g, mod3, mod3, w_bf, act]
    osize = jnp.dtype(out_dtype).itemsize
    vmem = 2 * (4 * tm * D + 2 * D * tn + osize * tm * tn) + 8 * 1024 * 1024
    if side is not None:
        side_w, sl = side
        _, srows, scols = side_w.shape
        nblk = srows // SIDE_ROWS
        assert srows % SIDE_ROWS == 0 and nblk <= (n_tok // tm) * nj
        blk = lambda i, j: jnp.minimum(i * nj + j, nblk - 1)
        in_specs.append(pl.BlockSpec((None, SIDE_ROWS, scols), lambda i, j: (sl, blk(i, j), 0)))
        args.append(side_w)
        out_shape = (out_shape, jax.ShapeDtypeStruct((srows, scols), BF16))
        out_specs = (out_specs, pl.BlockSpec((SIDE_ROWS, scols), lambda i, j: (blk(i, j), 0)))
        vmem += 2 * (4 + 2) * SIDE_ROWS * scols
    return pl.pallas_call(
        functools.partial(_in_proj_act_kernel, select=select),
        out_shape=out_shape,
        grid=(n_tok // tm, nj),
        in_specs=in_specs,
        out_specs=out_specs,
        scratch_shapes=[pltpu.VMEM((tm, 1), F32)],
        compiler_params=_cparams("arbitrary", "arbitrary", vmem=vmem),
        name="in_proj_act",
    )(*args)


def _in_proj(x2, g, mod3, w_bf, wl, ncols, w_lora_bf, *, tm=1024):
    n_tok = x2.shape[0]
    tn = 2048
    assert ncols % tn == 0
    per_b = SEQ // tm
    x_spec = pl.BlockSpec((tm, D), lambda i, j: (i, 0))
    g_spec = pl.BlockSpec((1, D), lambda i, j: (0, 0))
    sh_spec = pl.BlockSpec((1, 1, D), lambda i, j: (i // per_b, 0, 0))
    sc_spec = pl.BlockSpec((1, 1, D), lambda i, j: (i // per_b, 0, 1))
    w_spec = pl.BlockSpec((None, D, tn), lambda i, j: (wl, 0, j))
    o_spec = pl.BlockSpec((tm, tn), lambda i, j: (i, j))
    grid = (n_tok // tm, ncols // tn)
    scratch = [pltpu.VMEM((tm, 1), F32)]
    nl = w_lora_bf.shape[1]
    vmem = 2 * (4 * tm * D + 2 * D * tn + 4 * tm * tn + 2 * D * nl + 4 * tm * nl) + 4 * 1024 * 1024
    return pl.pallas_call(
        _in_proj_lora_kernel,
        out_shape=(jax.ShapeDtypeStruct((n_tok, ncols), F32), jax.ShapeDtypeStruct((n_tok, nl), F32)),
        grid=grid,
        in_specs=[x_spec, g_spec, sc_spec, sh_spec, w_spec, pl.BlockSpec((D, nl), lambda i, j: (0, 0))],
        out_specs=(o_spec, pl.BlockSpec((tm, nl), lambda i, j: (i, 0))),
        scratch_shapes=scratch,
        compiler_params=_cparams("parallel", "arbitrary", vmem=vmem),
        name="in_proj_lora",
    )(x2, g, mod3, mod3, w_bf, w_lora_bf)


CAST_ROWS = 256


def _round_weight(w_ref, w_scr):
    @pl.when(pl.program_id(0) == 0)
    def _():
        def body(i, carry):
            rows = pl.ds(pl.multiple_of(i * CAST_ROWS, CAST_ROWS), CAST_ROWS)
            w_scr[rows, :] = w_ref[rows, :].astype(BF16)
            return carry
        lax.fori_loop(0, w_ref.shape[0] // CAST_ROWS, body, 0, unroll=True)


def _out_proj_kernel(m_ref, w_ref, x_ref, gate_ref, o_ref, w_scr):
    _round_weight(w_ref, w_scr)
    y = jnp.dot(m_ref[...], w_scr[...], preferred_element_type=F32)
    o_ref[...] = x_ref[...] + gate_ref[0] * y


def _out_proj_final_kernel(m_ref, w_ref, x_ref, gate_ref, fg_ref, o_ref, w_scr):
    _round_weight(w_ref, w_scr)
    y = jnp.dot(m_ref[...], w_scr[...], preferred_element_type=F32)
    xn = x_ref[...] + gate_ref[0] * y
    o_ref[...] = xn * lax.rsqrt(jnp.mean(xn * xn, axis=-1, keepdims=True) + RMS_EPS) * fg_ref[...]


def _out_proj(mix, w, wl, x2, mod3, final_g=None, *, tm=512):
    n_tok = x2.shape[0]
    per_b = SEQ // tm
    in_specs = [pl.BlockSpec((tm, D), lambda i: (i, 0)),
                pl.BlockSpec((None, D, D), lambda i: (wl, 0, 0), pipeline_mode=pl.Buffered(1)),
                pl.BlockSpec((tm, D), lambda i: (i, 0)),
                pl.BlockSpec((1, 1, D), lambda i: (i // per_b, 0, 2))]
    args = [mix, w, x2, mod3]
    kern = _out_proj_kernel
    if final_g is not None:
        in_specs.append(pl.BlockSpec((1, D), lambda i: (0, 0)))
        args.append(final_g)
        kern = _out_proj_final_kernel
    return pl.pallas_call(
        kern,
        out_shape=jax.ShapeDtypeStruct((n_tok, D), F32),
        grid=(n_tok // tm,),
        in_specs=in_specs,
        out_specs=pl.BlockSpec((tm, D), lambda i: (i, 0)),
        scratch_shapes=[pltpu.VMEM((D, D), BF16)],
        compiler_params=_cparams("arbitrary", vmem=6 * D * D + 2 * (2 + 4 + 4) * tm * D + 8 * 1024 * 1024),
        name="out_proj_final" if final_g is not None else "out_proj",
    )(*args)


def _sg_mix_kernel(u_ref, v_ref, z_ref, lng_ref, lnb_ref, ws_ref, bs_ref, o_ref, *, tm):
    ti = lax.broadcasted_iota(jnp.int32, (SG_CHUNK, SG_CHUNK), 0)
    si = lax.broadcasted_iota(jnp.int32, (SG_CHUNK, SG_CHUNK), 1)
    causal = ti >= si
    ws = [jnp.where(causal, ws_ref[g], 0.0).astype(BF16) for g in range(SG_GROUPS)]
    for c in range(tm // SG_CHUNK):
        rows = pl.ds(c * SG_CHUNK, SG_CHUNK)
        v = v_ref[rows, :].astype(F32)
        mean = jnp.mean(v, axis=-1, keepdims=True)
        d = v - mean
        var = jnp.mean(d * d, axis=-1, keepdims=True)
        vn = (d * lax.rsqrt(var + LN_EPS) * lng_ref[...] + lnb_ref[...]).astype(BF16)
        f = [jnp.dot(ws[g], vn[:, g * LANES:(g + 1) * LANES], preferred_element_type=F32) for g in range(SG_GROUPS)]
        for g in range(SG_GROUPS):
            cols = pl.ds(g * LANES, LANES)
            uz = u_ref[rows, cols].astype(F32) * z_ref[rows, cols].astype(F32)
            o_ref[rows, cols] = (uz * (f[g] + bs_ref[g])).astype(BF16)


def _sg_out_kernel(u_ref, v_ref, z_ref, lng_ref, lnb_ref, ws_ref, bs_ref, w_ref, x_ref, gate_ref, *rest, tm, final):
    if final:
        fg_ref, o_ref, w_scr, mix_scr = rest
    else:
        o_ref, w_scr, mix_scr = rest
    _round_weight(w_ref, w_scr)
    _sg_mix_kernel(u_ref, v_ref, z_ref, lng_ref, lnb_ref, ws_ref, bs_ref, mix_scr, tm=tm)
    y = jnp.dot(mix_scr[...], w_scr[...], preferred_element_type=F32)
    xn = x_ref[...] + gate_ref[0] * y
    if final:
        xn = xn * lax.rsqrt(jnp.mean(xn * xn, axis=-1, keepdims=True) + RMS_EPS) * fg_ref[...]
    o_ref[...] = xn


def _sg_out(p, ln_g, ln_b, w_s, b_full, w, wl, x2, mod3, final_g=None, *, tm=256):
    n_tok = p.shape[0]
    per_b = SEQ // tm
    tile = lambda c: pl.BlockSpec((tm, D), lambda i: (i, c))
    vec = pl.BlockSpec((1, D), lambda i: (0, 0))
    in_specs = [tile(0), tile(1), tile(2), vec, vec,
                pl.BlockSpec((SG_GROUPS, SG_CHUNK, SG_CHUNK), lambda i: (0, 0, 0)),
                pl.BlockSpec((SG_GROUPS, SG_CHUNK, LANES), lambda i: (0, 0, 0)),
                pl.BlockSpec((None, D, D), lambda i: (wl, 0, 0), pipeline_mode=pl.Buffered(1)),
                tile(0),
                pl.BlockSpec((1, 1, D), lambda i: (i // per_b, 0, 2))]
    args = [p, p, p, ln_g, ln_b, w_s, b_full, w, x2, mod3]
    if final_g is not None:
        in_specs.append(vec)
        args.append(final_g)
    vmem = (6 * D * D + 2 * tm * D + 2 * (3 * 2 + 4 + 4) * tm * D
            + 2 * 4 * SG_GROUPS * SG_CHUNK * (SG_CHUNK + LANES) + 8 * 1024 * 1024)
    return pl.pallas_call(
        functools.partial(_sg_out_kernel, tm=tm, final=final_g is not None),
        out_shape=jax.ShapeDtypeStruct((n_tok, D), F32),
        grid=(n_tok // tm,),
        in_specs=in_specs,
        out_specs=pl.BlockSpec((tm, D), lambda i: (i, 0)),
        scratch_shapes=[pltpu.VMEM((D, D), BF16), pltpu.VMEM((tm, D), BF16)],
        compiler_params=_cparams("arbitrary", vmem=vmem),
        name="sg_out_final" if final_g is not None else "sg_out",
    )(*args)


def _first_half(shape):
    return (lax.broadcasted_iota(jnp.int32, shape, 1) % HEAD) < (HEAD // 2)


def _rope_table_kernel(pos_ref, invf_ref, cos_ref, sin_ref):
    ang = pos_ref[...].astype(F32) * invf_ref[...]
    cos_ref[...] = jnp.cos(ang)
    sin_ref[...] = jnp.where(_first_half(ang.shape), -jnp.sin(ang), jnp.sin(ang))


def _rope(x, cos, sin_signed, first):
    partner = jnp.where(first, pltpu.roll(x, LANES - HEAD // 2, 1), pltpu.roll(x, HEAD // 2, 1))
    return x * cos + partner * sin_signed


def _swa_mix_kernel(sink_ref, p_ref, kvp_ref, cos_ref, sin_ref, cosp_ref, sinp_ref, o_ref):
    i = pl.program_id(0)
    first_key = jnp.where((i % (SEQ // SWA_BLOCK)) != 0, 0, SWA_BLOCK)
    cos, sin_s, cosp, sinp_s = cos_ref[...], sin_ref[...], cosp_ref[...], sinp_ref[...]
    first = _first_half(cos.shape)
    lane = lax.broadcasted_iota(jnp.int32, (2 * SWA_BLOCK, LANES), 1)
    low = lane < HEAD

    qi = lax.broadcasted_iota(jnp.int32, (SWA_BLOCK, 2 * SWA_BLOCK), 0)
    kj = lax.broadcasted_iota(jnp.int32, (SWA_BLOCK, 2 * SWA_BLOCK), 1)
    rel = qi + SWA_BLOCK - kj
    mask = (rel >= 0) & (rel < SWA_BLOCK) & (kj >= first_key)

    k_lo, k_hi, v_lo, v_hi = [], [], [], []
    for kp in range(SWA_KV // 2):
        ksl = pl.ds(kp * LANES, LANES)
        vsl = pl.ds(SWA_KVW + kp * LANES, LANES)
        k2 = jnp.concatenate([_rope(kvp_ref[:, ksl], cosp, sinp_s, first),
                              _rope(p_ref[:, pl.ds(D + kp * LANES, LANES)], cos, sin_s, first)], axis=0)
        v2 = jnp.concatenate([kvp_ref[:, vsl], p_ref[:, pl.ds(D + SWA_KVW + kp * LANES, LANES)]], axis=0)
        k2r = pltpu.roll(k2, HEAD, 1)
        v2r = pltpu.roll(v2, HEAD, 1)
        for sub in range(2):
            src_k, src_kr = (k2, k2r) if sub == 0 else (k2r, k2)
            src_v, src_vr = (v2, v2r) if sub == 0 else (v2r, v2)
            k_lo.append(jnp.where(low, src_k, 0.0).astype(BF16))
            k_hi.append(jnp.where(low, 0.0, src_kr).astype(BF16))
            v_lo.append(jnp.where(low, src_v, 0.0).astype(BF16))
            v_hi.append(jnp.where(low, 0.0, src_vr).astype(BF16))

    zoff = D + 2 * SWA_KVW
    for h0 in range(0, SWA_HEADS, SWA_STAGE):
        hs = range(h0, h0 + SWA_STAGE)
        prs = range(h0 // 2, (h0 + SWA_STAGE) // 2)
        cols = {pr: pl.ds(pr * LANES, LANES) for pr in prs}
        q2 = {pr: (_rope(p_ref[:, cols[pr]], cos, sin_s, first) * (HEAD ** -0.5)).astype(BF16) for pr in prs}
        s = {h: _dot_nt(q2[h // 2], (k_hi if h % 2 else k_lo)[h // 8]) for h in hs}
        s = {h: jnp.where(mask, s[h], -jnp.inf) for h in hs}
        m = {h: jnp.maximum(jnp.max(s[h], axis=-1, keepdims=True), sink_ref[h]) for h in hs}
        e = {h: jnp.exp(s[h] - m[h]) for h in hs}
        denom = {h: jnp.sum(e[h], axis=-1, keepdims=True) + jnp.exp(sink_ref[h] - m[h]) for h in hs}
        prob = {h: (e[h] * (1.0 / denom[h])).astype(BF16) for h in hs}
        o = {h: jnp.dot(prob[h], (v_hi if h % 2 else v_lo)[h // 8], preferred_element_type=F32) for h in hs}
        for pr in prs:
            gate = p_ref[:, pl.ds(zoff + pr * LANES, LANES)]
            o_ref[:, cols[pr]] = ((o[2 * pr] + o[2 * pr + 1]) * gate).astype(BF16)


def _swa_mix(p, cos_t, sin_t, sinks):
    n_tok, ncols = p.shape
    tb = SWA_BLOCK
    kvb = D // (2 * SWA_KVW)
    cur = pl.BlockSpec((tb, LANES), lambda i: (i, 0))
    prev = pl.BlockSpec((tb, LANES), lambda i: (jnp.maximum(i - 1, 0), 0))
    return pl.pallas_call(
        _swa_mix_kernel,
        out_shape=jax.ShapeDtypeStruct((n_tok, D), BF16),
        grid=(n_tok // tb,),
        in_specs=[pl.BlockSpec(memory_space=pltpu.SMEM),
                  pl.BlockSpec((tb, ncols), lambda i: (i, 0)),
                  pl.BlockSpec((tb, 2 * SWA_KVW), lambda i: (jnp.maximum(i - 1, 0), kvb)),
                  cur, cur, prev, prev],
        out_specs=pl.BlockSpec((tb, D), lambda i: (i, 0)),
        compiler_params=_cparams("parallel"),
        name="swa_mix",
    )(sinks, p, p, cos_t, sin_t, cos_t, sin_t)


def _head_sums(xs, seg):
    out = jnp.dot(jnp.concatenate([x.astype(BF16) for x in xs], axis=0), seg, preferred_element_type=F32)
    n = xs[0].shape[0]
    return [out[i * n:(i + 1) * n] for i in range(len(xs))]


def _cumsum_rows(x, tri):
    hi = x.astype(BF16)
    lo = (x - hi.astype(F32)).astype(BF16)
    return jnp.dot(tri, hi, preferred_element_type=F32) + jnp.dot(tri, lo, preferred_element_type=F32)


def _bd(x, same_head):
    t = jnp.concatenate([x] * RW_HG, axis=0)
    return jnp.where(same_head, t, 0.0).astype(BF16)


def _diag_blocks(o, lane_head):
    acc = o[0:HEAD]
    for h in range(1, RW_HG):
        acc = jnp.where(lane_head == h, o[h * HEAD:(h + 1) * HEAD], acc)
    return acc


def _dot(a, b):
    return jnp.dot(a.astype(BF16), b.astype(BF16), preferred_element_type=F32)


def _dot_nt(a, b):
    return lax.dot_general(a, b, (((1,), (1,)), ((), ())), preferred_element_type=F32)


def _dot_tn(a, b):
    return lax.dot_general(a, b, (((0,), (0,)), ((), ())), preferred_element_type=F32)


def _rwkv_groups(r, lw, cum, k, v, kap, bet, h0, masks):
    incl, strict, eye, same_head, lane_head = masks
    c = RW_CHUNK
    gr = range(len(r))
    bd = lambda t: _bd(t, same_head)
    g_t = [jnp.exp(cum[g]) for g in gr]
    g_inv = [jnp.exp(-cum[g]) for g in gr]
    cref = [cum[g][c // 2 - 1:c // 2] for g in gr]
    cend = [cum[g][c - 1:c] for g in gr]
    to_ref = [jnp.exp(-cref[g]) for g in gr]
    r_true = [r[g] * g_t[g] for g in gr]
    kap_true = [kap[g] * (g_t[g] * jnp.exp(-lw[g])) for g in gr]
    r_s = [r_true[g] * to_ref[g] for g in gr]
    kap_s = [kap_true[g] * to_ref[g] for g in gr]
    ginv = [g_inv[g] * jnp.exp(cref[g]) for g in gr]
    eend = [g_inv[g] * jnp.exp(cend[g]) for g in gr]

    s = [_dot_nt(jnp.concatenate([r_s[g], kap_s[g]], axis=0).astype(BF16),
                 jnp.concatenate([bd(k[g] * ginv[g]), bd(bet[g] * ginv[g])], axis=0)) for g in gr]
    a_rk = [jnp.where(incl, s[g][:c, :RW_GW], 0.0) for g in gr]
    a_rb = [jnp.where(incl, s[g][:c, RW_GW:], 0.0) for g in gr]
    a_kk = [jnp.where(strict, s[g][c:, :RW_GW], 0.0) for g in gr]
    a_kb = [jnp.where(strict, s[g][c:, RW_GW:], 0.0) for g in gr]

    x = [eye - a_kb[g] for g in gr]
    p = [_dot(a_kb[g], bd(a_kb[g])) for g in gr]
    akv = [_dot(a_kk[g], bd(v[g])) for g in gr]
    for _ in range(4):
        xp = [_dot(jnp.concatenate([x[g], p[g]], axis=0), bd(p[g])) for g in gr]
        x = [x[g] + xp[g][:c] for g in gr]
        p = [xp[g][c:] for g in gr]
    t_inv = [x[g] + _dot(x[g], bd(p[g])) for g in gr]

    wu = [-_dot(t_inv[g], jnp.concatenate([bd(kap_true[g]), bd(akv[g])], axis=1)) for g in gr]

    hw = [_dot(jnp.concatenate([wu[g][:, :RW_GW], r_true[g], eye * jnp.exp(cend[g])], axis=0), bd(h0[g]))
          for g in gr]
    u = [hw[g][:c] + wu[g][:, RW_GW:] for g in gr]
    y = [hw[g][c:2 * c] + _dot(jnp.concatenate([a_rk[g], a_rb[g]], axis=1),
                               jnp.concatenate([bd(v[g]), bd(u[g])], axis=0)) for g in gr]
    o2 = [_dot_tn(jnp.concatenate([k[g] * eend[g], bet[g] * eend[g]], axis=0).astype(BF16),
                  jnp.concatenate([v[g], u[g]], axis=0).astype(BF16)) for g in gr]
    h_new = [hw[g][2 * c:] + _diag_blocks(o2[g], lane_head) for g in gr]
    return y, h_new


def _rwkv_mix_kernel(p_ref, pp_ref, pl_ref, plp_ref, mu_ref, mul_ref, w0_ref, a0_ref, wl_ref, al_ref,
                     kk_ref, ka_ref, rk_ref, gng_ref, gnb_ref, *rest):
    if len(rest) == 4:
        side_ref, o_ref, side_out_ref, h_scr = rest
        side_out_ref[...] = side_ref[...].astype(BF16)
    else:
        o_ref, h_scr = rest
    ci = pl.program_id(1)
    c = RW_CHUNK

    @pl.when(ci == 0)
    def _():
        h_scr[...] = jnp.zeros_like(h_scr)

    not_first = (ci != 0).astype(F32)
    row = lax.broadcasted_iota(jnp.int32, (c, 1), 0)

    def shifted(cur, prev_ref):
        prev_row = prev_ref[7:8, :] * not_first
        rolled = pltpu.roll(cur, 1, 0)
        return jnp.where(row == 0, prev_row, rolled)

    lcur = pl_ref[...]
    lcur = lcur + (shifted(lcur, plp_ref) - lcur) * mul_ref[...]
    dw = jnp.tanh(lcur[:, :RW_LP // 2])
    da = lcur[:, RW_LP // 2:]

    ti = lax.broadcasted_iota(jnp.int32, (c, RW_GW), 0)
    li = lax.broadcasted_iota(jnp.int32, (c, RW_GW), 1)
    si = li % HEAD
    lane_head = li // HEAD
    rb = lax.broadcasted_iota(jnp.int32, (RW_GW, RW_GW), 0) // HEAD
    cb = lax.broadcasted_iota(jnp.int32, (RW_GW, RW_GW), 1) // HEAD
    same_head = rb == cb
    seg = same_head.astype(BF16)
    masks = (ti >= si, ti > si, (ti == si).astype(F32), same_head, lane_head)

    gr = range(RW_GROUPS)
    gsl = [pl.ds(g * RW_GW, RW_GW) for g in gr]

    def lerp(off, g):
        cols = pl.ds(off + g * RW_GW, RW_GW)
        cur = p_ref[:, cols]
        return cur + (shifted(cur, pp_ref.at[:, cols]) - cur) * mu_ref[:, cols]

    r = [lerp(0, g) for g in gr]
    k = [lerp(D, g) for g in gr]
    v = [lerp(2 * D, g) for g in gr]
    lw = [-DECAY_SCALE * jax.nn.sigmoid(w0_ref[:, gsl[g]] + _dot(dw, wl_ref[:, gsl[g]])) for g in gr]
    a = [jax.nn.sigmoid(a0_ref[:, gsl[g]] + _dot(da, al_ref[:, gsl[g]])) for g in gr]
    kap = [k[g] * kk_ref[:, gsl[g]] for g in gr]
    k = [k[g] * (1.0 + (a[g] - 1.0) * ka_ref[:, gsl[g]]) for g in gr]
    sums = _head_sums([kap[g] * kap[g] for g in gr] + [r[g] * k[g] * rk_ref[:, gsl[g]] for g in gr], seg)
    kap = [kap[g] / jnp.maximum(jnp.sqrt(sums[g]), 1e-12) for g in gr]
    bonus = sums[RW_GROUPS:]
    bet = [kap[g] * a[g] for g in gr]
    tri = (lax.broadcasted_iota(jnp.int32, (c, c), 0) >= lax.broadcasted_iota(jnp.int32, (c, c), 1)).astype(BF16)
    cum = [_cumsum_rows(lw[g], tri) for g in gr]

    y, h_new = [], []
    for lo in range(0, RW_GROUPS, RW_STAGE):
        sl = slice(lo, lo + RW_STAGE)
        ys, hs = _rwkv_groups(r[sl], lw[sl], cum[sl], k[sl], v[sl], kap[sl], bet[sl],
                              [h_scr[g] for g in range(lo, lo + RW_STAGE)], masks)
        y += ys
        h_new += hs
    for g in gr:
        h_scr[g] = h_new[g]

    ysum = _head_sums(y, seg)
    dlt = [y[g] - ysum[g] * (1.0 / HEAD) for g in gr]
    dsum = _head_sums([dlt[g] * dlt[g] for g in gr], seg)
    for g in gr:
        rstd = lax.rsqrt(dsum[g] * (1.0 / HEAD) + GN_EPS)
        yn = dlt[g] * rstd * gng_ref[:, gsl[g]] + gnb_ref[:, gsl[g]] + bonus[g] * v[g]
        o_ref[:, gsl[g]] = (yn * jax.nn.silu(lerp(3 * D, g))).astype(BF16)


def _rwkv_mix(p, p_lora, mu, mu_l, w0, a0, w_lora, a_lora, k_k, k_a, r_k, gn_g, gn_b, side=None):
    n_tok = p.shape[0]
    c = RW_CHUNK
    nc = SEQ // c
    ncol = p.shape[1]
    nl = p_lora.shape[1]
    row = lambda b, ci: (b * nc + ci, 0)
    prev = lambda b, ci: (jnp.maximum((b * nc + ci) * (c // 8) - 1, 0), 0)
    vec = pl.BlockSpec((1, D), lambda b, ci: (0, 0))
    lora = pl.BlockSpec((RW_LP // 2, D), lambda b, ci: (0, 0))
    in_specs = [pl.BlockSpec((c, ncol), row),
                pl.BlockSpec((8, ncol), prev),
                pl.BlockSpec((c, nl), row),
                pl.BlockSpec((8, nl), prev),
                pl.BlockSpec((1, ncol), lambda b, ci: (0, 0)),
                pl.BlockSpec((1, nl), lambda b, ci: (0, 0)),
                vec, vec, lora, lora,
                vec, vec, vec, vec, vec]
    args = [p, p, p_lora, p_lora, mu, mu_l, w0, a0, w_lora, a_lora, k_k, k_a, r_k, gn_g, gn_b]
    out_shape = jax.ShapeDtypeStruct((n_tok, D), BF16)
    out_specs = pl.BlockSpec((c, D), row)
    if side is not None:
        side_w, sl = side
        steps = (n_tok // SEQ) * nc
        _, srows, scols = side_w.shape
        srow = srows // steps
        assert srows % steps == 0 and srow % 16 == 0
        in_specs.append(pl.BlockSpec((None, srow, scols), lambda b, ci: (sl, b * nc + ci, 0)))
        args.append(side_w)
        out_shape = (out_shape, jax.ShapeDtypeStruct((srows, scols), BF16))
        out_specs = (out_specs, pl.BlockSpec((srow, scols), row))
    return pl.pallas_call(
        _rwkv_mix_kernel,
        out_shape=out_shape,
        grid=(n_tok // SEQ, nc),
        in_specs=in_specs,
        out_specs=out_specs,
        scratch_shapes=[pltpu.VMEM((RW_GROUPS, HEAD, RW_GW), F32)],
        compiler_params=_cparams("parallel", "arbitrary"),
        name="rwkv_mix",
    )(*args)


def kernel(x, c, positions, norm_g, mod_w, mod_b, final_norm_g, sg_w_in, sg_w_out, sg_ln_g, sg_ln_b, sg_w_spatial,
           sg_b_spatial, swa_w_in, swa_w_out, swa_sinks, rwkv_w_in, rwkv_w_out, rwkv_mu, rwkv_w0, rwkv_w_lora,
           rwkv_a0, rwkv_a_lora, rwkv_k_k, rwkv_k_a, rwkv_r_k, rwkv_gn_g, rwkv_gn_b):
    batch, seq, d = x.shape
    assert (seq, d) == (SEQ, D) and norm_g.shape[0] == DEPTH
    n_tok = batch * seq
    x2 = x.reshape(n_tok, D)
    c8 = jnp.zeros((8, D), F32).at[:batch].set(c)
    pos2 = positions.reshape(n_tok, 1)
    half = HEAD // 2
    inv_freq = ROPE_THETA ** (-jnp.arange(half, dtype=F32) / half)
    invf = jnp.tile(inv_freq, LANES // half).reshape(1, LANES)
    mod, cos_t, sin_t = _modulation(c8, mod_w, mod_b, pos2, invf)
    row = lambda a: a.reshape(1, -1)

    def act_rows(spans):
        rows = [jnp.concatenate([jnp.full((w,), val[t], F32) for w, *val in spans]) for t in range(3)]
        return jnp.zeros((8, rows[0].shape[0]), F32).at[:3].set(jnp.stack(rows))

    gelu_span = (GELU_A, GELU_B, 1.0)
    silu_span = (0.5, 0.0, 1.0)
    sg_first_bf, rwkv_in_bf = sg_w_in[:1].astype(BF16), rwkv_w_in.astype(BF16)
    w_out = (sg_w_out, swa_w_out, rwkv_w_out)
    side_bf = None
    for i in range(DEPTH):
        kind, j = i % 3, i // 3
        mod3 = mod[i, :batch].reshape(batch, 1, 3 * D)
        g = row(norm_g[i])
        has_next = i + 1 < DEPTH
        final_g = None if has_next else row(final_norm_g)
        if kind == 0:
            act = act_rows([(2 * D, *gelu_span), (D, *silu_span)])
            w = sg_first_bf if i == 0 else side_bf[None]
            p = _in_proj_act(x2, g, mod3, w, 0, 3 * D, act, False, BF16, (swa_w_in, j) if has_next else None)
            p, side_bf = p if has_next else (p, None)
            x2 = _sg_out(p, row(sg_ln_g[j]), row(sg_ln_b[j]), sg_w_spatial[j],
                         jnp.broadcast_to(sg_b_spatial[j][:, :, None], (SG_GROUPS, SG_CHUNK, LANES)),
                         sg_w_out, j, x2, mod3, final_g)
            continue
        elif kind == 1:
            act = act_rows([(D + 2 * SWA_KVW, 0.0, 0.0, 0.0), (D, *silu_span)])
            p = _in_proj_act(x2, g, mod3, side_bf[None], 0, 2 * D + 2 * SWA_KVW, act, True, F32)
            mix = _swa_mix(p, cos_t, sin_t, swa_sinks[j])
        else:
            lp = RW_LP // 2 - RW_LORA
            w_in = rwkv_in_bf[j]
            w_l = jnp.concatenate([jnp.pad(w_in[:, 4 * D:4 * D + RW_LORA], ((0, 0), (0, lp))),
                                   jnp.pad(w_in[:, 4 * D + RW_LORA:], ((0, 0), (0, lp)))], axis=1)
            p, p_lora = _in_proj(x2, g, mod3, rwkv_in_bf, j, 4 * D, w_l)
            mu = rwkv_mu[j]
            mu_l = jnp.concatenate([jnp.pad(mu[4 * D:4 * D + RW_LORA], (0, lp)), jnp.pad(mu[4 * D + RW_LORA:], (0, lp))])
            wl_pad = jnp.pad(rwkv_w_lora[j], ((0, lp), (0, 0))).astype(BF16)
            al_pad = jnp.pad(rwkv_a_lora[j], ((0, lp), (0, 0))).astype(BF16)
            mix = _rwkv_mix(p, p_lora, row(mu[:4 * D]), row(mu_l), row(rwkv_w0[j]),
                            row(rwkv_a0[j]), wl_pad, al_pad, row(rwkv_k_k[j]),
                            row(rwkv_k_a[j]), row(rwkv_r_k[j]), row(rwkv_gn_g[j]), row(rwkv_gn_b[j]),
                            (sg_w_in, j + 1) if has_next else None)
            mix, side_bf = mix if has_next else (mix, None)
        x2 = _out_proj(mix, w_out[kind], j, x2, mod3, final_g)
    return x2.reshape(batch, seq, D)
```

```python
import functools
import math

import jax
import jax.numpy as jnp
from jax import lax
from jax.experimental import pallas as pl
from jax.experimental.pallas import tpu as pltpu

F32 = jnp.float32
BF16 = jnp.bfloat16

D = 2048
SEQ = 4096
DEPTH = 4
HEAD = 64
LANES = 128
SUBLANES = 8
SG_CHUNK = 128
SG_GROUPS = 16
SWA_HEADS = D // HEAD
SWA_KV = SWA_HEADS // 8
SWA_KVW = SWA_KV * HEAD
SWA_BLOCK = 128
SWA_STAGE = 16
ROPE_THETA = 10000.0
RW_LORA = 96
RW_LP = 256
RW_CHUNK = 64
RW_HG = 4
RW_GW = RW_HG * HEAD
RW_GROUPS = D // RW_GW
RW_STAGE = 8
DECAY_SCALE = math.exp(-0.5)
GN_EPS = 64e-5
RMS_EPS = 1e-6
LN_EPS = 1e-5

VMEM_LIMIT = 48 * 1024 * 1024


def _cparams(*sem, vmem=VMEM_LIMIT):
    return pltpu.CompilerParams(dimension_semantics=sem, vmem_limit_bytes=vmem)


def _mod_kernel(c_ref, w_ref, b_ref, pos_ref, invf_ref, o_ref, cos_ref, sin_ref):
    cond = jax.nn.silu(c_ref[...]).astype(BF16)
    o_ref[0] = jnp.dot(cond, w_ref[0].astype(BF16), preferred_element_type=F32) + b_ref[0]
    _rope_table_kernel(pos_ref, invf_ref, cos_ref, sin_ref)


def _modulation(c8, mod_w, mod_b, pos2, invf):
    tn = 1536
    n = mod_w.shape[2]
    nj = n // tn
    n_tok = pos2.shape[0]
    rows = n_tok // (DEPTH * nj)
    assert n % tn == 0 and rows * DEPTH * nj == n_tok and rows % 8 == 0
    table = pl.BlockSpec((rows, LANES), lambda l, j: (l * nj + j, 0))
    return pl.pallas_call(
        _mod_kernel,
        out_shape=(jax.ShapeDtypeStruct((DEPTH, SUBLANES, n), F32),
                   jax.ShapeDtypeStruct((n_tok, LANES), F32), jax.ShapeDtypeStruct((n_tok, LANES), F32)),
        grid=(DEPTH, nj),
        in_specs=[pl.BlockSpec((SUBLANES, D), lambda l, j: (0, 0)),
                  pl.BlockSpec((1, D, tn), lambda l, j: (l, 0, j)),
                  pl.BlockSpec((1, 1, tn), lambda l, j: (l, 0, j)),
                  pl.BlockSpec((rows, 1), lambda l, j: (l * nj + j, 0)),
                  pl.BlockSpec((1, LANES), lambda l, j: (0, 0))],
        out_specs=(pl.BlockSpec((1, SUBLANES, tn), lambda l, j: (l, 0, j)), table, table),
        compiler_params=_cparams("parallel", "parallel"),
        name="modulation",
    )(c8, mod_w, mod_b.reshape(DEPTH, 1, n), pos2, invf)


NORM_ROWS = 128


def _row_rsqrt(x_ref, rs_scr):
    def body(i, carry):
        rows = pl.ds(pl.multiple_of(i * NORM_ROWS, NORM_ROWS), NORM_ROWS)
        x = x_ref[rows, :]
        rs_scr[rows, :] = lax.rsqrt(jnp.mean(x * x, axis=-1, keepdims=True) + RMS_EPS)
        return carry
    lax.fori_loop(0, x_ref.shape[0] // NORM_ROWS, body, 0, unroll=True)


def _modulated(x_ref, rs_scr, g_ref, sc_ref, sh_ref):
    gain = g_ref[...] * (1.0 + sc_ref[0])
    return (x_ref[...] * rs_scr[...] * gain + sh_ref[0]).astype(BF16)


GELU_A = math.sqrt(2.0 / math.pi)
GELU_B = GELU_A * 0.044715
ACT_COLS = 256


def _in_proj_act_kernel(x_ref, g_ref, sc_ref, sh_ref, w_ref, act_ref, *rest, select):
    if len(rest) == 4:
        side_ref, o_ref, side_out_ref, rs_scr = rest
        side_out_ref[...] = side_ref[...].astype(BF16)
    else:
        o_ref, rs_scr = rest

    @pl.when(pl.program_id(1) == 0)
    def _():
        _row_rsqrt(x_ref, rs_scr)

    h = _modulated(x_ref, rs_scr, g_ref, sc_ref, sh_ref)
    for n in range(o_ref.shape[1] // ACT_COLS):
        cols = pl.ds(n * ACT_COLS, ACT_COLS)
        p = jnp.dot(h, w_ref[:, cols], preferred_element_type=F32)
        gated = (0.5 * p) * (1.0 + jnp.tanh(p * (act_ref[0:1, cols] + act_ref[1:2, cols] * (p * p))))
        o_ref[:, cols] = (jnp.where(act_ref[2:3, cols] > 0.0, gated, p) if select else gated).astype(o_ref.dtype)


def _in_proj_lora_kernel(x_ref, g_ref, sc_ref, sh_ref, w_ref, wl_ref, o_ref, ol_ref, rs_scr):
    @pl.when(pl.program_id(1) == 0)
    def _():
        _row_rsqrt(x_ref, rs_scr)
        ol_ref[...] = jnp.dot(_modulated(x_ref, rs_scr, g_ref, sc_ref, sh_ref), wl_ref[...],
                              preferred_element_type=F32)

    h = _modulated(x_ref, rs_scr, g_ref, sc_ref, sh_ref)
    o_ref[...] = jnp.dot(h, w_ref[...], preferred_element_type=F32)


SIDE_ROWS = 128


def _in_proj_act(x2, g, mod3, w_bf, wl, ncols, act, select, out_dtype, side=None, *, tm=1024):
    n_tok = x2.shape[0]
    tn = 2048 if (out_dtype == BF16 and ncols % 2048 == 0) else (1024 if ncols % 1024 == 0 else 1536)
    assert ncols % tn == 0
    per_b = SEQ // tm
    nj = ncols // tn
    in_specs = [pl.BlockSpec((tm, D), lambda i, j: (i, 0)),
                pl.BlockSpec((1, D), lambda i, j: (0, 0)),
                pl.BlockSpec((1, 1, D), lambda i, j: (i // per_b, 0, 1)),
                pl.BlockSpec((1, 1, D), lambda i, j: (i // per_b, 0, 0)),
                pl.BlockSpec((None, D, tn), lambda i, j: (wl, 0, j)),
                pl.BlockSpec((SUBLANES, tn), lambda i, j: (0, j))]
    out_shape = jax.ShapeDtypeStruct((n_tok, ncols), out_dtype)
    out_specs = pl.BlockSpec((tm, tn), lambda i, j: (i, j))
    args = [x2, g, mod3, mod3, w_bf, act]
    osize = jnp.dtype(out_dtype).itemsize
    vmem = 2 * (4 * tm * D + 2 * D * tn + osize * tm * tn) + 8 * 1024 * 1024
    if side is not None:
        side_w, sl = side
        _, srows, scols = side_w.shape
        nblk = srows // SIDE_ROWS
        assert srows % SIDE_ROWS == 0 and nblk <= (n_tok // tm) * nj
        blk = lambda i, j: jnp.minimum(i * nj + j, nblk - 1)
        in_specs.append(pl.BlockSpec((None, SIDE_ROWS, scols), lambda i, j: (sl, blk(i, j), 0)))
        args.append(side_w)
        out_shape = (out_shape, jax.ShapeDtypeStruct((srows, scols), BF16))
        out_specs = (out_specs, pl.BlockSpec((SIDE_ROWS, scols), lambda i, j: (blk(i, j), 0)))
        vmem += 2 * (4 + 2) * SIDE_ROWS * scols
    return pl.pallas_call(
        functools.partial(_in_proj_act_kernel, select=select),
        out_shape=out_shape,
        grid=(n_tok // tm, nj),
        in_specs=in_specs,
        out_specs=out_specs,
        scratch_shapes=[pltpu.VMEM((tm, 1), F32)],
        compiler_params=_cparams("arbitrary", "arbitrary", vmem=vmem),
        name="in_proj_act",
    )(*args)


def _in_proj(x2, g, mod3, w_bf, wl, ncols, w_lora_bf, *, tm=1024):
    n_tok = x2.shape[0]
    tn = 2048
    assert ncols % tn == 0
    per_b = SEQ // tm
    x_spec = pl.BlockSpec((tm, D), lambda i, j: (i, 0))
    g_spec = pl.BlockSpec((1, D), lambda i, j: (0, 0))
    sh_spec = pl.BlockSpec((1, 1, D), lambda i, j: (i // per_b, 0, 0))
    sc_spec = pl.BlockSpec((1, 1, D), lambda i, j: (i // per_b, 0, 1))
    w_spec = pl.BlockSpec((None, D, tn), lambda i, j: (wl, 0, j))
    o_spec = pl.BlockSpec((tm, tn), lambda i, j: (i, j))
    grid = (n_tok // tm, ncols // tn)
    scratch = [pltpu.VMEM((tm, 1), F32)]
    nl = w_lora_bf.shape[1]
    vmem = 2 * (4 * tm * D + 2 * D * tn + 4 * tm * tn + 2 * D * nl + 4 * tm * nl) + 4 * 1024 * 1024
    return pl.pallas_call(
        _in_proj_lora_kernel,
        out_shape=(jax.ShapeDtypeStruct((n_tok, ncols), F32), jax.ShapeDtypeStruct((n_tok, nl), F32)),
        grid=grid,
        in_specs=[x_spec, g_spec, sc_spec, sh_spec, w_spec, pl.BlockSpec((D, nl), lambda i, j: (0, 0))],
        out_specs=(o_spec, pl.BlockSpec((tm, nl), lambda i, j: (i, 0))),
        scratch_shapes=scratch,
        compiler_params=_cparams("parallel", "arbitrary", vmem=vmem),
        name="in_proj_lora",
    )(x2, g, mod3, mod3, w_bf, w_lora_bf)


CAST_ROWS = 256


def _round_weight(w_ref, w_scr):
    @pl.when(pl.program_id(0) == 0)
    def _():
        def body(i, carry):
            rows = pl.ds(pl.multiple_of(i * CAST_ROWS, CAST_ROWS), CAST_ROWS)
            w_scr[rows, :] = w_ref[rows, :].astype(BF16)
            return carry
        lax.fori_loop(0, w_ref.shape[0] // CAST_ROWS, body, 0, unroll=True)


def _out_proj_kernel(m_ref, w_ref, x_ref, gate_ref, o_ref, w_scr):
    _round_weight(w_ref, w_scr)
    y = jnp.dot(m_ref[...], w_scr[...], preferred_element_type=F32)
    o_ref[...] = x_ref[...] + gate_ref[0] * y


def _out_proj_final_kernel(m_ref, w_ref, x_ref, gate_ref, fg_ref, o_ref, w_scr):
    _round_weight(w_ref, w_scr)
    y = jnp.dot(m_ref[...], w_scr[...], preferred_element_type=F32)
    xn = x_ref[...] + gate_ref[0] * y
    o_ref[...] = xn * lax.rsqrt(jnp.mean(xn * xn, axis=-1, keepdims=True) + RMS_EPS) * fg_ref[...]


def _out_proj(mix, w, wl, x2, mod3, final_g=None, *, tm=512):
    n_tok = x2.shape[0]
    per_b = SEQ // tm
    in_specs = [pl.BlockSpec((tm, D), lambda i: (i, 0)),
                pl.BlockSpec((None, D, D), lambda i: (wl, 0, 0), pipeline_mode=pl.Buffered(1)),
                pl.BlockSpec((tm, D), lambda i: (i, 0)),
                pl.BlockSpec((1, 1, D), lambda i: (i // per_b, 0, 2))]
    args = [mix, w, x2, mod3]
    kern = _out_proj_kernel
    if final_g is not None:
        in_specs.append(pl.BlockSpec((1, D), lambda i: (0, 0)))
        args.append(final_g)
        kern = _out_proj_final_kernel
    return pl.pallas_call(
        kern,
        out_shape=jax.ShapeDtypeStruct((n_tok, D), F32),
        grid=(n_tok // tm,),
        in_specs=in_specs,
        out_specs=pl.BlockSpec((tm, D), lambda i: (i, 0)),
        scratch_shapes=[pltpu.VMEM((D, D), BF16)],
        compiler_params=_cparams("arbitrary", vmem=6 * D * D + 2 * (2 + 4 + 4) * tm * D + 8 * 1024 * 1024),
        name="out_proj_final" if final_g is not None else "out_proj",
    )(*args)


def _sg_mix_kernel(u_ref, v_ref, z_ref, lng_ref, lnb_ref, ws_ref, bs_ref, o_ref, *, tm):
    ti = lax.broadcasted_iota(jnp.int32, (SG_CHUNK, SG_CHUNK), 0)
    si = lax.broadcasted_iota(jnp.int32, (SG_CHUNK, SG_CHUNK), 1)
    causal = ti >= si
    ws = [jnp.where(causal, ws_ref[g], 0.0).astype(BF16) for g in range(SG_GROUPS)]
    for c in range(tm // SG_CHUNK):
        rows = pl.ds(c * SG_CHUNK, SG_CHUNK)
        v = v_ref[rows, :].astype(F32)
        mean = jnp.mean(v, axis=-1, keepdims=True)
        d = v - mean
        var = jnp.mean(d * d, axis=-1, keepdims=True)
        vn = (d * lax.rsqrt(var + LN_EPS) * lng_ref[...] + lnb_ref[...]).astype(BF16)
        f = [jnp.dot(ws[g], vn[:, g * LANES:(g + 1) * LANES], preferred_element_type=F32) for g in range(SG_GROUPS)]
        for g in range(SG_GROUPS):
            cols = pl.ds(g * LANES, LANES)
            uz = u_ref[rows, cols].astype(F32) * z_ref[rows, cols].astype(F32)
            o_ref[rows, cols] = (uz * (f[g] + bs_ref[g])).astype(BF16)


def _sg_out_kernel(u_ref, v_ref, z_ref, lng_ref, lnb_ref, ws_ref, bs_ref, w_ref, x_ref, gate_ref, *rest, tm, final):
    if final:
        fg_ref, o_ref, w_scr, mix_scr = rest
    else:
        o_ref, w_scr, mix_scr = rest
    _round_weight(w_ref, w_scr)
    _sg_mix_kernel(u_ref, v_ref, z_ref, lng_ref, lnb_ref, ws_ref, bs_ref, mix_scr, tm=tm)
    y = jnp.dot(mix_scr[...], w_scr[...], preferred_element_type=F32)
    xn = x_ref[...] + gate_ref[0] * y
    if final:
        xn = xn * lax.rsqrt(jnp.mean(xn * xn, axis=-1, keepdims=True) + RMS_EPS) * fg_ref[...]
    o_ref[...] = xn


def _sg_out(p, ln_g, ln_b, w_s, b_full, w, wl, x2, mod3, final_g=None, *, tm=256):
    n_tok = p.shape[0]
    per_b = SEQ // tm
    tile = lambda c: pl.BlockSpec((tm, D), lambda i: (i, c))
    vec = pl.BlockSpec((1, D), lambda i: (0, 0))
    in_specs = [tile(0), tile(1), tile(2), vec, vec,
                pl.BlockSpec((SG_GROUPS, SG_CHUNK, SG_CHUNK), lambda i: (0, 0, 0)),
                pl.BlockSpec((SG_GROUPS, SG_CHUNK, LANES), lambda i: (0, 0, 0)),
                pl.BlockSpec((None, D, D), lambda i: (wl, 0, 0), pipeline_mode=pl.Buffered(1)),
                tile(0),
                pl.BlockSpec((1, 1, D), lambda i: (i // per_b, 0, 2))]
    args = [p, p, p, ln_g, ln_b, w_s, b_full, w, x2, mod3]
    if final_g is not None:
        in_specs.append(vec)
        args.append(final_g)
    vmem = (6 * D * D + 2 * tm * D + 2 * (3 * 2 + 4 + 4) * tm * D
            + 2 * 4 * SG_GROUPS * SG_CHUNK * (SG_CHUNK + LANES) + 8 * 1024 * 1024)
    return pl.pallas_call(
        functools.partial(_sg_out_kernel, tm=tm, final=final_g is not None),
        out_shape=jax.ShapeDtypeStruct((n_tok, D), F32),
        grid=(n_tok // tm,),
        in_specs=in_specs,
        out_specs=pl.BlockSpec((tm, D), lambda i: (i, 0)),
        scratch_shapes=[pltpu.VMEM((D, D), BF16), pltpu.VMEM((tm, D), BF16)],
        compiler_params=_cparams("arbitrary", vmem=vmem),
        name="sg_out_final" if final_g is not None else "sg_out",
    )(*args)


def _first_half(shape):
    return (lax.broadcasted_iota(jnp.int32, shape, 1) % HEAD) < (HEAD // 2)


def _rope_table_kernel(pos_ref, invf_ref, cos_ref, sin_ref):
    ang = pos_ref[...].astype(F32) * invf_ref[...]
    cos_ref[...] = jnp.cos(ang)
    sin_ref[...] = jnp.where(_first_half(ang.shape), -jnp.sin(ang), jnp.sin(ang))


def _rope(x, cos, sin_signed, first):
    partner = jnp.where(first, pltpu.roll(x, LANES - HEAD // 2, 1), pltpu.roll(x, HEAD // 2, 1))
    return x * cos + partner * sin_signed


def _swa_mix_kernel(sink_ref, p_ref, kvp_ref, cos_ref, sin_ref, cosp_ref, sinp_ref, o_ref):
    i = pl.program_id(0)
    first_key = jnp.where((i % (SEQ // SWA_BLOCK)) != 0, 0, SWA_BLOCK)
    cos, sin_s, cosp, sinp_s = cos_ref[...], sin_ref[...], cosp_ref[...], sinp_ref[...]
    first = _first_half(cos.shape)
    lane = lax.broadcasted_iota(jnp.int32, (2 * SWA_BLOCK, LANES), 1)
    low = lane < HEAD

    qi = lax.broadcasted_iota(jnp.int32, (SWA_BLOCK, 2 * SWA_BLOCK), 0)
    kj = lax.broadcasted_iota(jnp.int32, (SWA_BLOCK, 2 * SWA_BLOCK), 1)
    rel = qi + SWA_BLOCK - kj
    mask = (rel >= 0) & (rel < SWA_BLOCK) & (kj >= first_key)

    k_lo, k_hi, v_lo, v_hi = [], [], [], []
    for kp in range(SWA_KV // 2):
        ksl = pl.ds(kp * LANES, LANES)
        vsl = pl.ds(SWA_KVW + kp * LANES, LANES)
        k2 = jnp.concatenate([_rope(kvp_ref[:, ksl], cosp, sinp_s, first),
                              _rope(p_ref[:, pl.ds(D + kp * LANES, LANES)], cos, sin_s, first)], axis=0)
        v2 = jnp.concatenate([kvp_ref[:, vsl], p_ref[:, pl.ds(D + SWA_KVW + kp * LANES, LANES)]], axis=0)
        k2r = pltpu.roll(k2, HEAD, 1)
        v2r = pltpu.roll(v2, HEAD, 1)
        for sub in range(2):
            src_k, src_kr = (k2, k2r) if sub == 0 else (k2r, k2)
            src_v, src_vr = (v2, v2r) if sub == 0 else (v2r, v2)
            k_lo.append(jnp.where(low, src_k, 0.0).astype(BF16))
            k_hi.append(jnp.where(low, 0.0, src_kr).astype(BF16))
            v_lo.append(jnp.where(low, src_v, 0.0).astype(BF16))
            v_hi.append(jnp.where(low, 0.0, src_vr).astype(BF16))

    zoff = D + 2 * SWA_KVW
    for h0 in range(0, SWA_HEADS, SWA_STAGE):
        hs = range(h0, h0 + SWA_STAGE)
        prs = range(h0 // 2, (h0 + SWA_STAGE) // 2)
        cols = {pr: pl.ds(pr * LANES, LANES) for pr in prs}
        q2 = {pr: (_rope(p_ref[:, cols[pr]], cos, sin_s, first) * (HEAD ** -0.5)).astype(BF16) for pr in prs}
        s = {h: _dot_nt(q2[h // 2], (k_hi if h % 2 else k_lo)[h // 8]) for h in hs}
        s = {h: jnp.where(mask, s[h], -jnp.inf) for h in hs}
        m = {h: jnp.maximum(jnp.max(s[h], axis=-1, keepdims=True), sink_ref[h]) for h in hs}
        e = {h: jnp.exp(s[h] - m[h]) for h in hs}
        denom = {h: jnp.sum(e[h], axis=-1, keepdims=True) + jnp.exp(sink_ref[h] - m[h]) for h in hs}
        prob = {h: (e[h] * (1.0 / denom[h])).astype(BF16) for h in hs}
        o = {h: jnp.dot(prob[h], (v_hi if h % 2 else v_lo)[h // 8], preferred_element_type=F32) for h in hs}
        for pr in prs:
            gate = p_ref[:, pl.ds(zoff + pr * LANES, LANES)]
            o_ref[:, cols[pr]] = ((o[2 * pr] + o[2 * pr + 1]) * gate).astype(BF16)


def _swa_mix(p, cos_t, sin_t, sinks):
    n_tok, ncols = p.shape
    tb = SWA_BLOCK
    kvb = D // (2 * SWA_KVW)
    cur = pl.BlockSpec((tb, LANES), lambda i: (i, 0))
    prev = pl.BlockSpec((tb, LANES), lambda i: (jnp.maximum(i - 1, 0), 0))
    return pl.pallas_call(
        _swa_mix_kernel,
        out_shape=jax.ShapeDtypeStruct((n_tok, D), BF16),
        grid=(n_tok // tb,),
        in_specs=[pl.BlockSpec(memory_space=pltpu.SMEM),
                  pl.BlockSpec((tb, ncols), lambda i: (i, 0)),
                  pl.BlockSpec((tb, 2 * SWA_KVW), lambda i: (jnp.maximum(i - 1, 0), kvb)),
                  cur, cur, prev, prev],
        out_specs=pl.BlockSpec((tb, D), lambda i: (i, 0)),
        compiler_params=_cparams("parallel"),
        name="swa_mix",
    )(sinks, p, p, cos_t, sin_t, cos_t, sin_t)


def _head_sums(xs, seg):
    out = jnp.dot(jnp.concatenate([x.astype(BF16) for x in xs], axis=0), seg, preferred_element_type=F32)
    n = xs[0].shape[0]
    return [out[i * n:(i + 1) * n] for i in range(len(xs))]


def _cumsum_rows(x, tri):
    hi = x.astype(BF16)
    lo = (x - hi.astype(F32)).astype(BF16)
    return jnp.dot(tri, hi, preferred_element_type=F32) + jnp.dot(tri, lo, preferred_element_type=F32)


def _bd(x, same_head):
    t = jnp.concatenate([x] * RW_HG, axis=0)
    return jnp.where(same_head, t, 0.0).astype(BF16)


def _diag_blocks(o, lane_head):
    acc = o[0:HEAD]
    for h in range(1, RW_HG):
        acc = jnp.where(lane_head == h, o[h * HEAD:(h + 1) * HEAD], acc)
    return acc


def _dot(a, b):
    return jnp.dot(a.astype(BF16), b.astype(BF16), preferred_element_type=F32)


def _dot_nt(a, b):
    return lax.dot_general(a, b, (((1,), (1,)), ((), ())), preferred_element_type=F32)


def _dot_tn(a, b):
    return lax.dot_general(a, b, (((0,), (0,)), ((), ())), preferred_element_type=F32)


def _rwkv_groups(r, lw, cum, k, v, kap, bet, h0, masks):
    incl, strict, eye, same_head, lane_head = masks
    c = RW_CHUNK
    gr = range(len(r))
    bd = lambda t: _bd(t, same_head)
    g_t = [jnp.exp(cum[g]) for g in gr]
    g_inv = [jnp.exp(-cum[g]) for g in gr]
    cref = [cum[g][c // 2 - 1:c // 2] for g in gr]
    cend = [cum[g][c - 1:c] for g in gr]
    to_ref = [jnp.exp(-cref[g]) for g in gr]
    r_true = [r[g] * g_t[g] for g in gr]
    kap_true = [kap[g] * (g_t[g] * jnp.exp(-lw[g])) for g in gr]
    r_s = [r_true[g] * to_ref[g] for g in gr]
    kap_s = [kap_true[g] * to_ref[g] for g in gr]
    ginv = [g_inv[g] * jnp.exp(cref[g]) for g in gr]
    eend = [g_inv[g] * jnp.exp(cend[g]) for g in gr]

    s = [_dot_nt(jnp.concatenate([r_s[g], kap_s[g]], axis=0).astype(BF16),
                 jnp.concatenate([bd(k[g] * ginv[g]), bd(bet[g] * ginv[g])], axis=0)) for g in gr]
    a_rk = [jnp.where(incl, s[g][:c, :RW_GW], 0.0) for g in gr]
    a_rb = [jnp.where(incl, s[g][:c, RW_GW:], 0.0) for g in gr]
    a_kk = [jnp.where(strict, s[g][c:, :RW_GW], 0.0) for g in gr]
    a_kb = [jnp.where(strict, s[g][c:, RW_GW:], 0.0) for g in gr]

    x = [eye - a_kb[g] for g in gr]
    p = [_dot(a_kb[g], bd(a_kb[g])) for g in gr]
    akv = [_dot(a_kk[g], bd(v[g])) for g in gr]
    for _ in range(4):
        xp = [_dot(jnp.concatenate([x[g], p[g]], axis=0), bd(p[g])) for g in gr]
        x = [x[g] + xp[g][:c] for g in gr]
        p = [xp[g][c:] for g in gr]
    t_inv = [x[g] + _dot(x[g], bd(p[g])) for g in gr]

    wu = [-_dot(t_inv[g], jnp.concatenate([bd(kap_true[g]), bd(akv[g])], axis=1)) for g in gr]

    hw = [_dot(jnp.concatenate([wu[g][:, :RW_GW], r_true[g], eye * jnp.exp(cend[g])], axis=0), bd(h0[g]))
          for g in gr]
    u = [hw[g][:c] + wu[g][:, RW_GW:] for g in gr]
    y = [hw[g][c:2 * c] + _dot(jnp.concatenate([a_rk[g], a_rb[g]], axis=1),
                               jnp.concatenate([bd(v[g]), bd(u[g])], axis=0)) for g in gr]
    o2 = [_dot_tn(jnp.concatenate([k[g] * eend[g], bet[g] * eend[g]], axis=0).astype(BF16),
                  jnp.concatenate([v[g], u[g]], axis=0).astype(BF16)) for g in gr]
    h_new = [hw[g][2 * c:] + _diag_blocks(o2[g], lane_head) for g in gr]
    return y, h_new


def _rwkv_mix_kernel(p_ref, pp_ref, pl_ref, plp_ref, mu_ref, mul_ref, w0_ref, a0_ref, wl_ref, al_ref,
                     kk_ref, ka_ref, rk_ref, gng_ref, gnb_ref, *rest):
    if len(rest) == 4:
        side_ref, o_ref, side_out_ref, h_scr = rest
        side_out_ref[...] = side_ref[...].astype(BF16)
    else:
        o_ref, h_scr = rest
    ci = pl.program_id(1)
    c = RW_CHUNK

    @pl.when(ci == 0)
    def _():
        h_scr[...] = jnp.zeros_like(h_scr)

    not_first = (ci != 0).astype(F32)
    row = lax.broadcasted_iota(jnp.int32, (c, 1), 0)

    def shifted(cur, prev_ref):
        prev_row = prev_ref[SUBLANES - 1:SUBLANES, :] * not_first
        rolled = pltpu.roll(cur, 1, 0)
        return jnp.where(row == 0, prev_row, rolled)

    lcur = pl_ref[...]
    lcur = lcur + (shifted(lcur, plp_ref) - lcur) * mul_ref[...]
    dw = jnp.tanh(lcur[:, :RW_LP // 2])
    da = lcur[:, RW_LP // 2:]

    ti = lax.broadcasted_iota(jnp.int32, (c, RW_GW), 0)
    li = lax.broadcasted_iota(jnp.int32, (c, RW_GW), 1)
    si = li % HEAD
    lane_head = li // HEAD
    rb = lax.broadcasted_iota(jnp.int32, (RW_GW, RW_GW), 0) // HEAD
    cb = lax.broadcasted_iota(jnp.int32, (RW_GW, RW_GW), 1) // HEAD
    same_head = rb == cb
    seg = same_head.astype(BF16)
    masks = (ti >= si, ti > si, (ti == si).astype(F32), same_head, lane_head)

    gr = range(RW_GROUPS)
    gsl = [pl.ds(g * RW_GW, RW_GW) for g in gr]

    def lerp(off, g):
        cols = pl.ds(off + g * RW_GW, RW_GW)
        cur = p_ref[:, cols]
        return cur + (shifted(cur, pp_ref.at[:, cols]) - cur) * mu_ref[:, cols]

    r = [lerp(0, g) for g in gr]
    k = [lerp(D, g) for g in gr]
    v = [lerp(2 * D, g) for g in gr]
    lw = [-DECAY_SCALE * jax.nn.sigmoid(w0_ref[:, gsl[g]] + _dot(dw, wl_ref[:, gsl[g]])) for g in gr]
    a = [jax.nn.sigmoid(a0_ref[:, gsl[g]] + _dot(da, al_ref[:, gsl[g]])) for g in gr]
    kap = [k[g] * kk_ref[:, gsl[g]] for g in gr]
    k = [k[g] * (1.0 + (a[g] - 1.0) * ka_ref[:, gsl[g]]) for g in gr]
    sums = _head_sums([kap[g] * kap[g] for g in gr] + [r[g] * k[g] * rk_ref[:, gsl[g]] for g in gr], seg)
    kap = [kap[g] / jnp.maximum(jnp.sqrt(sums[g]), 1e-12) for g in gr]
    bonus = sums[RW_GROUPS:]
    bet = [kap[g] * a[g] for g in gr]
    tri = (lax.broadcasted_iota(jnp.int32, (c, c), 0) >= lax.broadcasted_iota(jnp.int32, (c, c), 1)).astype(BF16)
    cum = [_cumsum_rows(lw[g], tri) for g in gr]

    y, h_new = [], []
    for lo in range(0, RW_GROUPS, RW_STAGE):
        sl = slice(lo, lo + RW_STAGE)
        ys, hs = _rwkv_groups(r[sl], lw[sl], cum[sl], k[sl], v[sl], kap[sl], bet[sl],
                              [h_scr[g] for g in range(lo, lo + RW_STAGE)], masks)
        y += ys
        h_new += hs
    for g in gr:
        h_scr[g] = h_new[g]

    ysum = _head_sums(y, seg)
    dlt = [y[g] - ysum[g] * (1.0 / HEAD) for g in gr]
    dsum = _head_sums([dlt[g] * dlt[g] for g in gr], seg)
    for g in gr:
        rstd = lax.rsqrt(dsum[g] * (1.0 / HEAD) + GN_EPS)
        yn = dlt[g] * rstd * gng_ref[:, gsl[g]] + gnb_ref[:, gsl[g]] + bonus[g] * v[g]
        o_ref[:, gsl[g]] = (yn * jax.nn.silu(lerp(3 * D, g))).astype(BF16)


def _rwkv_mix(p, p_lora, mu, mu_l, w0, a0, w_lora, a_lora, k_k, k_a, r_k, gn_g, gn_b, side=None):
    n_tok = p.shape[0]
    c = RW_CHUNK
    nc = SEQ // c
    ncol = p.shape[1]
    nl = p_lora.shape[1]
    row = lambda b, ci: (b * nc + ci, 0)
    prev = lambda b, ci: (jnp.maximum((b * nc + ci) * (c // SUBLANES) - 1, 0), 0)
    vec = pl.BlockSpec((1, D), lambda b, ci: (0, 0))
    lora = pl.BlockSpec((RW_LP // 2, D), lambda b, ci: (0, 0))
    in_specs = [pl.BlockSpec((c, ncol), row),
                pl.BlockSpec((SUBLANES, ncol), prev),
                pl.BlockSpec((c, nl), row),
                pl.BlockSpec((SUBLANES, nl), prev),
                pl.BlockSpec((1, ncol), lambda b, ci: (0, 0)),
                pl.BlockSpec((1, nl), lambda b, ci: (0, 0)),
                vec, vec, lora, lora,
                vec, vec, vec, vec, vec]
    args = [p, p, p_lora, p_lora, mu, mu_l, w0, a0, w_lora, a_lora, k_k, k_a, r_k, gn_g, gn_b]
    out_shape = jax.ShapeDtypeStruct((n_tok, D), BF16)
    out_specs = pl.BlockSpec((c, D), row)
    if side is not None:
        side_w, sl = side
        steps = (n_tok // SEQ) * nc
        _, srows, scols = side_w.shape
        srow = srows // steps
        assert srows % steps == 0 and srow % 16 == 0
        in_specs.append(pl.BlockSpec((None, srow, scols), lambda b, ci: (sl, b * nc + ci, 0)))
        args.append(side_w)
        out_shape = (out_shape, jax.ShapeDtypeStruct((srows, scols), BF16))
        out_specs = (out_specs, pl.BlockSpec((srow, scols), row))
    return pl.pallas_call(
        _rwkv_mix_kernel,
        out_shape=out_shape,
        grid=(n_tok // SEQ, nc),
        in_specs=in_specs,
        out_specs=out_specs,
        scratch_shapes=[pltpu.VMEM((RW_GROUPS, HEAD, RW_GW), F32)],
        compiler_params=_cparams("parallel", "arbitrary"),
        name="rwkv_mix",
    )(*args)


def kernel(x, c, positions, norm_g, mod_w, mod_b, final_norm_g, sg_w_in, sg_w_out, sg_ln_g, sg_ln_b, sg_w_spatial,
           sg_b_spatial, swa_w_in, swa_w_out, swa_sinks, rwkv_w_in, rwkv_w_out, rwkv_mu, rwkv_w0, rwkv_w_lora,
           rwkv_a0, rwkv_a_lora, rwkv_k_k, rwkv_k_a, rwkv_r_k, rwkv_gn_g, rwkv_gn_b):
    batch, seq, d = x.shape
    assert (seq, d) == (SEQ, D) and norm_g.shape[0] == DEPTH
    n_tok = batch * seq
    x2 = x.reshape(n_tok, D)
    c8 = jnp.zeros((SUBLANES, D), F32).at[:batch].set(c)
    pos2 = positions.reshape(n_tok, 1)
    half = HEAD // 2
    inv_freq = ROPE_THETA ** (-jnp.arange(half, dtype=F32) / half)
    invf = jnp.tile(inv_freq, LANES // half).reshape(1, LANES)
    mod, cos_t, sin_t = _modulation(c8, mod_w, mod_b, pos2, invf)
    row = lambda a: a.reshape(1, -1)

    def act_rows(spans):
        rows = [jnp.concatenate([jnp.full((w,), val[t], F32) for w, *val in spans]) for t in range(3)]
        return jnp.zeros((SUBLANES, rows[0].shape[0]), F32).at[:3].set(jnp.stack(rows))

    gelu_span = (GELU_A, GELU_B, 1.0)
    silu_span = (0.5, 0.0, 1.0)
    sg_first_bf, rwkv_in_bf = sg_w_in[:1].astype(BF16), rwkv_w_in.astype(BF16)
    w_out = (sg_w_out, swa_w_out, rwkv_w_out)
    side_bf = None
    for i in range(DEPTH):
        kind, j = i % 3, i // 3
        mod3 = mod[i, :batch].reshape(batch, 1, 3 * D)
        g = row(norm_g[i])
        has_next = i + 1 < DEPTH
        final_g = None if has_next else row(final_norm_g)
        if kind == 0:
            act = act_rows([(2 * D, *gelu_span), (D, *silu_span)])
            w = sg_first_bf if i == 0 else side_bf[None]
            p = _in_proj_act(x2, g, mod3, w, 0, 3 * D, act, False, BF16, (swa_w_in, j) if has_next else None)
            p, side_bf = p if has_next else (p, None)
            x2 = _sg_out(p, row(sg_ln_g[j]), row(sg_ln_b[j]), sg_w_spatial[j],
                         jnp.broadcast_to(sg_b_spatial[j][:, :, None], (SG_GROUPS, SG_CHUNK, LANES)),
                         sg_w_out, j, x2, mod3, final_g)
            continue
        elif kind == 1:
            act = act_rows([(D + 2 * SWA_KVW, 0.0, 0.0, 0.0), (D, *silu_span)])
            p = _in_proj_act(x2, g, mod3, side_bf[None], 0, 2 * D + 2 * SWA_KVW, act, True, F32)
            mix = _swa_mix(p, cos_t, sin_t, swa_sinks[j])
        else:
            lp = RW_LP // 2 - RW_LORA
            w_in = rwkv_in_bf[j]
            w_l = jnp.concatenate([jnp.pad(w_in[:, 4 * D:4 * D + RW_LORA], ((0, 0), (0, lp))),
                                   jnp.pad(w_in[:, 4 * D + RW_LORA:], ((0, 0), (0, lp)))], axis=1)
            p, p_lora = _in_proj(x2, g, mod3, rwkv_in_bf, j, 4 * D, w_l)
            mu = rwkv_mu[j]
            mu_l = jnp.concatenate([jnp.pad(mu[4 * D:4 * D + RW_LORA], (0, lp)), jnp.pad(mu[4 * D + RW_LORA:], (0, lp))])
            wl_pad = jnp.pad(rwkv_w_lora[j], ((0, lp), (0, 0))).astype(BF16)
            al_pad = jnp.pad(rwkv_a_lora[j], ((0, lp), (0, 0))).astype(BF16)
            mix = _rwkv_mix(p, p_lora, row(mu[:4 * D]), row(mu_l), row(rwkv_w0[j]),
                            row(rwkv_a0[j]), wl_pad, al_pad, row(rwkv_k_k[j]),
                            row(rwkv_k_a[j]), row(rwkv_r_k[j]), row(rwkv_gn_g[j]), row(rwkv_gn_b[j]),
                            (sg_w_in, j + 1) if has_next else None)
            mix, side_bf = mix if has_next else (mix, None)
        x2 = _out_proj(mix, w_out[kind], j, x2, mod3, final_g)
    return x2.reshape(batch, seq, D)
```

```python
import functools
import math

import jax
import jax.numpy as jnp
from jax import lax
from jax.experimental import pallas as pl
from jax.experimental.pallas import tpu as pltpu

F32 = jnp.float32
BF16 = jnp.bfloat16

D = 2048
SEQ = 4096
DEPTH = 4
HEAD = 64
LANES = 128
SUBLANES = 8
SG_CHUNK = 128
SG_GROUPS = 16
SWA_HEADS = D // HEAD
SWA_KV = SWA_HEADS // 8
SWA_KVW = SWA_KV * HEAD
SWA_BLOCK = 128
SWA_STAGE = 16
ROPE_THETA = 10000.0
RW_LORA = 96
RW_LP = 256
RW_CHUNK = 64
RW_HG = 4
RW_GW = RW_HG * HEAD
RW_GROUPS = D // RW_GW
RW_STAGE = 8
DECAY_SCALE = math.exp(-0.5)
GN_EPS = 64e-5
RMS_EPS = 1e-6
LN_EPS = 1e-5

VMEM_LIMIT = 48 * 1024 * 1024


def _cparams(*sem, vmem=VMEM_LIMIT):
    return pltpu.CompilerParams(dimension_semantics=sem, vmem_limit_bytes=vmem)


def _mod_kernel(c_ref, w_ref, b_ref, pos_ref, invf_ref, o_ref, cos_ref, sin_ref):
    cond = jax.nn.silu(c_ref[...]).astype(BF16)
    o_ref[0] = jnp.dot(cond, w_ref[0].astype(BF16), preferred_element_type=F32) + b_ref[0]
    _rope_table_kernel(pos_ref, invf_ref, cos_ref, sin_ref)


def _modulation(c8, mod_w, mod_b, pos2, invf, layers):
    tn = 1536
    n = mod_w.shape[2]
    nj = n // tn
    n_tok = pos2.shape[0]
    rows = n_tok // (layers * nj)
    assert n % tn == 0 and rows * layers * nj == n_tok and rows % SUBLANES == 0
    table = pl.BlockSpec((rows, LANES), lambda l, j: (l * nj + j, 0))
    return pl.pallas_call(
        _mod_kernel,
        out_shape=(jax.ShapeDtypeStruct((layers, SUBLANES, n), F32),
                   jax.ShapeDtypeStruct((n_tok, LANES), F32), jax.ShapeDtypeStruct((n_tok, LANES), F32)),
        grid=(layers, nj),
        in_specs=[pl.BlockSpec((SUBLANES, D), lambda l, j: (0, 0)),
                  pl.BlockSpec((1, D, tn), lambda l, j: (l, 0, j)),
                  pl.BlockSpec((1, 1, tn), lambda l, j: (l, 0, j)),
                  pl.BlockSpec((rows, 1), lambda l, j: (l * nj + j, 0)),
                  pl.BlockSpec((1, LANES), lambda l, j: (0, 0))],
        out_specs=(pl.BlockSpec((1, SUBLANES, tn), lambda l, j: (l, 0, j)), table, table),
        compiler_params=_cparams("parallel", "parallel"),
        name="modulation",
    )(c8, mod_w, mod_b.reshape(DEPTH, 1, n), pos2, invf)


NORM_ROWS = 128


def _row_rsqrt(x_ref, rs_scr):
    def body(i, carry):
        rows = pl.ds(pl.multiple_of(i * NORM_ROWS, NORM_ROWS), NORM_ROWS)
        x = x_ref[rows, :]
        rs_scr[rows, :] = lax.rsqrt(jnp.mean(x * x, axis=-1, keepdims=True) + RMS_EPS)
        return carry
    lax.fori_loop(0, x_ref.shape[0] // NORM_ROWS, body, 0, unroll=True)


def _modulated(x_ref, rs_scr, g_ref, sc_ref, sh_ref):
    gain = g_ref[...] * (1.0 + sc_ref[0])
    return (x_ref[...] * rs_scr[...] * gain + sh_ref[0]).astype(BF16)


GELU_A = math.sqrt(2.0 / math.pi)
GELU_B = GELU_A * 0.044715
ACT_COLS = 256


def _in_proj_act_kernel(x_ref, g_ref, sc_ref, sh_ref, w_ref, act_ref, *rest, select):
    if len(rest) == 4:
        side_ref, o_ref, side_out_ref, rs_scr = rest
        side_out_ref[...] = side_ref[...].astype(BF16)
    else:
        o_ref, rs_scr = rest

    @pl.when(pl.program_id(1) == 0)
    def _():
        _row_rsqrt(x_ref, rs_scr)

    h = _modulated(x_ref, rs_scr, g_ref, sc_ref, sh_ref)
    for n in range(o_ref.shape[1] // ACT_COLS):
        cols = pl.ds(n * ACT_COLS, ACT_COLS)
        p = jnp.dot(h, w_ref[:, cols], preferred_element_type=F32)
        gated = (0.5 * p) * (1.0 + jnp.tanh(p * (act_ref[0:1, cols] + act_ref[1:2, cols] * (p * p))))
        o_ref[:, cols] = (jnp.where(act_ref[2:3, cols] > 0.0, gated, p) if select else gated).astype(o_ref.dtype)


def _in_proj_lora_kernel(x_ref, g_ref, sc_ref, sh_ref, w_ref, wl_ref, o_ref, ol_ref, rs_scr):
    @pl.when(pl.program_id(1) == 0)
    def _():
        _row_rsqrt(x_ref, rs_scr)
        ol_ref[...] = jnp.dot(_modulated(x_ref, rs_scr, g_ref, sc_ref, sh_ref), wl_ref[...],
                              preferred_element_type=F32)

    h = _modulated(x_ref, rs_scr, g_ref, sc_ref, sh_ref)
    o_ref[...] = jnp.dot(h, w_ref[...], preferred_element_type=F32)


SIDE_ROWS = 128


def _in_proj_act(x2, g, mod3, w_bf, wl, ncols, act, select, out_dtype, side=None, *, tm=1024):
    n_tok = x2.shape[0]
    tn = 2048 if (out_dtype == BF16 and ncols % 2048 == 0) else (1024 if ncols % 1024 == 0 else 1536)
    assert ncols % tn == 0
    per_b = SEQ // tm
    nj = ncols // tn
    in_specs = [pl.BlockSpec((tm, D), lambda i, j: (i, 0)),
                pl.BlockSpec((1, D), lambda i, j: (0, 0)),
                pl.BlockSpec((1, 1, D), lambda i, j: (i // per_b, 0, 1)),
                pl.BlockSpec((1, 1, D), lambda i, j: (i // per_b, 0, 0)),
                pl.BlockSpec((None, D, tn), lambda i, j: (wl, 0, j)),
                pl.BlockSpec((SUBLANES, tn), lambda i, j: (0, j))]
    out_shape = jax.ShapeDtypeStruct((n_tok, ncols), out_dtype)
    out_specs = pl.BlockSpec((tm, tn), lambda i, j: (i, j))
    args = [x2, g, mod3, mod3, w_bf, act]
    osize = jnp.dtype(out_dtype).itemsize
    vmem = 2 * (4 * tm * D + 2 * D * tn + osize * tm * tn) + 8 * 1024 * 1024
    if side is not None:
        side_w, sl = side
        _, srows, scols = side_w.shape
        nblk = srows // SIDE_ROWS
        assert srows % SIDE_ROWS == 0 and nblk <= (n_tok // tm) * nj
        blk = lambda i, j: jnp.minimum(i * nj + j, nblk - 1)
        in_specs.append(pl.BlockSpec((None, SIDE_ROWS, scols), lambda i, j: (sl, blk(i, j), 0)))
        args.append(side_w)
        out_shape = (out_shape, jax.ShapeDtypeStruct((srows, scols), BF16))
        out_specs = (out_specs, pl.BlockSpec((SIDE_ROWS, scols), lambda i, j: (blk(i, j), 0)))
        vmem += 2 * (4 + 2) * SIDE_ROWS * scols
    return pl.pallas_call(
        functools.partial(_in_proj_act_kernel, select=select),
        out_shape=out_shape,
        grid=(n_tok // tm, nj),
        in_specs=in_specs,
        out_specs=out_specs,
        scratch_shapes=[pltpu.VMEM((tm, 1), F32)],
        compiler_params=_cparams("arbitrary", "arbitrary", vmem=vmem),
        name="in_proj_act",
    )(*args)


def _in_proj(x2, g, mod3, w_bf, wl, ncols, w_lora_bf, *, tm=1024):
    n_tok = x2.shape[0]
    tn = 2048
    assert ncols % tn == 0
    per_b = SEQ // tm
    x_spec = pl.BlockSpec((tm, D), lambda i, j: (i, 0))
    g_spec = pl.BlockSpec((1, D), lambda i, j: (0, 0))
    sh_spec = pl.BlockSpec((1, 1, D), lambda i, j: (i // per_b, 0, 0))
    sc_spec = pl.BlockSpec((1, 1, D), lambda i, j: (i // per_b, 0, 1))
    w_spec = pl.BlockSpec((None, D, tn), lambda i, j: (wl, 0, j))
    o_spec = pl.BlockSpec((tm, tn), lambda i, j: (i, j))
    grid = (n_tok // tm, ncols // tn)
    scratch = [pltpu.VMEM((tm, 1), F32)]
    nl = w_lora_bf.shape[1]
    vmem = 2 * (4 * tm * D + 2 * D * tn + 4 * tm * tn + 2 * D * nl + 4 * tm * nl) + 4 * 1024 * 1024
    return pl.pallas_call(
        _in_proj_lora_kernel,
        out_shape=(jax.ShapeDtypeStruct((n_tok, ncols), F32), jax.ShapeDtypeStruct((n_tok, nl), F32)),
        grid=grid,
        in_specs=[x_spec, g_spec, sc_spec, sh_spec, w_spec, pl.BlockSpec((D, nl), lambda i, j: (0, 0))],
        out_specs=(o_spec, pl.BlockSpec((tm, nl), lambda i, j: (i, 0))),
        scratch_shapes=scratch,
        compiler_params=_cparams("parallel", "arbitrary", vmem=vmem),
        name="in_proj_lora",
    )(x2, g, mod3, mod3, w_bf, w_lora_bf)


CAST_ROWS = 256


def _round_weight(w_ref, w_scr):
    @pl.when(pl.program_id(0) == 0)
    def _():
        def body(i, carry):
            rows = pl.ds(pl.multiple_of(i * CAST_ROWS, CAST_ROWS), CAST_ROWS)
            w_scr[rows, :] = w_ref[rows, :].astype(BF16)
            return carry
        lax.fori_loop(0, w_ref.shape[0] // CAST_ROWS, body, 0, unroll=True)


def _out_proj_kernel(m_ref, w_ref, x_ref, gate_ref, o_ref, w_scr):
    _round_weight(w_ref, w_scr)
    y = jnp.dot(m_ref[...], w_scr[...], preferred_element_type=F32)
    o_ref[...] = x_ref[...] + gate_ref[0] * y


MOD_ROWS = 16


def _out_proj_mod_kernel(m_ref, w_ref, x_ref, gate_ref, c_ref, mw_ref, mb_ref, o_ref, mo_ref, w_scr, cond_scr):
    @pl.when(pl.program_id(0) == 0)
    def _():
        cond_scr[...] = jax.nn.silu(c_ref[...])

    _out_proj_kernel(m_ref, w_ref, x_ref, gate_ref, o_ref, w_scr)
    nb, cols = c_ref.shape[0], mo_ref.shape[1]
    tiles = cols // LANES

    def body(r, acc):
        rows = pl.ds(pl.multiple_of(r * MOD_ROWS, MOD_ROWS), MOD_ROWS)
        w = [mw_ref[rows, pl.ds(t * LANES, LANES)] for t in range(tiles)]
        cond = [cond_scr[b, rows, :] for b in range(nb)]
        return tuple(acc[b * tiles + t] + w[t] * cond[b] for b in range(nb) for t in range(tiles))

    zero = jnp.zeros((MOD_ROWS, LANES), F32)
    acc = lax.fori_loop(0, D // MOD_ROWS, body, (zero,) * (nb * tiles), unroll=True)
    rowid = lax.broadcasted_iota(jnp.int32, (SUBLANES, cols), 0)
    out = jnp.zeros((SUBLANES, cols), F32)
    for b in range(nb):
        sums = [jnp.sum(acc[b * tiles + t], axis=0, keepdims=True) for t in range(tiles)]
        out = jnp.where(rowid == b, jnp.concatenate(sums, axis=1) + mb_ref[...], out)
    mo_ref[...] = out


def _out_proj_final_kernel(m_ref, w_ref, x_ref, gate_ref, fg_ref, o_ref, w_scr):
    _round_weight(w_ref, w_scr)
    y = jnp.dot(m_ref[...], w_scr[...], preferred_element_type=F32)
    xn = x_ref[...] + gate_ref[0] * y
    o_ref[...] = xn * lax.rsqrt(jnp.mean(xn * xn, axis=-1, keepdims=True) + RMS_EPS) * fg_ref[...]


def _out_proj(mix, w, wl, x2, mod3, final_g=None, next_mod=None, *, tm=512):
    n_tok = x2.shape[0]
    per_b = SEQ // tm
    steps = n_tok // tm
    in_specs = [pl.BlockSpec((tm, D), lambda i: (i, 0)),
                pl.BlockSpec((None, D, D), lambda i: (wl, 0, 0), pipeline_mode=pl.Buffered(1)),
                pl.BlockSpec((tm, D), lambda i: (i, 0)),
                pl.BlockSpec((1, 1, D), lambda i: (i // per_b, 0, 2))]
    args = [mix, w, x2, mod3]
    kern, name = _out_proj_kernel, "out_proj"
    out_shape = jax.ShapeDtypeStruct((n_tok, D), F32)
    out_specs = pl.BlockSpec((tm, D), lambda i: (i, 0))
    scratch = [pltpu.VMEM((D, D), BF16)]
    vmem = 6 * D * D + 2 * (2 + 4 + 4) * tm * D + 8 * 1024 * 1024
    if final_g is not None:
        in_specs.append(pl.BlockSpec((1, D), lambda i: (0, 0)))
        args.append(final_g)
        kern, name = _out_proj_final_kernel, "out_proj_final"
    elif next_mod is not None:
        cb, mod_w, mod_b, layer = next_mod
        n = mod_w.shape[2]
        cols = n // steps
        assert cols * steps == n and cols % LANES == 0
        in_specs += [pl.BlockSpec(cb.shape, lambda i: (0, 0, 0), pipeline_mode=pl.Buffered(1)),
                     pl.BlockSpec((None, D, cols), lambda i: (layer, 0, i)),
                     pl.BlockSpec((None, 1, cols), lambda i: (layer, 0, i))]
        args += [cb, mod_w, mod_b]
        kern, name = _out_proj_mod_kernel, "out_proj_mod"
        out_shape = (out_shape, jax.ShapeDtypeStruct((SUBLANES, n), F32))
        out_specs = (out_specs, pl.BlockSpec((SUBLANES, cols), lambda i: (0, i)))
        scratch.append(pltpu.VMEM(cb.shape, F32))
        vmem += 2 * 4 * cb.size + 2 * 4 * D * cols
    return pl.pallas_call(
        kern,
        out_shape=out_shape,
        grid=(steps,),
        in_specs=in_specs,
        out_specs=out_specs,
        scratch_shapes=scratch,
        compiler_params=_cparams("arbitrary", vmem=vmem),
        name=name,
    )(*args)


def _sg_mix_kernel(u_ref, v_ref, z_ref, lng_ref, lnb_ref, ws_ref, bs_ref, o_ref, *, tm):
    ti = lax.broadcasted_iota(jnp.int32, (SG_CHUNK, SG_CHUNK), 0)
    si = lax.broadcasted_iota(jnp.int32, (SG_CHUNK, SG_CHUNK), 1)
    causal = ti >= si
    ws = [jnp.where(causal, ws_ref[g], 0.0).astype(BF16) for g in range(SG_GROUPS)]
    for c in range(tm // SG_CHUNK):
        rows = pl.ds(c * SG_CHUNK, SG_CHUNK)
        v = v_ref[rows, :].astype(F32)
        mean = jnp.mean(v, axis=-1, keepdims=True)
        d = v - mean
        var = jnp.mean(d * d, axis=-1, keepdims=True)
        vn = (d * lax.rsqrt(var + LN_EPS) * lng_ref[...] + lnb_ref[...]).astype(BF16)
        f = [jnp.dot(ws[g], vn[:, g * LANES:(g + 1) * LANES], preferred_element_type=F32) for g in range(SG_GROUPS)]
        for g in range(SG_GROUPS):
            cols = pl.ds(g * LANES, LANES)
            uz = u_ref[rows, cols].astype(F32) * z_ref[rows, cols].astype(F32)
            o_ref[rows, cols] = (uz * (f[g] + bs_ref[g])).astype(BF16)


def _sg_out_kernel(u_ref, v_ref, z_ref, lng_ref, lnb_ref, ws_ref, bs_ref, w_ref, x_ref, gate_ref, *rest, tm, final):
    if final:
        fg_ref, o_ref, w_scr, mix_scr = rest
    else:
        o_ref, w_scr, mix_scr = rest
    _round_weight(w_ref, w_scr)
    _sg_mix_kernel(u_ref, v_ref, z_ref, lng_ref, lnb_ref, ws_ref, bs_ref, mix_scr, tm=tm)
    y = jnp.dot(mix_scr[...], w_scr[...], preferred_element_type=F32)
    xn = x_ref[...] + gate_ref[0] * y
    if final:
        xn = xn * lax.rsqrt(jnp.mean(xn * xn, axis=-1, keepdims=True) + RMS_EPS) * fg_ref[...]
    o_ref[...] = xn


def _sg_out(p, ln_g, ln_b, w_s, b_full, w, wl, x2, mod3, final_g=None, *, tm=256):
    n_tok = p.shape[0]
    per_b = SEQ // tm
    tile = lambda c: pl.BlockSpec((tm, D), lambda i: (i, c))
    vec = pl.BlockSpec((1, D), lambda i: (0, 0))
    in_specs = [tile(0), tile(1), tile(2), vec, vec,
                pl.BlockSpec((SG_GROUPS, SG_CHUNK, SG_CHUNK), lambda i: (0, 0, 0)),
                pl.BlockSpec((SG_GROUPS, SG_CHUNK, LANES), lambda i: (0, 0, 0)),
                pl.BlockSpec((None, D, D), lambda i: (wl, 0, 0), pipeline_mode=pl.Buffered(1)),
                tile(0),
                pl.BlockSpec((1, 1, D), lambda i: (i // per_b, 0, 2))]
    args = [p, p, p, ln_g, ln_b, w_s, b_full, w, x2, mod3]
    if final_g is not None:
        in_specs.append(vec)
        args.append(final_g)
    vmem = (6 * D * D + 2 * tm * D + 2 * (3 * 2 + 4 + 4) * tm * D
            + 2 * 4 * SG_GROUPS * SG_CHUNK * (SG_CHUNK + LANES) + 8 * 1024 * 1024)
    return pl.pallas_call(
        functools.partial(_sg_out_kernel, tm=tm, final=final_g is not None),
        out_shape=jax.ShapeDtypeStruct((n_tok, D), F32),
        grid=(n_tok // tm,),
        in_specs=in_specs,
        out_specs=pl.BlockSpec((tm, D), lambda i: (i, 0)),
        scratch_shapes=[pltpu.VMEM((D, D), BF16), pltpu.VMEM((tm, D), BF16)],
        compiler_params=_cparams("arbitrary", vmem=vmem),
        name="sg_out_final" if final_g is not None else "sg_out",
    )(*args)


def _first_half(shape):
    return (lax.broadcasted_iota(jnp.int32, shape, 1) % HEAD) < (HEAD // 2)


def _rope_table_kernel(pos_ref, invf_ref, cos_ref, sin_ref):
    ang = pos_ref[...].astype(F32) * invf_ref[...]
    cos_ref[...] = jnp.cos(ang)
    sin_ref[...] = jnp.where(_first_half(ang.shape), -jnp.sin(ang), jnp.sin(ang))


def _rope(x, cos, sin_signed, first):
    partner = jnp.where(first, pltpu.roll(x, LANES - HEAD // 2, 1), pltpu.roll(x, HEAD // 2, 1))
    return x * cos + partner * sin_signed


def _swa_mix_kernel(sink_ref, p_ref, kvp_ref, cos_ref, sin_ref, cosp_ref, sinp_ref, o_ref):
    i = pl.program_id(0)
    first_key = jnp.where((i % (SEQ // SWA_BLOCK)) != 0, 0, SWA_BLOCK)
    cos, sin_s, cosp, sinp_s = cos_ref[...], sin_ref[...], cosp_ref[...], sinp_ref[...]
    first = _first_half(cos.shape)
    lane = lax.broadcasted_iota(jnp.int32, (2 * SWA_BLOCK, LANES), 1)
    low = lane < HEAD

    qi = lax.broadcasted_iota(jnp.int32, (SWA_BLOCK, 2 * SWA_BLOCK), 0)
    kj = lax.broadcasted_iota(jnp.int32, (SWA_BLOCK, 2 * SWA_BLOCK), 1)
    rel = qi + SWA_BLOCK - kj
    mask = (rel >= 0) & (rel < SWA_BLOCK) & (kj >= first_key)

    k_lo, k_hi, v_lo, v_hi = [], [], [], []
    for kp in range(SWA_KV // 2):
        ksl = pl.ds(kp * LANES, LANES)
        vsl = pl.ds(SWA_KVW + kp * LANES, LANES)
        k2 = jnp.concatenate([_rope(kvp_ref[:, ksl], cosp, sinp_s, first),
                              _rope(p_ref[:, pl.ds(D + kp * LANES, LANES)], cos, sin_s, first)], axis=0)
        v2 = jnp.concatenate([kvp_ref[:, vsl], p_ref[:, pl.ds(D + SWA_KVW + kp * LANES, LANES)]], axis=0)
        k2r = pltpu.roll(k2, HEAD, 1)
        v2r = pltpu.roll(v2, HEAD, 1)
        for sub in range(2):
            src_k, src_kr = (k2, k2r) if sub == 0 else (k2r, k2)
            src_v, src_vr = (v2, v2r) if sub == 0 else (v2r, v2)
            k_lo.append(jnp.where(low, src_k, 0.0).astype(BF16))
            k_hi.append(jnp.where(low, 0.0, src_kr).astype(BF16))
            v_lo.append(jnp.where(low, src_v, 0.0).astype(BF16))
            v_hi.append(jnp.where(low, 0.0, src_vr).astype(BF16))

    zoff = D + 2 * SWA_KVW
    for h0 in range(0, SWA_HEADS, SWA_STAGE):
        hs = range(h0, h0 + SWA_STAGE)
        prs = range(h0 // 2, (h0 + SWA_STAGE) // 2)
        cols = {pr: pl.ds(pr * LANES, LANES) for pr in prs}
        q2 = {pr: (_rope(p_ref[:, cols[pr]], cos, sin_s, first) * (HEAD ** -0.5)).astype(BF16) for pr in prs}
        s = {h: _dot_nt(q2[h // 2], (k_hi if h % 2 else k_lo)[h // 8]) for h in hs}
        s = {h: jnp.where(mask, s[h], -jnp.inf) for h in hs}
        m = {h: jnp.maximum(jnp.max(s[h], axis=-1, keepdims=True), sink_ref[h]) for h in hs}
        e = {h: jnp.exp(s[h] - m[h]) for h in hs}
        denom = {h: jnp.sum(e[h], axis=-1, keepdims=True) + jnp.exp(sink_ref[h] - m[h]) for h in hs}
        prob = {h: (e[h] * (1.0 / denom[h])).astype(BF16) for h in hs}
        o = {h: jnp.dot(prob[h], (v_hi if h % 2 else v_lo)[h // 8], preferred_element_type=F32) for h in hs}
        for pr in prs:
            gate = p_ref[:, pl.ds(zoff + pr * LANES, LANES)]
            o_ref[:, cols[pr]] = ((o[2 * pr] + o[2 * pr + 1]) * gate).astype(BF16)


def _swa_mix(p, cos_t, sin_t, sinks):
    n_tok, ncols = p.shape
    tb = SWA_BLOCK
    kvb = D // (2 * SWA_KVW)
    cur = pl.BlockSpec((tb, LANES), lambda i: (i, 0))
    prev = pl.BlockSpec((tb, LANES), lambda i: (jnp.maximum(i - 1, 0), 0))
    return pl.pallas_call(
        _swa_mix_kernel,
        out_shape=jax.ShapeDtypeStruct((n_tok, D), BF16),
        grid=(n_tok // tb,),
        in_specs=[pl.BlockSpec(memory_space=pltpu.SMEM),
                  pl.BlockSpec((tb, ncols), lambda i: (i, 0)),
                  pl.BlockSpec((tb, 2 * SWA_KVW), lambda i: (jnp.maximum(i - 1, 0), kvb)),
                  cur, cur, prev, prev],
        out_specs=pl.BlockSpec((tb, D), lambda i: (i, 0)),
        compiler_params=_cparams("parallel"),
        name="swa_mix",
    )(sinks, p, p, cos_t, sin_t, cos_t, sin_t)


def _head_sums(xs, seg):
    out = jnp.dot(jnp.concatenate([x.astype(BF16) for x in xs], axis=0), seg, preferred_element_type=F32)
    n = xs[0].shape[0]
    return [out[i * n:(i + 1) * n] for i in range(len(xs))]


def _cumsum_rows(x, tri):
    hi = x.astype(BF16)
    lo = (x - hi.astype(F32)).astype(BF16)
    return jnp.dot(tri, hi, preferred_element_type=F32) + jnp.dot(tri, lo, preferred_element_type=F32)


def _bd(x, same_head):
    t = jnp.concatenate([x] * RW_HG, axis=0)
    return jnp.where(same_head, t, 0.0).astype(BF16)


def _diag_blocks(o, lane_head):
    acc = o[0:HEAD]
    for h in range(1, RW_HG):
        acc = jnp.where(lane_head == h, o[h * HEAD:(h + 1) * HEAD], acc)
    return acc


def _dot(a, b):
    return jnp.dot(a.astype(BF16), b.astype(BF16), preferred_element_type=F32)


def _dot_nt(a, b):
    return lax.dot_general(a, b, (((1,), (1,)), ((), ())), preferred_element_type=F32)


def _dot_tn(a, b):
    return lax.dot_general(a, b, (((0,), (0,)), ((), ())), preferred_element_type=F32)


def _rwkv_groups(r, lw, cum, k, v, kap, bet, h0, masks):
    incl, strict, eye, same_head, lane_head = masks
    c = RW_CHUNK
    gr = range(len(r))
    bd = lambda t: _bd(t, same_head)
    g_t = [jnp.exp(cum[g]) for g in gr]
    g_inv = [jnp.exp(-cum[g]) for g in gr]
    cref = [cum[g][c // 2 - 1:c // 2] for g in gr]
    cend = [cum[g][c - 1:c] for g in gr]
    to_ref = [jnp.exp(-cref[g]) for g in gr]
    r_true = [r[g] * g_t[g] for g in gr]
    kap_true = [kap[g] * (g_t[g] * jnp.exp(-lw[g])) for g in gr]
    r_s = [r_true[g] * to_ref[g] for g in gr]
    kap_s = [kap_true[g] * to_ref[g] for g in gr]
    ginv = [g_inv[g] * jnp.exp(cref[g]) for g in gr]
    eend = [g_inv[g] * jnp.exp(cend[g]) for g in gr]

    s = [_dot_nt(jnp.concatenate([r_s[g], kap_s[g]], axis=0).astype(BF16),
                 jnp.concatenate([bd(k[g] * ginv[g]), bd(bet[g] * ginv[g])], axis=0)) for g in gr]
    a_rk = [jnp.where(incl, s[g][:c, :RW_GW], 0.0) for g in gr]
    a_rb = [jnp.where(incl, s[g][:c, RW_GW:], 0.0) for g in gr]
    a_kk = [jnp.where(strict, s[g][c:, :RW_GW], 0.0) for g in gr]
    a_kb = [jnp.where(strict, s[g][c:, RW_GW:], 0.0) for g in gr]

    x = [eye - a_kb[g] for g in gr]
    p = [_dot(a_kb[g], bd(a_kb[g])) for g in gr]
    akv = [_dot(a_kk[g], bd(v[g])) for g in gr]
    for _ in range(4):
        xp = [_dot(jnp.concatenate([x[g], p[g]], axis=0), bd(p[g])) for g in gr]
        x = [x[g] + xp[g][:c] for g in gr]
        p = [xp[g][c:] for g in gr]
    t_inv = [x[g] + _dot(x[g], bd(p[g])) for g in gr]

    wu = [-_dot(t_inv[g], jnp.concatenate([bd(kap_true[g]), bd(akv[g])], axis=1)) for g in gr]

    hw = [_dot(jnp.concatenate([wu[g][:, :RW_GW], r_true[g], eye * jnp.exp(cend[g])], axis=0), bd(h0[g]))
          for g in gr]
    u = [hw[g][:c] + wu[g][:, RW_GW:] for g in gr]
    y = [hw[g][c:2 * c] + _dot(jnp.concatenate([a_rk[g], a_rb[g]], axis=1),
                               jnp.concatenate([bd(v[g]), bd(u[g])], axis=0)) for g in gr]
    o2 = [_dot_tn(jnp.concatenate([k[g] * eend[g], bet[g] * eend[g]], axis=0).astype(BF16),
                  jnp.concatenate([v[g], u[g]], axis=0).astype(BF16)) for g in gr]
    h_new = [hw[g][2 * c:] + _diag_blocks(o2[g], lane_head) for g in gr]
    return y, h_new


def _rwkv_mix_kernel(p_ref, pp_ref, pl_ref, plp_ref, mu_ref, mul_ref, w0_ref, a0_ref, wl_ref, al_ref,
                     kk_ref, ka_ref, rk_ref, gng_ref, gnb_ref, *rest):
    if len(rest) == 4:
        side_ref, o_ref, side_out_ref, h_scr = rest
        side_out_ref[...] = side_ref[...].astype(BF16)
    else:
        o_ref, h_scr = rest
    ci = pl.program_id(1)
    c = RW_CHUNK

    @pl.when(ci == 0)
    def _():
        h_scr[...] = jnp.zeros_like(h_scr)

    not_first = (ci != 0).astype(F32)
    row = lax.broadcasted_iota(jnp.int32, (c, 1), 0)

    def shifted(cur, prev_ref):
        prev_row = prev_ref[SUBLANES - 1:SUBLANES, :] * not_first
        rolled = pltpu.roll(cur, 1, 0)
        return jnp.where(row == 0, prev_row, rolled)

    lcur = pl_ref[...]
    lcur = lcur + (shifted(lcur, plp_ref) - lcur) * mul_ref[...]
    dw = jnp.tanh(lcur[:, :RW_LP // 2])
    da = lcur[:, RW_LP // 2:]

    ti = lax.broadcasted_iota(jnp.int32, (c, RW_GW), 0)
    li = lax.broadcasted_iota(jnp.int32, (c, RW_GW), 1)
    si = li % HEAD
    lane_head = li // HEAD
    rb = lax.broadcasted_iota(jnp.int32, (RW_GW, RW_GW), 0) // HEAD
    cb = lax.broadcasted_iota(jnp.int32, (RW_GW, RW_GW), 1) // HEAD
    same_head = rb == cb
    seg = same_head.astype(BF16)
    masks = (ti >= si, ti > si, (ti == si).astype(F32), same_head, lane_head)

    gr = range(RW_GROUPS)
    gsl = [pl.ds(g * RW_GW, RW_GW) for g in gr]

    def lerp(off, g):
        cols = pl.ds(off + g * RW_GW, RW_GW)
        cur = p_ref[:, cols]
        return cur + (shifted(cur, pp_ref.at[:, cols]) - cur) * mu_ref[:, cols]

    r = [lerp(0, g) for g in gr]
    k = [lerp(D, g) for g in gr]
    v = [lerp(2 * D, g) for g in gr]
    lw = [-DECAY_SCALE * jax.nn.sigmoid(w0_ref[:, gsl[g]] + _dot(dw, wl_ref[:, gsl[g]])) for g in gr]
    a = [jax.nn.sigmoid(a0_ref[:, gsl[g]] + _dot(da, al_ref[:, gsl[g]])) for g in gr]
    kap = [k[g] * kk_ref[:, gsl[g]] for g in gr]
    k = [k[g] * (1.0 + (a[g] - 1.0) * ka_ref[:, gsl[g]]) for g in gr]
    sums = _head_sums([kap[g] * kap[g] for g in gr] + [r[g] * k[g] * rk_ref[:, gsl[g]] for g in gr], seg)
    kap = [kap[g] / jnp.maximum(jnp.sqrt(sums[g]), 1e-12) for g in gr]
    bonus = sums[RW_GROUPS:]
    bet = [kap[g] * a[g] for g in gr]
    tri = (lax.broadcasted_iota(jnp.int32, (c, c), 0) >= lax.broadcasted_iota(jnp.int32, (c, c), 1)).astype(BF16)
    cum = [_cumsum_rows(lw[g], tri) for g in gr]

    y, h_new = [], []
    for lo in range(0, RW_GROUPS, RW_STAGE):
        sl = slice(lo, lo + RW_STAGE)
        ys, hs = _rwkv_groups(r[sl], lw[sl], cum[sl], k[sl], v[sl], kap[sl], bet[sl],
                              [h_scr[g] for g in range(lo, lo + RW_STAGE)], masks)
        y += ys
        h_new += hs
    for g in gr:
        h_scr[g] = h_new[g]

    ysum = _head_sums(y, seg)
    dlt = [y[g] - ysum[g] * (1.0 / HEAD) for g in gr]
    dsum = _head_sums([dlt[g] * dlt[g] for g in gr], seg)
    for g in gr:
        rstd = lax.rsqrt(dsum[g] * (1.0 / HEAD) + GN_EPS)
        yn = dlt[g] * rstd * gng_ref[:, gsl[g]] + gnb_ref[:, gsl[g]] + bonus[g] * v[g]
        o_ref[:, gsl[g]] = (yn * jax.nn.silu(lerp(3 * D, g))).astype(BF16)


def _rwkv_mix(p, p_lora, mu, mu_l, w0, a0, w_lora, a_lora, k_k, k_a, r_k, gn_g, gn_b, side=None):
    n_tok = p.shape[0]
    c = RW_CHUNK
    nc = SEQ // c
    ncol = p.shape[1]
    nl = p_lora.shape[1]
    row = lambda b, ci: (b * nc + ci, 0)
    prev = lambda b, ci: (jnp.maximum((b * nc + ci) * (c // SUBLANES) - 1, 0), 0)
    vec = pl.BlockSpec((1, D), lambda b, ci: (0, 0))
    lora = pl.BlockSpec((RW_LP // 2, D), lambda b, ci: (0, 0))
    in_specs = [pl.BlockSpec((c, ncol), row),
                pl.BlockSpec((SUBLANES, ncol), prev),
                pl.BlockSpec((c, nl), row),
                pl.BlockSpec((SUBLANES, nl), prev),
                pl.BlockSpec((1, ncol), lambda b, ci: (0, 0)),
                pl.BlockSpec((1, nl), lambda b, ci: (0, 0)),
                vec, vec, lora, lora,
                vec, vec, vec, vec, vec]
    args = [p, p, p_lora, p_lora, mu, mu_l, w0, a0, w_lora, a_lora, k_k, k_a, r_k, gn_g, gn_b]
    out_shape = jax.ShapeDtypeStruct((n_tok, D), BF16)
    out_specs = pl.BlockSpec((c, D), row)
    if side is not None:
        side_w, sl = side
        steps = (n_tok // SEQ) * nc
        _, srows, scols = side_w.shape
        srow = srows // steps
        assert srows % steps == 0 and srow % 16 == 0
        in_specs.append(pl.BlockSpec((None, srow, scols), lambda b, ci: (sl, b * nc + ci, 0)))
        args.append(side_w)
        out_shape = (out_shape, jax.ShapeDtypeStruct((srows, scols), BF16))
        out_specs = (out_specs, pl.BlockSpec((srow, scols), row))
    return pl.pallas_call(
        _rwkv_mix_kernel,
        out_shape=out_shape,
        grid=(n_tok // SEQ, nc),
        in_specs=in_specs,
        out_specs=out_specs,
        scratch_shapes=[pltpu.VMEM((RW_GROUPS, HEAD, RW_GW), F32)],
        compiler_params=_cparams("parallel", "arbitrary"),
        name="rwkv_mix",
    )(*args)


def kernel(x, c, positions, norm_g, mod_w, mod_b, final_norm_g, sg_w_in, sg_w_out, sg_ln_g, sg_ln_b, sg_w_spatial,
           sg_b_spatial, swa_w_in, swa_w_out, swa_sinks, rwkv_w_in, rwkv_w_out, rwkv_mu, rwkv_w0, rwkv_w_lora,
           rwkv_a0, rwkv_a_lora, rwkv_k_k, rwkv_k_a, rwkv_r_k, rwkv_gn_g, rwkv_gn_b):
    batch, seq, d = x.shape
    assert (seq, d) == (SEQ, D) and norm_g.shape[0] == DEPTH
    n_tok = batch * seq
    x2 = x.reshape(n_tok, D)
    c8 = jnp.zeros((SUBLANES, D), F32).at[:batch].set(c)
    pos2 = positions.reshape(n_tok, 1)
    half = HEAD // 2
    inv_freq = ROPE_THETA ** (-jnp.arange(half, dtype=F32) / half)
    invf = jnp.tile(inv_freq, LANES // half).reshape(1, LANES)
    alone = min(i + 1 for i in range(DEPTH) if i % 3 != 0)
    assert all(i % 3 != 0 for i in range(alone - 1, DEPTH - 1))
    mod, cos_t, sin_t = _modulation(c8, mod_w, mod_b, pos2, invf, alone)
    c_lanes = jnp.broadcast_to(c[:, :, None], (batch, D, LANES))
    mod_b3 = mod_b.reshape(DEPTH, 1, 3 * D)
    mod8 = None
    row = lambda a: a.reshape(1, -1)

    def act_rows(spans):
        rows = [jnp.concatenate([jnp.full((w,), val[t], F32) for w, *val in spans]) for t in range(3)]
        return jnp.zeros((SUBLANES, rows[0].shape[0]), F32).at[:3].set(jnp.stack(rows))

    gelu_span = (GELU_A, GELU_B, 1.0)
    silu_span = (0.5, 0.0, 1.0)
    sg_first_bf, rwkv_in_bf = sg_w_in[:1].astype(BF16), rwkv_w_in.astype(BF16)
    w_out = (sg_w_out, swa_w_out, rwkv_w_out)
    side_bf = None
    for i in range(DEPTH):
        kind, j = i % 3, i // 3
        mod3 = (mod[i] if i < alone else mod8)[:batch].reshape(batch, 1, 3 * D)
        g = row(norm_g[i])
        has_next = i + 1 < DEPTH
        final_g = None if has_next else row(final_norm_g)
        if kind == 0:
            act = act_rows([(2 * D, *gelu_span), (D, *silu_span)])
            w = sg_first_bf if i == 0 else side_bf[None]
            p = _in_proj_act(x2, g, mod3, w, 0, 3 * D, act, False, BF16, (swa_w_in, j) if has_next else None)
            p, side_bf = p if has_next else (p, None)
            x2 = _sg_out(p, row(sg_ln_g[j]), row(sg_ln_b[j]), sg_w_spatial[j],
                         jnp.broadcast_to(sg_b_spatial[j][:, :, None], (SG_GROUPS, SG_CHUNK, LANES)),
                         sg_w_out, j, x2, mod3, final_g)
            continue
        elif kind == 1:
            act = act_rows([(D + 2 * SWA_KVW, 0.0, 0.0, 0.0), (D, *silu_span)])
            p = _in_proj_act(x2, g, mod3, side_bf[None], 0, 2 * D + 2 * SWA_KVW, act, True, F32)
            mix = _swa_mix(p, cos_t, sin_t, swa_sinks[j])
        else:
            lp = RW_LP // 2 - RW_LORA
            w_in = rwkv_in_bf[j]
            w_l = jnp.concatenate([jnp.pad(w_in[:, 4 * D:4 * D + RW_LORA], ((0, 0), (0, lp))),
                                   jnp.pad(w_in[:, 4 * D + RW_LORA:], ((0, 0), (0, lp)))], axis=1)
            p, p_lora = _in_proj(x2, g, mod3, rwkv_in_bf, j, 4 * D, w_l)
            mu = rwkv_mu[j]
            mu_l = jnp.concatenate([jnp.pad(mu[4 * D:4 * D + RW_LORA], (0, lp)), jnp.pad(mu[4 * D + RW_LORA:], (0, lp))])
            wl_pad = jnp.pad(rwkv_w_lora[j], ((0, lp), (0, 0))).astype(BF16)
            al_pad = jnp.pad(rwkv_a_lora[j], ((0, lp), (0, 0))).astype(BF16)
            mix = _rwkv_mix(p, p_lora, row(mu[:4 * D]), row(mu_l), row(rwkv_w0[j]),
                            row(rwkv_a0[j]), wl_pad, al_pad, row(rwkv_k_k[j]),
                            row(rwkv_k_a[j]), row(rwkv_r_k[j]), row(rwkv_gn_g[j]), row(rwkv_gn_b[j]),
                            (sg_w_in, j + 1) if has_next else None)
            mix, side_bf = mix if has_next else (mix, None)
        x2 = _out_proj(mix, w_out[kind], j, x2, mod3, final_g, (c_lanes, mod_w, mod_b3, i + 1) if has_next else None)
        x2, mod8 = x2 if has_next else (x2, None)
    return x2.reshape(batch, seq, D)
```

```python
import functools
import math

import jax
import jax.numpy as jnp
from jax import lax
from jax.experimental import pallas as pl
from jax.experimental.pallas import tpu as pltpu

F32 = jnp.float32
BF16 = jnp.bfloat16

D = 2048
SEQ = 4096
DEPTH = 4
HEAD = 64
LANES = 128
SUBLANES = 8
SG_CHUNK = 128
SG_GROUPS = 16
SWA_HEADS = D // HEAD
SWA_KV = SWA_HEADS // 8
SWA_KVW = SWA_KV * HEAD
SWA_BLOCK = 128
SWA_STAGE = 16
ROPE_THETA = 10000.0
RW_LORA = 96
RW_LP = 256
RW_CHUNK = 64
RW_HG = 4
RW_GW = RW_HG * HEAD
RW_GROUPS = D // RW_GW
RW_STAGE = 8
DECAY_SCALE = math.exp(-0.5)
GN_EPS = 64e-5
RMS_EPS = 1e-6
LN_EPS = 1e-5

VMEM_LIMIT = 48 * 1024 * 1024


def _cparams(*sem, vmem=VMEM_LIMIT):
    return pltpu.CompilerParams(dimension_semantics=sem, vmem_limit_bytes=vmem)


def _mod_kernel(c_ref, w_ref, b_ref, pos_ref, invf_ref, o_ref, cos_ref, sin_ref):
    cond = jax.nn.silu(c_ref[...]).astype(BF16)
    o_ref[0] = jnp.dot(cond, w_ref[0].astype(BF16), preferred_element_type=F32) + b_ref[0]
    _rope_table_kernel(pos_ref, invf_ref, cos_ref, sin_ref)


def _modulation(c8, mod_w, mod_b, pos2, invf, layers):
    tn = 1536
    n = mod_w.shape[2]
    nj = n // tn
    n_tok = pos2.shape[0]
    rows = n_tok // (layers * nj)
    assert n % tn == 0 and rows * layers * nj == n_tok and rows % SUBLANES == 0
    table = pl.BlockSpec((rows, LANES), lambda l, j: (l * nj + j, 0))
    return pl.pallas_call(
        _mod_kernel,
        out_shape=(jax.ShapeDtypeStruct((layers, SUBLANES, n), F32),
                   jax.ShapeDtypeStruct((n_tok, LANES), F32), jax.ShapeDtypeStruct((n_tok, LANES), F32)),
        grid=(layers, nj),
        in_specs=[pl.BlockSpec((SUBLANES, D), lambda l, j: (0, 0)),
                  pl.BlockSpec((1, D, tn), lambda l, j: (l, 0, j)),
                  pl.BlockSpec((1, 1, tn), lambda l, j: (l, 0, j)),
                  pl.BlockSpec((rows, 1), lambda l, j: (l * nj + j, 0)),
                  pl.BlockSpec((1, LANES), lambda l, j: (0, 0))],
        out_specs=(pl.BlockSpec((1, SUBLANES, tn), lambda l, j: (l, 0, j)), table, table),
        compiler_params=_cparams("parallel", "parallel"),
        name="modulation",
    )(c8, mod_w, mod_b.reshape(DEPTH, 1, n), pos2, invf)


NORM_ROWS = 128


def _row_rsqrt(x_ref, rs_scr):
    def body(i, carry):
        rows = pl.ds(pl.multiple_of(i * NORM_ROWS, NORM_ROWS), NORM_ROWS)
        x = x_ref[rows, :]
        rs_scr[rows, :] = lax.rsqrt(jnp.mean(x * x, axis=-1, keepdims=True) + RMS_EPS)
        return carry
    lax.fori_loop(0, x_ref.shape[0] // NORM_ROWS, body, 0, unroll=True)


def _modulated(x_ref, rs_scr, g_ref, sc_ref, sh_ref):
    gain = g_ref[...] * (1.0 + sc_ref[0])
    return (x_ref[...] * rs_scr[...] * gain + sh_ref[0]).astype(BF16)


GELU_A = math.sqrt(2.0 / math.pi)
GELU_B = GELU_A * 0.044715
ACT_COLS = 256


def _in_proj_act_kernel(x_ref, g_ref, sc_ref, sh_ref, w_ref, act_ref, *rest, select):
    if len(rest) == 4:
        side_ref, o_ref, side_out_ref, rs_scr = rest
        side_out_ref[...] = side_ref[...].astype(BF16)
    else:
        o_ref, rs_scr = rest

    @pl.when(pl.program_id(1) == 0)
    def _():
        _row_rsqrt(x_ref, rs_scr)

    h = _modulated(x_ref, rs_scr, g_ref, sc_ref, sh_ref)
    for n in range(o_ref.shape[1] // ACT_COLS):
        cols = pl.ds(n * ACT_COLS, ACT_COLS)
        p = jnp.dot(h, w_ref[:, cols], preferred_element_type=F32)
        gated = (0.5 * p) * (1.0 + jnp.tanh(p * (act_ref[0:1, cols] + act_ref[1:2, cols] * (p * p))))
        o_ref[:, cols] = (jnp.where(act_ref[2:3, cols] > 0.0, gated, p) if select else gated).astype(o_ref.dtype)


def _in_proj_lora_kernel(x_ref, g_ref, sc_ref, sh_ref, w_ref, wl_ref, o_ref, ol_ref, rs_scr):
    @pl.when(pl.program_id(1) == 0)
    def _():
        _row_rsqrt(x_ref, rs_scr)
        ol_ref[...] = jnp.dot(_modulated(x_ref, rs_scr, g_ref, sc_ref, sh_ref), wl_ref[...],
                              preferred_element_type=F32)

    h = _modulated(x_ref, rs_scr, g_ref, sc_ref, sh_ref)
    o_ref[...] = jnp.dot(h, w_ref[...], preferred_element_type=F32)


SIDE_ROWS = 128


def _in_proj_act(x2, g, mod3, w_bf, wl, ncols, act, select, out_dtype, side=None, *, tm=1024):
    n_tok = x2.shape[0]
    tn = 2048 if (out_dtype == BF16 and ncols % 2048 == 0) else (1024 if ncols % 1024 == 0 else 1536)
    assert ncols % tn == 0
    per_b = SEQ // tm
    nj = ncols // tn
    in_specs = [pl.BlockSpec((tm, D), lambda i, j: (i, 0)),
                pl.BlockSpec((1, D), lambda i, j: (0, 0)),
                pl.BlockSpec((1, 1, D), lambda i, j: (i // per_b, 0, 1)),
                pl.BlockSpec((1, 1, D), lambda i, j: (i // per_b, 0, 0)),
                pl.BlockSpec((None, D, tn), lambda i, j: (wl, 0, j)),
                pl.BlockSpec((SUBLANES, tn), lambda i, j: (0, j))]
    out_shape = jax.ShapeDtypeStruct((n_tok, ncols), out_dtype)
    out_specs = pl.BlockSpec((tm, tn), lambda i, j: (i, j))
    args = [x2, g, mod3, mod3, w_bf, act]
    osize = jnp.dtype(out_dtype).itemsize
    vmem = 2 * (4 * tm * D + 2 * D * tn + osize * tm * tn) + 8 * 1024 * 1024
    if side is not None:
        side_w, sl = side
        _, srows, scols = side_w.shape
        nblk = srows // SIDE_ROWS
        assert srows % SIDE_ROWS == 0 and nblk <= (n_tok // tm) * nj
        blk = lambda i, j: jnp.minimum(i * nj + j, nblk - 1)
        in_specs.append(pl.BlockSpec((None, SIDE_ROWS, scols), lambda i, j: (sl, blk(i, j), 0)))
        args.append(side_w)
        out_shape = (out_shape, jax.ShapeDtypeStruct((srows, scols), BF16))
        out_specs = (out_specs, pl.BlockSpec((SIDE_ROWS, scols), lambda i, j: (blk(i, j), 0)))
        vmem += 2 * (4 + 2) * SIDE_ROWS * scols
    return pl.pallas_call(
        functools.partial(_in_proj_act_kernel, select=select),
        out_shape=out_shape,
        grid=(n_tok // tm, nj),
        in_specs=in_specs,
        out_specs=out_specs,
        scratch_shapes=[pltpu.VMEM((tm, 1), F32)],
        compiler_params=_cparams("arbitrary", "arbitrary", vmem=vmem),
        name="in_proj_act",
    )(*args)


def _in_proj(x2, g, mod3, w_bf, wl, ncols, w_lora_bf, *, tm=1024):
    n_tok = x2.shape[0]
    tn = 2048
    assert ncols % tn == 0
    per_b = SEQ // tm
    x_spec = pl.BlockSpec((tm, D), lambda i, j: (i, 0))
    g_spec = pl.BlockSpec((1, D), lambda i, j: (0, 0))
    sh_spec = pl.BlockSpec((1, 1, D), lambda i, j: (i // per_b, 0, 0))
    sc_spec = pl.BlockSpec((1, 1, D), lambda i, j: (i // per_b, 0, 1))
    w_spec = pl.BlockSpec((None, D, tn), lambda i, j: (wl, 0, j))
    o_spec = pl.BlockSpec((tm, tn), lambda i, j: (i, j))
    grid = (n_tok // tm, ncols // tn)
    scratch = [pltpu.VMEM((tm, 1), F32)]
    nl = w_lora_bf.shape[1]
    vmem = 2 * (4 * tm * D + 2 * D * tn + 4 * tm * tn + 2 * D * nl + 4 * tm * nl) + 4 * 1024 * 1024
    return pl.pallas_call(
        _in_proj_lora_kernel,
        out_shape=(jax.ShapeDtypeStruct((n_tok, ncols), F32), jax.ShapeDtypeStruct((n_tok, nl), F32)),
        grid=grid,
        in_specs=[x_spec, g_spec, sc_spec, sh_spec, w_spec, pl.BlockSpec((D, nl), lambda i, j: (0, 0))],
        out_specs=(o_spec, pl.BlockSpec((tm, nl), lambda i, j: (i, 0))),
        scratch_shapes=scratch,
        compiler_params=_cparams("parallel", "arbitrary", vmem=vmem),
        name="in_proj_lora",
    )(x2, g, mod3, mod3, w_bf, w_lora_bf)


CAST_ROWS = 256


def _round_weight(w_ref, w_scr):
    @pl.when(pl.program_id(0) == 0)
    def _():
        def body(i, carry):
            rows = pl.ds(pl.multiple_of(i * CAST_ROWS, CAST_ROWS), CAST_ROWS)
            w_scr[rows, :] = w_ref[rows, :].astype(BF16)
            return carry
        lax.fori_loop(0, w_ref.shape[0] // CAST_ROWS, body, 0, unroll=True)


def _out_proj_kernel(m_ref, w_ref, x_ref, gate_ref, o_ref, w_scr):
    _round_weight(w_ref, w_scr)
    y = jnp.dot(m_ref[...], w_scr[...], preferred_element_type=F32)
    o_ref[...] = x_ref[...] + gate_ref[0] * y


MOD_ROWS = 16


def _out_proj_mod_kernel(m_ref, w_ref, x_ref, gate_ref, c_ref, mw_ref, mb_ref, o_ref, mo_ref, w_scr, cond_scr):
    @pl.when(pl.program_id(0) == 0)
    def _():
        cond_scr[...] = jax.nn.silu(c_ref[...])

    _out_proj_kernel(m_ref, w_ref, x_ref, gate_ref, o_ref, w_scr)
    nb, cols = c_ref.shape[0], mo_ref.shape[1]
    tiles = cols // LANES

    def body(r, acc):
        rows = pl.ds(pl.multiple_of(r * MOD_ROWS, MOD_ROWS), MOD_ROWS)
        w = [mw_ref[rows, pl.ds(t * LANES, LANES)] for t in range(tiles)]
        cond = [cond_scr[b, rows, :] for b in range(nb)]
        return tuple(acc[b * tiles + t] + w[t] * cond[b] for b in range(nb) for t in range(tiles))

    zero = jnp.zeros((MOD_ROWS, LANES), F32)
    acc = lax.fori_loop(0, D // MOD_ROWS, body, (zero,) * (nb * tiles), unroll=True)
    rowid = lax.broadcasted_iota(jnp.int32, (SUBLANES, cols), 0)
    out = jnp.zeros((SUBLANES, cols), F32)
    for b in range(nb):
        sums = [jnp.sum(acc[b * tiles + t], axis=0, keepdims=True) for t in range(tiles)]
        out = jnp.where(rowid == b, jnp.concatenate(sums, axis=1) + mb_ref[...], out)
    mo_ref[...] = out


def _out_proj_final_kernel(m_ref, w_ref, x_ref, gate_ref, fg_ref, o_ref, w_scr):
    _round_weight(w_ref, w_scr)
    y = jnp.dot(m_ref[...], w_scr[...], preferred_element_type=F32)
    xn = x_ref[...] + gate_ref[0] * y
    o_ref[...] = xn * lax.rsqrt(jnp.mean(xn * xn, axis=-1, keepdims=True) + RMS_EPS) * fg_ref[...]


def _out_proj(mix, w, wl, x2, mod3, final_g=None, next_mod=None, *, tm=512):
    n_tok = x2.shape[0]
    per_b = SEQ // tm
    steps = n_tok // tm
    in_specs = [pl.BlockSpec((tm, D), lambda i: (i, 0)),
                pl.BlockSpec((None, D, D), lambda i: (wl, 0, 0), pipeline_mode=pl.Buffered(1)),
                pl.BlockSpec((tm, D), lambda i: (i, 0)),
                pl.BlockSpec((1, 1, D), lambda i: (i // per_b, 0, 2))]
    args = [mix, w, x2, mod3]
    kern, name = _out_proj_kernel, "out_proj"
    out_shape = jax.ShapeDtypeStruct((n_tok, D), F32)
    out_specs = pl.BlockSpec((tm, D), lambda i: (i, 0))
    scratch = [pltpu.VMEM((D, D), BF16)]
    vmem = 6 * D * D + 2 * (2 + 4 + 4) * tm * D + 8 * 1024 * 1024
    if final_g is not None:
        in_specs.append(pl.BlockSpec((1, D), lambda i: (0, 0)))
        args.append(final_g)
        kern, name = _out_proj_final_kernel, "out_proj_final"
    elif next_mod is not None:
        cb, mod_w, mod_b, layer = next_mod
        n = mod_w.shape[2]
        cols = n // steps
        assert cols * steps == n and cols % LANES == 0
        in_specs += [pl.BlockSpec(cb.shape, lambda i: (0, 0, 0), pipeline_mode=pl.Buffered(1)),
                     pl.BlockSpec((None, D, cols), lambda i: (layer, 0, i)),
                     pl.BlockSpec((None, 1, cols), lambda i: (layer, 0, i))]
        args += [cb, mod_w, mod_b]
        kern, name = _out_proj_mod_kernel, "out_proj_mod"
        out_shape = (out_shape, jax.ShapeDtypeStruct((SUBLANES, n), F32))
        out_specs = (out_specs, pl.BlockSpec((SUBLANES, cols), lambda i: (0, i)))
        scratch.append(pltpu.VMEM(cb.shape, F32))
        vmem += 2 * 4 * cb.size + 2 * 4 * D * cols
    return pl.pallas_call(
        kern,
        out_shape=out_shape,
        grid=(steps,),
        in_specs=in_specs,
        out_specs=out_specs,
        scratch_shapes=scratch,
        compiler_params=_cparams("arbitrary", vmem=vmem),
        name=name,
    )(*args)


def _sg_mix_kernel(u_ref, v_ref, z_ref, lng_ref, lnb_ref, ws_ref, bs_ref, o_ref, *, tm):
    ti = lax.broadcasted_iota(jnp.int32, (SG_CHUNK, SG_CHUNK), 0)
    si = lax.broadcasted_iota(jnp.int32, (SG_CHUNK, SG_CHUNK), 1)
    causal = ti >= si
    ws = [jnp.where(causal, ws_ref[g], 0.0).astype(BF16) for g in range(SG_GROUPS)]
    for c in range(tm // SG_CHUNK):
        rows = pl.ds(c * SG_CHUNK, SG_CHUNK)
        v = v_ref[rows, :].astype(F32)
        mean = jnp.mean(v, axis=-1, keepdims=True)
        d = v - mean
        var = jnp.mean(d * d, axis=-1, keepdims=True)
        vn = (d * lax.rsqrt(var + LN_EPS) * lng_ref[...] + lnb_ref[...]).astype(BF16)
        f = [jnp.dot(ws[g], vn[:, g * LANES:(g + 1) * LANES], preferred_element_type=F32) for g in range(SG_GROUPS)]
        for g in range(SG_GROUPS):
            cols = pl.ds(g * LANES, LANES)
            uz = u_ref[rows, cols].astype(F32) * z_ref[rows, cols].astype(F32)
            o_ref[rows, cols] = (uz * (f[g] + bs_ref[g])).astype(BF16)


def _sg_out_kernel(u_ref, v_ref, z_ref, lng_ref, lnb_ref, ws_ref, bs_ref, w_ref, x_ref, gate_ref, *rest, tm, final):
    if final:
        fg_ref, o_ref, w_scr, mix_scr = rest
    else:
        o_ref, w_scr, mix_scr = rest
    _round_weight(w_ref, w_scr)
    _sg_mix_kernel(u_ref, v_ref, z_ref, lng_ref, lnb_ref, ws_ref, bs_ref, mix_scr, tm=tm)
    y = jnp.dot(mix_scr[...], w_scr[...], preferred_element_type=F32)
    xn = x_ref[...] + gate_ref[0] * y
    if final:
        xn = xn * lax.rsqrt(jnp.mean(xn * xn, axis=-1, keepdims=True) + RMS_EPS) * fg_ref[...]
    o_ref[...] = xn


def _sg_out(p, ln_g, ln_b, w_s, b_full, w, wl, x2, mod3, final_g=None, *, tm=256):
    n_tok = p.shape[0]
    per_b = SEQ // tm
    tile = lambda c: pl.BlockSpec((tm, D), lambda i: (i, c))
    vec = pl.BlockSpec((1, D), lambda i: (0, 0))
    in_specs = [tile(0), tile(1), tile(2), vec, vec,
                pl.BlockSpec((SG_GROUPS, SG_CHUNK, SG_CHUNK), lambda i: (0, 0, 0)),
                pl.BlockSpec((SG_GROUPS, SG_CHUNK, LANES), lambda i: (0, 0, 0)),
                pl.BlockSpec((None, D, D), lambda i: (wl, 0, 0), pipeline_mode=pl.Buffered(1)),
                tile(0),
                pl.BlockSpec((1, 1, D), lambda i: (i // per_b, 0, 2))]
    args = [p, p, p, ln_g, ln_b, w_s, b_full, w, x2, mod3]
    if final_g is not None:
        in_specs.append(vec)
        args.append(final_g)
    vmem = (6 * D * D + 2 * tm * D + 2 * (3 * 2 + 4 + 4) * tm * D
            + 2 * 4 * SG_GROUPS * SG_CHUNK * (SG_CHUNK + LANES) + 8 * 1024 * 1024)
    return pl.pallas_call(
        functools.partial(_sg_out_kernel, tm=tm, final=final_g is not None),
        out_shape=jax.ShapeDtypeStruct((n_tok, D), F32),
        grid=(n_tok // tm,),
        in_specs=in_specs,
        out_specs=pl.BlockSpec((tm, D), lambda i: (i, 0)),
        scratch_shapes=[pltpu.VMEM((D, D), BF16), pltpu.VMEM((tm, D), BF16)],
        compiler_params=_cparams("arbitrary", vmem=vmem),
        name="sg_out_final" if final_g is not None else "sg_out",
    )(*args)


def _first_half(shape):
    return (lax.broadcasted_iota(jnp.int32, shape, 1) % HEAD) < (HEAD // 2)


def _rope_table_kernel(pos_ref, invf_ref, cos_ref, sin_ref):
    ang = pos_ref[...].astype(F32) * invf_ref[...]
    cos_ref[...] = jnp.cos(ang)
    sin_ref[...] = jnp.where(_first_half(ang.shape), -jnp.sin(ang), jnp.sin(ang))


def _rope(x, cos, sin_signed, first):
    partner = jnp.where(first, pltpu.roll(x, LANES - HEAD // 2, 1), pltpu.roll(x, HEAD // 2, 1))
    return x * cos + partner * sin_signed


def _swa_mix_kernel(sink_ref, p_ref, kvp_ref, cos_ref, sin_ref, cosp_ref, sinp_ref, *rest):
    if len(rest) == 3:
        wt_ref, o_ref, wo_ref = rest
        wo_ref[...] = wt_ref[...].T.astype(BF16)
    else:
        o_ref, = rest
    i = pl.program_id(0)
    first_key = jnp.where((i % (SEQ // SWA_BLOCK)) != 0, 0, SWA_BLOCK)
    cos, sin_s, cosp, sinp_s = cos_ref[...], sin_ref[...], cosp_ref[...], sinp_ref[...]
    first = _first_half(cos.shape)
    lane = lax.broadcasted_iota(jnp.int32, (2 * SWA_BLOCK, LANES), 1)
    low = lane < HEAD

    qi = lax.broadcasted_iota(jnp.int32, (SWA_BLOCK, 2 * SWA_BLOCK), 0)
    kj = lax.broadcasted_iota(jnp.int32, (SWA_BLOCK, 2 * SWA_BLOCK), 1)
    rel = qi + SWA_BLOCK - kj
    mask = (rel >= 0) & (rel < SWA_BLOCK) & (kj >= first_key)

    k_lo, k_hi, v_lo, v_hi = [], [], [], []
    for kp in range(SWA_KV // 2):
        ksl = pl.ds(kp * LANES, LANES)
        vsl = pl.ds(SWA_KVW + kp * LANES, LANES)
        k2 = jnp.concatenate([_rope(kvp_ref[:, ksl], cosp, sinp_s, first),
                              _rope(p_ref[:, pl.ds(D + kp * LANES, LANES)], cos, sin_s, first)], axis=0)
        v2 = jnp.concatenate([kvp_ref[:, vsl], p_ref[:, pl.ds(D + SWA_KVW + kp * LANES, LANES)]], axis=0)
        k2r = pltpu.roll(k2, HEAD, 1)
        v2r = pltpu.roll(v2, HEAD, 1)
        for sub in range(2):
            src_k, src_kr = (k2, k2r) if sub == 0 else (k2r, k2)
            src_v, src_vr = (v2, v2r) if sub == 0 else (v2r, v2)
            k_lo.append(jnp.where(low, src_k, 0.0).astype(BF16))
            k_hi.append(jnp.where(low, 0.0, src_kr).astype(BF16))
            v_lo.append(jnp.where(low, src_v, 0.0).astype(BF16))
            v_hi.append(jnp.where(low, 0.0, src_vr).astype(BF16))

    zoff = D + 2 * SWA_KVW
    for h0 in range(0, SWA_HEADS, SWA_STAGE):
        hs = range(h0, h0 + SWA_STAGE)
        prs = range(h0 // 2, (h0 + SWA_STAGE) // 2)
        cols = {pr: pl.ds(pr * LANES, LANES) for pr in prs}
        q2 = {pr: (_rope(p_ref[:, cols[pr]], cos, sin_s, first) * (HEAD ** -0.5)).astype(BF16) for pr in prs}
        s = {h: _dot_nt(q2[h // 2], (k_hi if h % 2 else k_lo)[h // 8]) for h in hs}
        s = {h: jnp.where(mask, s[h], -jnp.inf) for h in hs}
        m = {h: jnp.maximum(jnp.max(s[h], axis=-1, keepdims=True), sink_ref[h]) for h in hs}
        e = {h: jnp.exp(s[h] - m[h]) for h in hs}
        denom = {h: jnp.sum(e[h], axis=-1, keepdims=True) + jnp.exp(sink_ref[h] - m[h]) for h in hs}
        prob = {h: (e[h] * (1.0 / denom[h])).astype(BF16) for h in hs}
        o = {h: jnp.dot(prob[h], (v_hi if h % 2 else v_lo)[h // 8], preferred_element_type=F32) for h in hs}
        for pr in prs:
            gate = p_ref[:, pl.ds(zoff + pr * LANES, LANES)]
            o_ref[:, cols[pr]] = ((o[2 * pr] + o[2 * pr + 1]) * gate).astype(BF16)


def _swa_mix(p, cos_t, sin_t, sinks, side=None):
    n_tok, ncols = p.shape
    tb = SWA_BLOCK
    steps = n_tok // tb
    kvb = D // (2 * SWA_KVW)
    cur = pl.BlockSpec((tb, LANES), lambda i: (i, 0))
    prev = pl.BlockSpec((tb, LANES), lambda i: (jnp.maximum(i - 1, 0), 0))
    in_specs = [pl.BlockSpec(memory_space=pltpu.SMEM),
                pl.BlockSpec((tb, ncols), lambda i: (i, 0)),
                pl.BlockSpec((tb, 2 * SWA_KVW), lambda i: (jnp.maximum(i - 1, 0), kvb)),
                cur, cur, prev, prev]
    args = [sinks, p, p, cos_t, sin_t, cos_t, sin_t]
    out_shape = jax.ShapeDtypeStruct((n_tok, D), BF16)
    out_specs = pl.BlockSpec((tb, D), lambda i: (i, 0))
    if side is not None:
        w_t, layer = side
        in_specs.append(pl.BlockSpec((None, LANES, D), lambda i: (layer, i, 0)))
        args.append(w_t)
        out_shape = (out_shape, jax.ShapeDtypeStruct((D, steps * LANES), BF16))
        out_specs = (out_specs, pl.BlockSpec((D, LANES), lambda i: (0, i)))
    return pl.pallas_call(
        _swa_mix_kernel,
        out_shape=out_shape,
        grid=(steps,),
        in_specs=in_specs,
        out_specs=out_specs,
        compiler_params=_cparams("parallel"),
        name="swa_mix",
    )(*args)


def _head_sums(xs, seg):
    out = jnp.dot(jnp.concatenate([x.astype(BF16) for x in xs], axis=0), seg, preferred_element_type=F32)
    n = xs[0].shape[0]
    return [out[i * n:(i + 1) * n] for i in range(len(xs))]


def _cumsum_rows(x, tri):
    hi = x.astype(BF16)
    lo = (x - hi.astype(F32)).astype(BF16)
    return jnp.dot(tri, hi, preferred_element_type=F32) + jnp.dot(tri, lo, preferred_element_type=F32)


def _bd(x, same_head):
    t = jnp.concatenate([x] * RW_HG, axis=0)
    return jnp.where(same_head, t, 0.0).astype(BF16)


def _diag_blocks(o, lane_head):
    acc = o[0:HEAD]
    for h in range(1, RW_HG):
        acc = jnp.where(lane_head == h, o[h * HEAD:(h + 1) * HEAD], acc)
    return acc


def _dot(a, b):
    return jnp.dot(a.astype(BF16), b.astype(BF16), preferred_element_type=F32)


def _dot_nt(a, b):
    return lax.dot_general(a, b, (((1,), (1,)), ((), ())), preferred_element_type=F32)


def _dot_tn(a, b):
    return lax.dot_general(a, b, (((0,), (0,)), ((), ())), preferred_element_type=F32)


def _rwkv_groups(r, lw, cum, k, v, kap, bet, h0, masks):
    incl, strict, eye, same_head, lane_head = masks
    c = RW_CHUNK
    gr = range(len(r))
    bd = lambda t: _bd(t, same_head)
    g_t = [jnp.exp(cum[g]) for g in gr]
    g_inv = [jnp.exp(-cum[g]) for g in gr]
    cref = [cum[g][c // 2 - 1:c // 2] for g in gr]
    cend = [cum[g][c - 1:c] for g in gr]
    to_ref = [jnp.exp(-cref[g]) for g in gr]
    r_true = [r[g] * g_t[g] for g in gr]
    kap_true = [kap[g] * (g_t[g] * jnp.exp(-lw[g])) for g in gr]
    r_s = [r_true[g] * to_ref[g] for g in gr]
    kap_s = [kap_true[g] * to_ref[g] for g in gr]
    ginv = [g_inv[g] * jnp.exp(cref[g]) for g in gr]
    eend = [g_inv[g] * jnp.exp(cend[g]) for g in gr]

    s = [_dot_nt(jnp.concatenate([r_s[g], kap_s[g]], axis=0).astype(BF16),
                 jnp.concatenate([bd(k[g] * ginv[g]), bd(bet[g] * ginv[g])], axis=0)) for g in gr]
    a_rk = [jnp.where(incl, s[g][:c, :RW_GW], 0.0) for g in gr]
    a_rb = [jnp.where(incl, s[g][:c, RW_GW:], 0.0) for g in gr]
    a_kk = [jnp.where(strict, s[g][c:, :RW_GW], 0.0) for g in gr]
    a_kb = [jnp.where(strict, s[g][c:, RW_GW:], 0.0) for g in gr]

    x = [eye - a_kb[g] for g in gr]
    p = [_dot(a_kb[g], bd(a_kb[g])) for g in gr]
    akv = [_dot(a_kk[g], bd(v[g])) for g in gr]
    for _ in range(4):
        xp = [_dot(jnp.concatenate([x[g], p[g]], axis=0), bd(p[g])) for g in gr]
        x = [x[g] + xp[g][:c] for g in gr]
        p = [xp[g][c:] for g in gr]
    t_inv = [x[g] + _dot(x[g], bd(p[g])) for g in gr]

    wu = [-_dot(t_inv[g], jnp.concatenate([bd(kap_true[g]), bd(akv[g])], axis=1)) for g in gr]

    hw = [_dot(jnp.concatenate([wu[g][:, :RW_GW], r_true[g], eye * jnp.exp(cend[g])], axis=0), bd(h0[g]))
          for g in gr]
    u = [hw[g][:c] + wu[g][:, RW_GW:] for g in gr]
    y = [hw[g][c:2 * c] + _dot(jnp.concatenate([a_rk[g], a_rb[g]], axis=1),
                               jnp.concatenate([bd(v[g]), bd(u[g])], axis=0)) for g in gr]
    o2 = [_dot_tn(jnp.concatenate([k[g] * eend[g], bet[g] * eend[g]], axis=0).astype(BF16),
                  jnp.concatenate([v[g], u[g]], axis=0).astype(BF16)) for g in gr]
    h_new = [hw[g][2 * c:] + _diag_blocks(o2[g], lane_head) for g in gr]
    return y, h_new


def _rwkv_mix_kernel(p_ref, pp_ref, pl_ref, plp_ref, mu_ref, mul_ref, w0_ref, a0_ref, wl_ref, al_ref,
                     kk_ref, ka_ref, rk_ref, gng_ref, gnb_ref, *rest):
    if len(rest) == 4:
        side_ref, o_ref, side_out_ref, h_scr = rest
        side_out_ref[...] = side_ref[...].astype(BF16)
    else:
        o_ref, h_scr = rest
    ci = pl.program_id(1)
    c = RW_CHUNK

    @pl.when(ci == 0)
    def _():
        h_scr[...] = jnp.zeros_like(h_scr)

    not_first = (ci != 0).astype(F32)
    row = lax.broadcasted_iota(jnp.int32, (c, 1), 0)

    def shifted(cur, prev_ref):
        prev_row = prev_ref[SUBLANES - 1:SUBLANES, :] * not_first
        rolled = pltpu.roll(cur, 1, 0)
        return jnp.where(row == 0, prev_row, rolled)

    lcur = pl_ref[...]
    lcur = lcur + (shifted(lcur, plp_ref) - lcur) * mul_ref[...]
    dw = jnp.tanh(lcur[:, :RW_LP // 2])
    da = lcur[:, RW_LP // 2:]

    ti = lax.broadcasted_iota(jnp.int32, (c, RW_GW), 0)
    li = lax.broadcasted_iota(jnp.int32, (c, RW_GW), 1)
    si = li % HEAD
    lane_head = li // HEAD
    rb = lax.broadcasted_iota(jnp.int32, (RW_GW, RW_GW), 0) // HEAD
    cb = lax.broadcasted_iota(jnp.int32, (RW_GW, RW_GW), 1) // HEAD
    same_head = rb == cb
    seg = same_head.astype(BF16)
    masks = (ti >= si, ti > si, (ti == si).astype(F32), same_head, lane_head)

    gr = range(RW_GROUPS)
    gsl = [pl.ds(g * RW_GW, RW_GW) for g in gr]

    def lerp(off, g):
        cols = pl.ds(off + g * RW_GW, RW_GW)
        cur = p_ref[:, cols]
        return cur + (shifted(cur, pp_ref.at[:, cols]) - cur) * mu_ref[:, cols]

    r = [lerp(0, g) for g in gr]
    k = [lerp(D, g) for g in gr]
    v = [lerp(2 * D, g) for g in gr]
    lw = [-DECAY_SCALE * jax.nn.sigmoid(w0_ref[:, gsl[g]] + _dot(dw, wl_ref[:, gsl[g]])) for g in gr]
    a = [jax.nn.sigmoid(a0_ref[:, gsl[g]] + _dot(da, al_ref[:, gsl[g]])) for g in gr]
    kap = [k[g] * kk_ref[:, gsl[g]] for g in gr]
    k = [k[g] * (1.0 + (a[g] - 1.0) * ka_ref[:, gsl[g]]) for g in gr]
    sums = _head_sums([kap[g] * kap[g] for g in gr] + [r[g] * k[g] * rk_ref[:, gsl[g]] for g in gr], seg)
    kap = [kap[g] / jnp.maximum(jnp.sqrt(sums[g]), 1e-12) for g in gr]
    bonus = sums[RW_GROUPS:]
    bet = [kap[g] * a[g] for g in gr]
    tri = (lax.broadcasted_iota(jnp.int32, (c, c), 0) >= lax.broadcasted_iota(jnp.int32, (c, c), 1)).astype(BF16)
    cum = [_cumsum_rows(lw[g], tri) for g in gr]

    y, h_new = [], []
    for lo in range(0, RW_GROUPS, RW_STAGE):
        sl = slice(lo, lo + RW_STAGE)
        ys, hs = _rwkv_groups(r[sl], lw[sl], cum[sl], k[sl], v[sl], kap[sl], bet[sl],
                              [h_scr[g] for g in range(lo, lo + RW_STAGE)], masks)
        y += ys
        h_new += hs
    for g in gr:
        h_scr[g] = h_new[g]

    ysum = _head_sums(y, seg)
    dlt = [y[g] - ysum[g] * (1.0 / HEAD) for g in gr]
    dsum = _head_sums([dlt[g] * dlt[g] for g in gr], seg)
    for g in gr:
        rstd = lax.rsqrt(dsum[g] * (1.0 / HEAD) + GN_EPS)
        yn = dlt[g] * rstd * gng_ref[:, gsl[g]] + gnb_ref[:, gsl[g]] + bonus[g] * v[g]
        o_ref[:, gsl[g]] = (yn * jax.nn.silu(lerp(3 * D, g))).astype(BF16)


def _rwkv_mix(p, p_lora, mu, mu_l, w0, a0, w_lora, a_lora, k_k, k_a, r_k, gn_g, gn_b, side=None):
    n_tok = p.shape[0]
    c = RW_CHUNK
    nc = SEQ // c
    ncol = p.shape[1]
    nl = p_lora.shape[1]
    row = lambda b, ci: (b * nc + ci, 0)
    prev = lambda b, ci: (jnp.maximum((b * nc + ci) * (c // SUBLANES) - 1, 0), 0)
    vec = pl.BlockSpec((1, D), lambda b, ci: (0, 0))
    lora = pl.BlockSpec((RW_LP // 2, D), lambda b, ci: (0, 0))
    in_specs = [pl.BlockSpec((c, ncol), row),
                pl.BlockSpec((SUBLANES, ncol), prev),
                pl.BlockSpec((c, nl), row),
                pl.BlockSpec((SUBLANES, nl), prev),
                pl.BlockSpec((1, ncol), lambda b, ci: (0, 0)),
                pl.BlockSpec((1, nl), lambda b, ci: (0, 0)),
                vec, vec, lora, lora,
                vec, vec, vec, vec, vec]
    args = [p, p, p_lora, p_lora, mu, mu_l, w0, a0, w_lora, a_lora, k_k, k_a, r_k, gn_g, gn_b]
    out_shape = jax.ShapeDtypeStruct((n_tok, D), BF16)
    out_specs = pl.BlockSpec((c, D), row)
    if side is not None:
        side_w, sl = side
        steps = (n_tok // SEQ) * nc
        _, srows, scols = side_w.shape
        srow = srows // steps
        assert srows % steps == 0 and srow % 16 == 0
        in_specs.append(pl.BlockSpec((None, srow, scols), lambda b, ci: (sl, b * nc + ci, 0)))
        args.append(side_w)
        out_shape = (out_shape, jax.ShapeDtypeStruct((srows, scols), BF16))
        out_specs = (out_specs, pl.BlockSpec((srow, scols), row))
    return pl.pallas_call(
        _rwkv_mix_kernel,
        out_shape=out_shape,
        grid=(n_tok // SEQ, nc),
        in_specs=in_specs,
        out_specs=out_specs,
        scratch_shapes=[pltpu.VMEM((RW_GROUPS, HEAD, RW_GW), F32)],
        compiler_params=_cparams("parallel", "arbitrary"),
        name="rwkv_mix",
    )(*args)


def kernel(x, c, positions, norm_g, mod_w, mod_b, final_norm_g, sg_w_in, sg_w_out, sg_ln_g, sg_ln_b, sg_w_spatial,
           sg_b_spatial, swa_w_in, swa_w_out, swa_sinks, rwkv_w_in, rwkv_w_out, rwkv_mu, rwkv_w0, rwkv_w_lora,
           rwkv_a0, rwkv_a_lora, rwkv_k_k, rwkv_k_a, rwkv_r_k, rwkv_gn_g, rwkv_gn_b):
    batch, seq, d = x.shape
    assert (seq, d) == (SEQ, D) and norm_g.shape[0] == DEPTH
    n_tok = batch * seq
    x2 = x.reshape(n_tok, D)
    c8 = jnp.zeros((SUBLANES, D), F32).at[:batch].set(c)
    pos2 = positions.reshape(n_tok, 1)
    half = HEAD // 2
    inv_freq = ROPE_THETA ** (-jnp.arange(half, dtype=F32) / half)
    invf = jnp.tile(inv_freq, LANES // half).reshape(1, LANES)
    alone = min(i + 1 for i in range(DEPTH) if i % 3 != 0)
    assert all(i % 3 != 0 for i in range(alone - 1, DEPTH - 1))
    mod, cos_t, sin_t = _modulation(c8, mod_w, mod_b, pos2, invf, alone)
    c_lanes = jnp.broadcast_to(c[:, :, None], (batch, D, LANES))
    mod_b3 = mod_b.reshape(DEPTH, 1, 3 * D)
    mod8 = None
    row = lambda a: a.reshape(1, -1)

    def act_rows(spans):
        rows = [jnp.concatenate([jnp.full((w,), val[t], F32) for w, *val in spans]) for t in range(3)]
        return jnp.zeros((SUBLANES, rows[0].shape[0]), F32).at[:3].set(jnp.stack(rows))

    gelu_span = (GELU_A, GELU_B, 1.0)
    silu_span = (0.5, 0.0, 1.0)
    sg_first_bf = sg_w_in[:1].astype(BF16)
    assert n_tok // SWA_BLOCK * LANES == 4 * D
    w_out = (sg_w_out, swa_w_out, rwkv_w_out)
    side_bf = None
    for i in range(DEPTH):
        kind, j = i % 3, i // 3
        mod3 = (mod[i] if i < alone else mod8)[:batch].reshape(batch, 1, 3 * D)
        g = row(norm_g[i])
        has_next = i + 1 < DEPTH
        final_g = None if has_next else row(final_norm_g)
        if kind == 0:
            act = act_rows([(2 * D, *gelu_span), (D, *silu_span)])
            w = sg_first_bf if i == 0 else side_bf[None]
            p = _in_proj_act(x2, g, mod3, w, 0, 3 * D, act, False, BF16, (swa_w_in, j) if has_next else None)
            p, side_bf = p if has_next else (p, None)
            x2 = _sg_out(p, row(sg_ln_g[j]), row(sg_ln_b[j]), sg_w_spatial[j],
                         jnp.broadcast_to(sg_b_spatial[j][:, :, None], (SG_GROUPS, SG_CHUNK, LANES)),
                         sg_w_out, j, x2, mod3, final_g)
            continue
        elif kind == 1:
            act = act_rows([(D + 2 * SWA_KVW, 0.0, 0.0, 0.0), (D, *silu_span)])
            p = _in_proj_act(x2, g, mod3, side_bf[None], 0, 2 * D + 2 * SWA_KVW, act, True, F32)
            mix = _swa_mix(p, cos_t, sin_t, swa_sinks[j],
                           (jnp.swapaxes(rwkv_w_in, 1, 2), (i + 1) // 3) if has_next else None)
            mix, side_bf = mix if has_next else (mix, None)
        else:
            lp = RW_LP // 2 - RW_LORA
            w_in = rwkv_w_in[j, :, 4 * D:].astype(BF16)
            w_l = jnp.concatenate([jnp.pad(w_in[:, :RW_LORA], ((0, 0), (0, lp))),
                                   jnp.pad(w_in[:, RW_LORA:], ((0, 0), (0, lp)))], axis=1)
            p, p_lora = _in_proj(x2, g, mod3, side_bf[None], 0, 4 * D, w_l)
            mu = rwkv_mu[j]
            mu_l = jnp.concatenate([jnp.pad(mu[4 * D:4 * D + RW_LORA], (0, lp)), jnp.pad(mu[4 * D + RW_LORA:], (0, lp))])
            wl_pad = jnp.pad(rwkv_w_lora[j], ((0, lp), (0, 0))).astype(BF16)
            al_pad = jnp.pad(rwkv_a_lora[j], ((0, lp), (0, 0))).astype(BF16)
            mix = _rwkv_mix(p, p_lora, row(mu[:4 * D]), row(mu_l), row(rwkv_w0[j]),
                            row(rwkv_a0[j]), wl_pad, al_pad, row(rwkv_k_k[j]),
                            row(rwkv_k_a[j]), row(rwkv_r_k[j]), row(rwkv_gn_g[j]), row(rwkv_gn_b[j]),
                            (sg_w_in, j + 1) if has_next else None)
            mix, side_bf = mix if has_next else (mix, None)
        x2 = _out_proj(mix, w_out[kind], j, x2, mod3, final_g, (c_lanes, mod_w, mod_b3, i + 1) if has_next else None)
        x2, mod8 = x2 if has_next else (x2, None)
    return x2.reshape(batch, seq, D)
```

```python
import functools
import math

import jax
import jax.numpy as jnp
from jax import lax
from jax.experimental import pallas as pl
from jax.experimental.pallas import tpu as pltpu

F32 = jnp.float32
BF16 = jnp.bfloat16

D = 2048
SEQ = 4096
DEPTH = 4
HEAD = 64
LANES = 128
SUBLANES = 8
SG_CHUNK = 128
SG_GROUPS = 16
SWA_HEADS = D // HEAD
SWA_KV = SWA_HEADS // 8
SWA_KVW = SWA_KV * HEAD
SWA_BLOCK = 128
SWA_STAGE = 16
ROPE_THETA = 10000.0
RW_LORA = 96
RW_LP = 256
RW_CHUNK = 64
RW_HG = 4
RW_GW = RW_HG * HEAD
RW_GROUPS = D // RW_GW
RW_STAGE = 8
DECAY_SCALE = math.exp(-0.5)
GN_EPS = 64e-5
RMS_EPS = 1e-6
LN_EPS = 1e-5

VMEM_LIMIT = 48 * 1024 * 1024


def _cparams(*sem, vmem=VMEM_LIMIT):
    return pltpu.CompilerParams(dimension_semantics=sem, vmem_limit_bytes=vmem)


def _mod_kernel(c_ref, w_ref, b_ref, pos_ref, invf_ref, o_ref, cos_ref, sin_ref):
    cond = jax.nn.silu(c_ref[...]).astype(BF16)
    o_ref[0] = jnp.dot(cond, w_ref[0].astype(BF16), preferred_element_type=F32) + b_ref[0]
    _rope_table_kernel(pos_ref, invf_ref, cos_ref, sin_ref)


def _modulation(c8, mod_w, mod_b, pos2, invf, layers):
    tn = 1536
    n = mod_w.shape[2]
    nj = n // tn
    n_tok = pos2.shape[0]
    rows = n_tok // (layers * nj)
    assert n % tn == 0 and rows * layers * nj == n_tok and rows % SUBLANES == 0
    table = pl.BlockSpec((rows, LANES), lambda l, j: (l * nj + j, 0))
    return pl.pallas_call(
        _mod_kernel,
        out_shape=(jax.ShapeDtypeStruct((layers, SUBLANES, n), F32),
                   jax.ShapeDtypeStruct((n_tok, LANES), F32), jax.ShapeDtypeStruct((n_tok, LANES), F32)),
        grid=(layers, nj),
        in_specs=[pl.BlockSpec((SUBLANES, D), lambda l, j: (0, 0)),
                  pl.BlockSpec((1, D, tn), lambda l, j: (l, 0, j)),
                  pl.BlockSpec((1, 1, tn), lambda l, j: (l, 0, j)),
                  pl.BlockSpec((rows, 1), lambda l, j: (l * nj + j, 0)),
                  pl.BlockSpec((1, LANES), lambda l, j: (0, 0))],
        out_specs=(pl.BlockSpec((1, SUBLANES, tn), lambda l, j: (l, 0, j)), table, table),
        compiler_params=_cparams("parallel", "parallel"),
        name="modulation",
    )(c8, mod_w, mod_b.reshape(DEPTH, 1, n), pos2, invf)


NORM_ROWS = 128


def _row_rsqrt(x_ref, rs_scr):
    def body(i, carry):
        rows = pl.ds(pl.multiple_of(i * NORM_ROWS, NORM_ROWS), NORM_ROWS)
        x = x_ref[rows, :]
        rs_scr[rows, :] = lax.rsqrt(jnp.mean(x * x, axis=-1, keepdims=True) + RMS_EPS)
        return carry
    lax.fori_loop(0, x_ref.shape[0] // NORM_ROWS, body, 0, unroll=True)


def _modulated(x_ref, rs_scr, g_ref, sc_ref, sh_ref):
    gain = g_ref[...] * (1.0 + sc_ref[0])
    return (x_ref[...] * rs_scr[...] * gain + sh_ref[0]).astype(BF16)


GELU_A = math.sqrt(2.0 / math.pi)
GELU_B = GELU_A * 0.044715
ACT_COLS = 256


def _in_proj_act_kernel(x_ref, g_ref, sc_ref, sh_ref, w_ref, act_ref, *rest, select):
    if len(rest) == 4:
        side_ref, o_ref, side_out_ref, rs_scr = rest
        side_out_ref[...] = side_ref[...].astype(BF16)
    else:
        o_ref, rs_scr = rest

    @pl.when(pl.program_id(1) == 0)
    def _():
        _row_rsqrt(x_ref, rs_scr)

    h = _modulated(x_ref, rs_scr, g_ref, sc_ref, sh_ref)
    for n in range(o_ref.shape[1] // ACT_COLS):
        cols = pl.ds(n * ACT_COLS, ACT_COLS)
        p = jnp.dot(h, w_ref[:, cols], preferred_element_type=F32)
        gated = (0.5 * p) * (1.0 + jnp.tanh(p * (act_ref[0:1, cols] + act_ref[1:2, cols] * (p * p))))
        o_ref[:, cols] = (jnp.where(act_ref[2:3, cols] > 0.0, gated, p) if select else gated).astype(o_ref.dtype)


def _in_proj_lora_kernel(x_ref, g_ref, sc_ref, sh_ref, w_ref, wl_ref, o_ref, ol_ref, rs_scr):
    @pl.when(pl.program_id(1) == 0)
    def _():
        _row_rsqrt(x_ref, rs_scr)
        ol_ref[...] = jnp.dot(_modulated(x_ref, rs_scr, g_ref, sc_ref, sh_ref), wl_ref[...],
                              preferred_element_type=F32)

    h = _modulated(x_ref, rs_scr, g_ref, sc_ref, sh_ref)
    o_ref[...] = jnp.dot(h, w_ref[...], preferred_element_type=F32)


SIDE_ROWS = 128


def _in_proj_act(x2, g, mod3, w_bf, wl, ncols, act, select, out_dtype, side=None, *, tm=1024):
    n_tok = x2.shape[0]
    tn = 2048 if (out_dtype == BF16 and ncols % 2048 == 0) else (1024 if ncols % 1024 == 0 else 1536)
    assert ncols % tn == 0
    per_b = SEQ // tm
    nj = ncols // tn
    in_specs = [pl.BlockSpec((tm, D), lambda i, j: (i, 0)),
                pl.BlockSpec((1, D), lambda i, j: (0, 0)),
                pl.BlockSpec((1, 1, D), lambda i, j: (i // per_b, 0, 1)),
                pl.BlockSpec((1, 1, D), lambda i, j: (i // per_b, 0, 0)),
                pl.BlockSpec((None, D, tn), lambda i, j: (wl, 0, j)),
                pl.BlockSpec((SUBLANES, tn), lambda i, j: (0, j))]
    out_shape = jax.ShapeDtypeStruct((n_tok, ncols), out_dtype)
    out_specs = pl.BlockSpec((tm, tn), lambda i, j: (i, j))
    args = [x2, g, mod3, mod3, w_bf, act]
    osize = jnp.dtype(out_dtype).itemsize
    vmem = 2 * (4 * tm * D + 2 * D * tn + osize * tm * tn) + 8 * 1024 * 1024
    if side is not None:
        side_w, sl = side
        _, srows, scols = side_w.shape
        nblk = srows // SIDE_ROWS
        assert srows % SIDE_ROWS == 0 and nblk <= (n_tok // tm) * nj
        blk = lambda i, j: jnp.minimum(i * nj + j, nblk - 1)
        in_specs.append(pl.BlockSpec((None, SIDE_ROWS, scols), lambda i, j: (sl, blk(i, j), 0)))
        args.append(side_w)
        out_shape = (out_shape, jax.ShapeDtypeStruct((srows, scols), BF16))
        out_specs = (out_specs, pl.BlockSpec((SIDE_ROWS, scols), lambda i, j: (blk(i, j), 0)))
        vmem += 2 * (4 + 2) * SIDE_ROWS * scols
    return pl.pallas_call(
        functools.partial(_in_proj_act_kernel, select=select),
        out_shape=out_shape,
        grid=(n_tok // tm, nj),
        in_specs=in_specs,
        out_specs=out_specs,
        scratch_shapes=[pltpu.VMEM((tm, 1), F32)],
        compiler_params=_cparams("arbitrary", "arbitrary", vmem=vmem),
        name="in_proj_act",
    )(*args)


def _in_proj(x2, g, mod3, w_bf, wl, ncols, w_lora_bf, *, tm=1024):
    n_tok = x2.shape[0]
    tn = 2048
    assert ncols % tn == 0
    per_b = SEQ // tm
    x_spec = pl.BlockSpec((tm, D), lambda i, j: (i, 0))
    g_spec = pl.BlockSpec((1, D), lambda i, j: (0, 0))
    sh_spec = pl.BlockSpec((1, 1, D), lambda i, j: (i // per_b, 0, 0))
    sc_spec = pl.BlockSpec((1, 1, D), lambda i, j: (i // per_b, 0, 1))
    w_spec = pl.BlockSpec((None, D, tn), lambda i, j: (wl, 0, j))
    o_spec = pl.BlockSpec((tm, tn), lambda i, j: (i, j))
    grid = (n_tok // tm, ncols // tn)
    scratch = [pltpu.VMEM((tm, 1), F32)]
    nl = w_lora_bf.shape[1]
    vmem = 2 * (4 * tm * D + 2 * D * tn + 4 * tm * tn + 2 * D * nl + 4 * tm * nl) + 4 * 1024 * 1024
    return pl.pallas_call(
        _in_proj_lora_kernel,
        out_shape=(jax.ShapeDtypeStruct((n_tok, ncols), F32), jax.ShapeDtypeStruct((n_tok, nl), F32)),
        grid=grid,
        in_specs=[x_spec, g_spec, sc_spec, sh_spec, w_spec, pl.BlockSpec((D, nl), lambda i, j: (0, 0))],
        out_specs=(o_spec, pl.BlockSpec((tm, nl), lambda i, j: (i, 0))),
        scratch_shapes=scratch,
        compiler_params=_cparams("parallel", "arbitrary", vmem=vmem),
        name="in_proj_lora",
    )(x2, g, mod3, mod3, w_bf, w_lora_bf)


CAST_ROWS = 256


def _round_weight(w_ref, w_scr):
    @pl.when(pl.program_id(0) == 0)
    def _():
        def body(i, carry):
            rows = pl.ds(pl.multiple_of(i * CAST_ROWS, CAST_ROWS), CAST_ROWS)
            w_scr[rows, :] = w_ref[rows, :].astype(BF16)
            return carry
        lax.fori_loop(0, w_ref.shape[0] // CAST_ROWS, body, 0, unroll=True)


def _out_proj_kernel(m_ref, w_ref, x_ref, gate_ref, o_ref, w_scr):
    _round_weight(w_ref, w_scr)
    y = jnp.dot(m_ref[...], w_scr[...], preferred_element_type=F32)
    o_ref[...] = x_ref[...] + gate_ref[0] * y


MOD_ROWS = 16


def _out_proj_mod_kernel(m_ref, w_ref, x_ref, gate_ref, c_ref, mw_ref, mb_ref, o_ref, mo_ref, w_scr, cond_scr):
    @pl.when(pl.program_id(0) == 0)
    def _():
        cond_scr[...] = jax.nn.silu(c_ref[...])

    _round_weight(w_ref, w_scr)
    nb, cols = c_ref.shape[0], mo_ref.shape[1]
    tiles = cols // LANES

    def body(r, acc):
        rows = pl.ds(pl.multiple_of(r * MOD_ROWS, MOD_ROWS), MOD_ROWS)
        w = [mw_ref[rows, pl.ds(t * LANES, LANES)] for t in range(tiles)]
        cond = [cond_scr[b, rows, :] for b in range(nb)]
        return tuple(acc[b * tiles + t] + w[t] * cond[b] for b in range(nb) for t in range(tiles))

    acc = (jnp.zeros((MOD_ROWS, LANES), F32),) * (nb * tiles)
    n_tiles = D // ACT_COLS
    share = D // MOD_ROWS // n_tiles
    for n in range(n_tiles):
        cs = pl.ds(n * ACT_COLS, ACT_COLS)
        acc = lax.fori_loop(n * share, (n + 1) * share, body, acc, unroll=True)
        bits = lax.bitcast_convert_type(acc[0], jnp.uint32)
        for a in acc[1:]:
            bits = bits | lax.bitcast_convert_type(a, jnp.uint32)
        one = ((bits >> 16) >> 16)[:1].astype(F32) + 1.0
        gate = gate_ref[0, :, cs] * jnp.concatenate([one] * (ACT_COLS // LANES), axis=1)
        y = jnp.dot(m_ref[...], w_scr[:, cs], preferred_element_type=F32)
        o_ref[:, cs] = x_ref[:, cs] + gate * y
    rowid = lax.broadcasted_iota(jnp.int32, (SUBLANES, cols), 0)
    out = jnp.zeros((SUBLANES, cols), F32)
    for b in range(nb):
        sums = [jnp.sum(acc[b * tiles + t], axis=0, keepdims=True) for t in range(tiles)]
        out = jnp.where(rowid == b, jnp.concatenate(sums, axis=1) + mb_ref[...], out)
    mo_ref[...] = out


def _out_proj_final_kernel(m_ref, w_ref, x_ref, gate_ref, fg_ref, o_ref, w_scr):
    _round_weight(w_ref, w_scr)
    y = jnp.dot(m_ref[...], w_scr[...], preferred_element_type=F32)
    xn = x_ref[...] + gate_ref[0] * y
    o_ref[...] = xn * lax.rsqrt(jnp.mean(xn * xn, axis=-1, keepdims=True) + RMS_EPS) * fg_ref[...]


def _out_proj(mix, w, wl, x2, mod3, final_g=None, next_mod=None, *, tm=512):
    n_tok = x2.shape[0]
    per_b = SEQ // tm
    steps = n_tok // tm
    in_specs = [pl.BlockSpec((tm, D), lambda i: (i, 0)),
                pl.BlockSpec((None, D, D), lambda i: (wl, 0, 0), pipeline_mode=pl.Buffered(1)),
                pl.BlockSpec((tm, D), lambda i: (i, 0)),
                pl.BlockSpec((1, 1, D), lambda i: (i // per_b, 0, 2))]
    args = [mix, w, x2, mod3]
    kern, name = _out_proj_kernel, "out_proj"
    out_shape = jax.ShapeDtypeStruct((n_tok, D), F32)
    out_specs = pl.BlockSpec((tm, D), lambda i: (i, 0))
    scratch = [pltpu.VMEM((D, D), BF16)]
    vmem = 6 * D * D + 2 * (2 + 4 + 4) * tm * D + 8 * 1024 * 1024
    if final_g is not None:
        in_specs.append(pl.BlockSpec((1, D), lambda i: (0, 0)))
        args.append(final_g)
        kern, name = _out_proj_final_kernel, "out_proj_final"
    elif next_mod is not None:
        cb, mod_w, mod_b, layer = next_mod
        n = mod_w.shape[2]
        cols = n // steps
        assert cols * steps == n and cols % LANES == 0
        in_specs += [pl.BlockSpec(cb.shape, lambda i: (0, 0, 0), pipeline_mode=pl.Buffered(1)),
                     pl.BlockSpec((None, D, cols), lambda i: (layer, 0, i)),
                     pl.BlockSpec((None, 1, cols), lambda i: (layer, 0, i))]
        args += [cb, mod_w, mod_b]
        kern, name = _out_proj_mod_kernel, "out_proj_mod"
        out_shape = (out_shape, jax.ShapeDtypeStruct((SUBLANES, n), F32))
        out_specs = (out_specs, pl.BlockSpec((SUBLANES, cols), lambda i: (0, i)))
        scratch.append(pltpu.VMEM(cb.shape, F32))
        vmem += 2 * 4 * cb.size + 2 * 4 * D * cols
    return pl.pallas_call(
        kern,
        out_shape=out_shape,
        grid=(steps,),
        in_specs=in_specs,
        out_specs=out_specs,
        scratch_shapes=scratch,
        compiler_params=_cparams("arbitrary", vmem=vmem),
        name=name,
    )(*args)


def _sg_mix_kernel(u_ref, v_ref, z_ref, lng_ref, lnb_ref, ws_ref, bs_ref, o_ref, *, tm):
    ti = lax.broadcasted_iota(jnp.int32, (SG_CHUNK, SG_CHUNK), 0)
    si = lax.broadcasted_iota(jnp.int32, (SG_CHUNK, SG_CHUNK), 1)
    causal = ti >= si
    ws = [jnp.where(causal, ws_ref[g], 0.0).astype(BF16) for g in range(SG_GROUPS)]
    for c in range(tm // SG_CHUNK):
        rows = pl.ds(c * SG_CHUNK, SG_CHUNK)
        v = v_ref[rows, :].astype(F32)
        mean = jnp.mean(v, axis=-1, keepdims=True)
        d = v - mean
        var = jnp.mean(d * d, axis=-1, keepdims=True)
        vn = (d * lax.rsqrt(var + LN_EPS) * lng_ref[...] + lnb_ref[...]).astype(BF16)
        f = [jnp.dot(ws[g], vn[:, g * LANES:(g + 1) * LANES], preferred_element_type=F32) for g in range(SG_GROUPS)]
        for g in range(SG_GROUPS):
            cols = pl.ds(g * LANES, LANES)
            uz = u_ref[rows, cols].astype(F32) * z_ref[rows, cols].astype(F32)
            o_ref[rows, cols] = (uz * (f[g] + bs_ref[g])).astype(BF16)


def _sg_out_kernel(u_ref, v_ref, z_ref, lng_ref, lnb_ref, ws_ref, bs_ref, w_ref, x_ref, gate_ref, *rest, tm, final):
    if final:
        fg_ref, o_ref, w_scr, mix_scr = rest
    else:
        o_ref, w_scr, mix_scr = rest
    _round_weight(w_ref, w_scr)
    _sg_mix_kernel(u_ref, v_ref, z_ref, lng_ref, lnb_ref, ws_ref, bs_ref, mix_scr, tm=tm)
    y = jnp.dot(mix_scr[...], w_scr[...], preferred_element_type=F32)
    xn = x_ref[...] + gate_ref[0] * y
    if final:
        xn = xn * lax.rsqrt(jnp.mean(xn * xn, axis=-1, keepdims=True) + RMS_EPS) * fg_ref[...]
    o_ref[...] = xn


def _sg_out(p, ln_g, ln_b, w_s, b_full, w, wl, x2, mod3, final_g=None, *, tm=256):
    n_tok = p.shape[0]
    per_b = SEQ // tm
    tile = lambda c: pl.BlockSpec((tm, D), lambda i: (i, c))
    vec = pl.BlockSpec((1, D), lambda i: (0, 0))
    in_specs = [tile(0), tile(1), tile(2), vec, vec,
                pl.BlockSpec((SG_GROUPS, SG_CHUNK, SG_CHUNK), lambda i: (0, 0, 0)),
                pl.BlockSpec((SG_GROUPS, SG_CHUNK, LANES), lambda i: (0, 0, 0)),
                pl.BlockSpec((None, D, D), lambda i: (wl, 0, 0), pipeline_mode=pl.Buffered(1)),
                tile(0),
                pl.BlockSpec((1, 1, D), lambda i: (i // per_b, 0, 2))]
    args = [p, p, p, ln_g, ln_b, w_s, b_full, w, x2, mod3]
    if final_g is not None:
        in_specs.append(vec)
        args.append(final_g)
    vmem = (6 * D * D + 2 * tm * D + 2 * (3 * 2 + 4 + 4) * tm * D
            + 2 * 4 * SG_GROUPS * SG_CHUNK * (SG_CHUNK + LANES) + 8 * 1024 * 1024)
    return pl.pallas_call(
        functools.partial(_sg_out_kernel, tm=tm, final=final_g is not None),
        out_shape=jax.ShapeDtypeStruct((n_tok, D), F32),
        grid=(n_tok // tm,),
        in_specs=in_specs,
        out_specs=pl.BlockSpec((tm, D), lambda i: (i, 0)),
        scratch_shapes=[pltpu.VMEM((D, D), BF16), pltpu.VMEM((tm, D), BF16)],
        compiler_params=_cparams("arbitrary", vmem=vmem),
        name="sg_out_final" if final_g is not None else "sg_out",
    )(*args)


def _first_half(shape):
    return (lax.broadcasted_iota(jnp.int32, shape, 1) % HEAD) < (HEAD // 2)


def _rope_table_kernel(pos_ref, invf_ref, cos_ref, sin_ref):
    ang = pos_ref[...].astype(F32) * invf_ref[...]
    cos_ref[...] = jnp.cos(ang)
    sin_ref[...] = jnp.where(_first_half(ang.shape), -jnp.sin(ang), jnp.sin(ang))


def _rope(x, cos, sin_signed, first):
    partner = jnp.where(first, pltpu.roll(x, LANES - HEAD // 2, 1), pltpu.roll(x, HEAD // 2, 1))
    return x * cos + partner * sin_signed


def _swa_mix_kernel(sink_ref, p_ref, kvp_ref, cos_ref, sin_ref, cosp_ref, sinp_ref, *rest):
    if len(rest) == 3:
        wt_ref, o_ref, wo_ref = rest
        wo_ref[...] = wt_ref[...].T.astype(BF16)
    else:
        o_ref, = rest
    i = pl.program_id(0)
    first_key = jnp.where((i % (SEQ // SWA_BLOCK)) != 0, 0, SWA_BLOCK)
    cos, sin_s, cosp, sinp_s = cos_ref[...], sin_ref[...], cosp_ref[...], sinp_ref[...]
    first = _first_half(cos.shape)
    lane = lax.broadcasted_iota(jnp.int32, (2 * SWA_BLOCK, LANES), 1)
    low = lane < HEAD

    qi = lax.broadcasted_iota(jnp.int32, (SWA_BLOCK, 2 * SWA_BLOCK), 0)
    kj = lax.broadcasted_iota(jnp.int32, (SWA_BLOCK, 2 * SWA_BLOCK), 1)
    rel = qi + SWA_BLOCK - kj
    mask = (rel >= 0) & (rel < SWA_BLOCK) & (kj >= first_key)

    k_lo, k_hi, v_lo, v_hi = [], [], [], []
    for kp in range(SWA_KV // 2):
        ksl = pl.ds(kp * LANES, LANES)
        vsl = pl.ds(SWA_KVW + kp * LANES, LANES)
        k2 = jnp.concatenate([_rope(kvp_ref[:, ksl], cosp, sinp_s, first),
                              _rope(p_ref[:, pl.ds(D + kp * LANES, LANES)], cos, sin_s, first)], axis=0)
        v2 = jnp.concatenate([kvp_ref[:, vsl], p_ref[:, pl.ds(D + SWA_KVW + kp * LANES, LANES)]], axis=0)
        k2r = pltpu.roll(k2, HEAD, 1)
        v2r = pltpu.roll(v2, HEAD, 1)
        for sub in range(2):
            src_k, src_kr = (k2, k2r) if sub == 0 else (k2r, k2)
            src_v, src_vr = (v2, v2r) if sub == 0 else (v2r, v2)
            k_lo.append(jnp.where(low, src_k, 0.0).astype(BF16))
            k_hi.append(jnp.where(low, 0.0, src_kr).astype(BF16))
            v_lo.append(jnp.where(low, src_v, 0.0).astype(BF16))
            v_hi.append(jnp.where(low, 0.0, src_vr).astype(BF16))

    zoff = D + 2 * SWA_KVW
    for h0 in range(0, SWA_HEADS, SWA_STAGE):
        hs = range(h0, h0 + SWA_STAGE)
        prs = range(h0 // 2, (h0 + SWA_STAGE) // 2)
        cols = {pr: pl.ds(pr * LANES, LANES) for pr in prs}
        q2 = {pr: (_rope(p_ref[:, cols[pr]], cos, sin_s, first) * (HEAD ** -0.5)).astype(BF16) for pr in prs}
        s = {h: _dot_nt(q2[h // 2], (k_hi if h % 2 else k_lo)[h // 8]) for h in hs}
        s = {h: jnp.where(mask, s[h], -jnp.inf) for h in hs}
        m = {h: jnp.maximum(jnp.max(s[h], axis=-1, keepdims=True), sink_ref[h]) for h in hs}
        e = {h: jnp.exp(s[h] - m[h]) for h in hs}
        denom = {h: jnp.sum(e[h], axis=-1, keepdims=True) + jnp.exp(sink_ref[h] - m[h]) for h in hs}
        prob = {h: (e[h] * (1.0 / denom[h])).astype(BF16) for h in hs}
        o = {h: jnp.dot(prob[h], (v_hi if h % 2 else v_lo)[h // 8], preferred_element_type=F32) for h in hs}
        for pr in prs:
            gate = p_ref[:, pl.ds(zoff + pr * LANES, LANES)]
            o_ref[:, cols[pr]] = ((o[2 * pr] + o[2 * pr + 1]) * gate).astype(BF16)


def _swa_mix(p, cos_t, sin_t, sinks, side=None):
    n_tok, ncols = p.shape
    tb = SWA_BLOCK
    steps = n_tok // tb
    kvb = D // (2 * SWA_KVW)
    cur = pl.BlockSpec((tb, LANES), lambda i: (i, 0))
    prev = pl.BlockSpec((tb, LANES), lambda i: (jnp.maximum(i - 1, 0), 0))
    in_specs = [pl.BlockSpec(memory_space=pltpu.SMEM),
                pl.BlockSpec((tb, ncols), lambda i: (i, 0)),
                pl.BlockSpec((tb, 2 * SWA_KVW), lambda i: (jnp.maximum(i - 1, 0), kvb)),
                cur, cur, prev, prev]
    args = [sinks, p, p, cos_t, sin_t, cos_t, sin_t]
    out_shape = jax.ShapeDtypeStruct((n_tok, D), BF16)
    out_specs = pl.BlockSpec((tb, D), lambda i: (i, 0))
    if side is not None:
        w_t, layer = side
        in_specs.append(pl.BlockSpec((None, LANES, D), lambda i: (layer, i, 0)))
        args.append(w_t)
        out_shape = (out_shape, jax.ShapeDtypeStruct((D, steps * LANES), BF16))
        out_specs = (out_specs, pl.BlockSpec((D, LANES), lambda i: (0, i)))
    return pl.pallas_call(
        _swa_mix_kernel,
        out_shape=out_shape,
        grid=(steps,),
        in_specs=in_specs,
        out_specs=out_specs,
        compiler_params=_cparams("parallel"),
        name="swa_mix",
    )(*args)


def _head_sums(xs, seg):
    out = jnp.dot(jnp.concatenate([x.astype(BF16) for x in xs], axis=0), seg, preferred_element_type=F32)
    n = xs[0].shape[0]
    return [out[i * n:(i + 1) * n] for i in range(len(xs))]


def _cumsum_rows(x, tri):
    hi = x.astype(BF16)
    lo = (x - hi.astype(F32)).astype(BF16)
    return jnp.dot(tri, hi, preferred_element_type=F32) + jnp.dot(tri, lo, preferred_element_type=F32)


def _bd(x, same_head):
    t = jnp.concatenate([x] * RW_HG, axis=0)
    return jnp.where(same_head, t, 0.0).astype(BF16)


def _diag_blocks(o, lane_head):
    acc = o[0:HEAD]
    for h in range(1, RW_HG):
        acc = jnp.where(lane_head == h, o[h * HEAD:(h + 1) * HEAD], acc)
    return acc


def _dot(a, b):
    return jnp.dot(a.astype(BF16), b.astype(BF16), preferred_element_type=F32)


def _dot_nt(a, b):
    return lax.dot_general(a, b, (((1,), (1,)), ((), ())), preferred_element_type=F32)


def _dot_tn(a, b):
    return lax.dot_general(a, b, (((0,), (0,)), ((), ())), preferred_element_type=F32)


def _rwkv_groups(r, lw, cum, k, v, kap, bet, h0, masks):
    incl, strict, eye, same_head, lane_head = masks
    c = RW_CHUNK
    gr = range(len(r))
    bd = lambda t: _bd(t, same_head)
    g_t = [jnp.exp(cum[g]) for g in gr]
    g_inv = [jnp.exp(-cum[g]) for g in gr]
    cref = [cum[g][c // 2 - 1:c // 2] for g in gr]
    cend = [cum[g][c - 1:c] for g in gr]
    to_ref = [jnp.exp(-cref[g]) for g in gr]
    r_true = [r[g] * g_t[g] for g in gr]
    kap_true = [kap[g] * (g_t[g] * jnp.exp(-lw[g])) for g in gr]
    r_s = [r_true[g] * to_ref[g] for g in gr]
    kap_s = [kap_true[g] * to_ref[g] for g in gr]
    ginv = [g_inv[g] * jnp.exp(cref[g]) for g in gr]
    eend = [g_inv[g] * jnp.exp(cend[g]) for g in gr]

    s = [_dot_nt(jnp.concatenate([r_s[g], kap_s[g]], axis=0).astype(BF16),
                 jnp.concatenate([bd(k[g] * ginv[g]), bd(bet[g] * ginv[g])], axis=0)) for g in gr]
    a_rk = [jnp.where(incl, s[g][:c, :RW_GW], 0.0) for g in gr]
    a_rb = [jnp.where(incl, s[g][:c, RW_GW:], 0.0) for g in gr]
    a_kk = [jnp.where(strict, s[g][c:, :RW_GW], 0.0) for g in gr]
    a_kb = [jnp.where(strict, s[g][c:, RW_GW:], 0.0) for g in gr]

    x = [eye - a_kb[g] for g in gr]
    p = [_dot(a_kb[g], bd(a_kb[g])) for g in gr]
    akv = [_dot(a_kk[g], bd(v[g])) for g in gr]
    for _ in range(4):
        xp = [_dot(jnp.concatenate([x[g], p[g]], axis=0), bd(p[g])) for g in gr]
        x = [x[g] + xp[g][:c] for g in gr]
        p = [xp[g][c:] for g in gr]
    t_inv = [x[g] + _dot(x[g], bd(p[g])) for g in gr]

    wu = [-_dot(t_inv[g], jnp.concatenate([bd(kap_true[g]), bd(akv[g])], axis=1)) for g in gr]

    hw = [_dot(jnp.concatenate([wu[g][:, :RW_GW], r_true[g], eye * jnp.exp(cend[g])], axis=0), bd(h0[g]))
          for g in gr]
    u = [hw[g][:c] + wu[g][:, RW_GW:] for g in gr]
    y = [hw[g][c:2 * c] + _dot(jnp.concatenate([a_rk[g], a_rb[g]], axis=1),
                               jnp.concatenate([bd(v[g]), bd(u[g])], axis=0)) for g in gr]
    o2 = [_dot_tn(jnp.concatenate([k[g] * eend[g], bet[g] * eend[g]], axis=0).astype(BF16),
                  jnp.concatenate([v[g], u[g]], axis=0).astype(BF16)) for g in gr]
    h_new = [hw[g][2 * c:] + _diag_blocks(o2[g], lane_head) for g in gr]
    return y, h_new


def _rwkv_mix_kernel(p_ref, pp_ref, pl_ref, plp_ref, mu_ref, mul_ref, w0_ref, a0_ref, wl_ref, al_ref,
                     kk_ref, ka_ref, rk_ref, gng_ref, gnb_ref, *rest):
    if len(rest) == 4:
        side_ref, o_ref, side_out_ref, h_scr = rest
        side_out_ref[...] = side_ref[...].astype(BF16)
    else:
        o_ref, h_scr = rest
    ci = pl.program_id(1)
    c = RW_CHUNK

    @pl.when(ci == 0)
    def _():
        h_scr[...] = jnp.zeros_like(h_scr)

    not_first = (ci != 0).astype(F32)
    row = lax.broadcasted_iota(jnp.int32, (c, 1), 0)

    def shifted(cur, prev_ref):
        prev_row = prev_ref[SUBLANES - 1:SUBLANES, :] * not_first
        rolled = pltpu.roll(cur, 1, 0)
        return jnp.where(row == 0, prev_row, rolled)

    lcur = pl_ref[...]
    lcur = lcur + (shifted(lcur, plp_ref) - lcur) * mul_ref[...]
    dw = jnp.tanh(lcur[:, :RW_LP // 2])
    da = lcur[:, RW_LP // 2:]

    ti = lax.broadcasted_iota(jnp.int32, (c, RW_GW), 0)
    li = lax.broadcasted_iota(jnp.int32, (c, RW_GW), 1)
    si = li % HEAD
    lane_head = li // HEAD
    rb = lax.broadcasted_iota(jnp.int32, (RW_GW, RW_GW), 0) // HEAD
    cb = lax.broadcasted_iota(jnp.int32, (RW_GW, RW_GW), 1) // HEAD
    same_head = rb == cb
    seg = same_head.astype(BF16)
    masks = (ti >= si, ti > si, (ti == si).astype(F32), same_head, lane_head)

    gr = range(RW_GROUPS)
    gsl = [pl.ds(g * RW_GW, RW_GW) for g in gr]

    def lerp(off, g):
        cols = pl.ds(off + g * RW_GW, RW_GW)
        cur = p_ref[:, cols]
        return cur + (shifted(cur, pp_ref.at[:, cols]) - cur) * mu_ref[:, cols]

    r = [lerp(0, g) for g in gr]
    k = [lerp(D, g) for g in gr]
    v = [lerp(2 * D, g) for g in gr]
    lw = [-DECAY_SCALE * jax.nn.sigmoid(w0_ref[:, gsl[g]] + _dot(dw, wl_ref[:, gsl[g]])) for g in gr]
    a = [jax.nn.sigmoid(a0_ref[:, gsl[g]] + _dot(da, al_ref[:, gsl[g]])) for g in gr]
    kap = [k[g] * kk_ref[:, gsl[g]] for g in gr]
    k = [k[g] * (1.0 + (a[g] - 1.0) * ka_ref[:, gsl[g]]) for g in gr]
    sums = _head_sums([kap[g] * kap[g] for g in gr] + [r[g] * k[g] * rk_ref[:, gsl[g]] for g in gr], seg)
    kap = [kap[g] / jnp.maximum(jnp.sqrt(sums[g]), 1e-12) for g in gr]
    bonus = sums[RW_GROUPS:]
    bet = [kap[g] * a[g] for g in gr]
    tri = (lax.broadcasted_iota(jnp.int32, (c, c), 0) >= lax.broadcasted_iota(jnp.int32, (c, c), 1)).astype(BF16)
    cum = [_cumsum_rows(lw[g], tri) for g in gr]

    y, h_new = [], []
    for lo in range(0, RW_GROUPS, RW_STAGE):
        sl = slice(lo, lo + RW_STAGE)
        ys, hs = _rwkv_groups(r[sl], lw[sl], cum[sl], k[sl], v[sl], kap[sl], bet[sl],
                              [h_scr[g] for g in range(lo, lo + RW_STAGE)], masks)
        y += ys
        h_new += hs
    for g in gr:
        h_scr[g] = h_new[g]

    ysum = _head_sums(y, seg)
    dlt = [y[g] - ysum[g] * (1.0 / HEAD) for g in gr]
    dsum = _head_sums([dlt[g] * dlt[g] for g in gr], seg)
    for g in gr:
        rstd = lax.rsqrt(dsum[g] * (1.0 / HEAD) + GN_EPS)
        yn = dlt[g] * rstd * gng_ref[:, gsl[g]] + gnb_ref[:, gsl[g]] + bonus[g] * v[g]
        o_ref[:, gsl[g]] = (yn * jax.nn.silu(lerp(3 * D, g))).astype(BF16)


def _rwkv_mix(p, p_lora, mu, mu_l, w0, a0, w_lora, a_lora, k_k, k_a, r_k, gn_g, gn_b, side=None):
    n_tok = p.shape[0]
    c = RW_CHUNK
    nc = SEQ // c
    ncol = p.shape[1]
    nl = p_lora.shape[1]
    row = lambda b, ci: (b * nc + ci, 0)
    prev = lambda b, ci: (jnp.maximum((b * nc + ci) * (c // SUBLANES) - 1, 0), 0)
    vec = pl.BlockSpec((1, D), lambda b, ci: (0, 0))
    lora = pl.BlockSpec((RW_LP // 2, D), lambda b, ci: (0, 0))
    in_specs = [pl.BlockSpec((c, ncol), row),
                pl.BlockSpec((SUBLANES, ncol), prev),
                pl.BlockSpec((c, nl), row),
                pl.BlockSpec((SUBLANES, nl), prev),
                pl.BlockSpec((1, ncol), lambda b, ci: (0, 0)),
                pl.BlockSpec((1, nl), lambda b, ci: (0, 0)),
                vec, vec, lora, lora,
                vec, vec, vec, vec, vec]
    args = [p, p, p_lora, p_lora, mu, mu_l, w0, a0, w_lora, a_lora, k_k, k_a, r_k, gn_g, gn_b]
    out_shape = jax.ShapeDtypeStruct((n_tok, D), BF16)
    out_specs = pl.BlockSpec((c, D), row)
    if side is not None:
        side_w, sl = side
        steps = (n_tok // SEQ) * nc
        _, srows, scols = side_w.shape
        srow = srows // steps
        assert srows % steps == 0 and srow % 16 == 0
        in_specs.append(pl.BlockSpec((None, srow, scols), lambda b, ci: (sl, b * nc + ci, 0)))
        args.append(side_w)
        out_shape = (out_shape, jax.ShapeDtypeStruct((srows, scols), BF16))
        out_specs = (out_specs, pl.BlockSpec((srow, scols), row))
    return pl.pallas_call(
        _rwkv_mix_kernel,
        out_shape=out_shape,
        grid=(n_tok // SEQ, nc),
        in_specs=in_specs,
        out_specs=out_specs,
        scratch_shapes=[pltpu.VMEM((RW_GROUPS, HEAD, RW_GW), F32)],
        compiler_params=_cparams("parallel", "arbitrary"),
        name="rwkv_mix",
    )(*args)


def kernel(x, c, positions, norm_g, mod_w, mod_b, final_norm_g, sg_w_in, sg_w_out, sg_ln_g, sg_ln_b, sg_w_spatial,
           sg_b_spatial, swa_w_in, swa_w_out, swa_sinks, rwkv_w_in, rwkv_w_out, rwkv_mu, rwkv_w0, rwkv_w_lora,
           rwkv_a0, rwkv_a_lora, rwkv_k_k, rwkv_k_a, rwkv_r_k, rwkv_gn_g, rwkv_gn_b):
    batch, seq, d = x.shape
    assert (seq, d) == (SEQ, D) and norm_g.shape[0] == DEPTH
    n_tok = batch * seq
    x2 = x.reshape(n_tok, D)
    c8 = jnp.zeros((SUBLANES, D), F32).at[:batch].set(c)
    pos2 = positions.reshape(n_tok, 1)
    half = HEAD // 2
    inv_freq = ROPE_THETA ** (-jnp.arange(half, dtype=F32) / half)
    invf = jnp.tile(inv_freq, LANES // half).reshape(1, LANES)
    alone = min(i + 1 for i in range(DEPTH) if i % 3 != 0)
    assert all(i % 3 != 0 for i in range(alone - 1, DEPTH - 1))
    mod, cos_t, sin_t = _modulation(c8, mod_w, mod_b, pos2, invf, alone)
    c_lanes = jnp.broadcast_to(c[:, :, None], (batch, D, LANES))
    mod_b3 = mod_b.reshape(DEPTH, 1, 3 * D)
    mod8 = None
    row = lambda a: a.reshape(1, -1)

    def act_rows(spans):
        rows = [jnp.concatenate([jnp.full((w,), val[t], F32) for w, *val in spans]) for t in range(3)]
        return jnp.zeros((SUBLANES, rows[0].shape[0]), F32).at[:3].set(jnp.stack(rows))

    gelu_span = (GELU_A, GELU_B, 1.0)
    silu_span = (0.5, 0.0, 1.0)
    sg_first_bf = sg_w_in[:1].astype(BF16)
    assert n_tok // SWA_BLOCK * LANES == 4 * D
    w_out = (sg_w_out, swa_w_out, rwkv_w_out)
    side_bf = None
    for i in range(DEPTH):
        kind, j = i % 3, i // 3
        mod3 = (mod[i] if i < alone else mod8)[:batch].reshape(batch, 1, 3 * D)
        g = row(norm_g[i])
        has_next = i + 1 < DEPTH
        final_g = None if has_next else row(final_norm_g)
        if kind == 0:
            act = act_rows([(2 * D, *gelu_span), (D, *silu_span)])
            w = sg_first_bf if i == 0 else side_bf[None]
            p = _in_proj_act(x2, g, mod3, w, 0, 3 * D, act, False, BF16, (swa_w_in, j) if has_next else None)
            p, side_bf = p if has_next else (p, None)
            x2 = _sg_out(p, row(sg_ln_g[j]), row(sg_ln_b[j]), sg_w_spatial[j],
                         jnp.broadcast_to(sg_b_spatial[j][:, :, None], (SG_GROUPS, SG_CHUNK, LANES)),
                         sg_w_out, j, x2, mod3, final_g)
            continue
        elif kind == 1:
            act = act_rows([(D + 2 * SWA_KVW, 0.0, 0.0, 0.0), (D, *silu_span)])
            p = _in_proj_act(x2, g, mod3, side_bf[None], 0, 2 * D + 2 * SWA_KVW, act, True, F32)
            mix = _swa_mix(p, cos_t, sin_t, swa_sinks[j],
                           (jnp.swapaxes(rwkv_w_in, 1, 2), (i + 1) // 3) if has_next else None)
            mix, side_bf = mix if has_next else (mix, None)
        else:
            lp = RW_LP // 2 - RW_LORA
            w_in = rwkv_w_in[j, :, 4 * D:].astype(BF16)
            w_l = jnp.concatenate([jnp.pad(w_in[:, :RW_LORA], ((0, 0), (0, lp))),
                                   jnp.pad(w_in[:, RW_LORA:], ((0, 0), (0, lp)))], axis=1)
            p, p_lora = _in_proj(x2, g, mod3, side_bf[None], 0, 4 * D, w_l)
            mu = rwkv_mu[j]
            mu_l = jnp.concatenate([jnp.pad(mu[4 * D:4 * D + RW_LORA], (0, lp)), jnp.pad(mu[4 * D + RW_LORA:], (0, lp))])
            wl_pad = jnp.pad(rwkv_w_lora[j], ((0, lp), (0, 0))).astype(BF16)
            al_pad = jnp.pad(rwkv_a_lora[j], ((0, lp), (0, 0))).astype(BF16)
            mix = _rwkv_mix(p, p_lora, row(mu[:4 * D]), row(mu_l), row(rwkv_w0[j]),
                            row(rwkv_a0[j]), wl_pad, al_pad, row(rwkv_k_k[j]),
                            row(rwkv_k_a[j]), row(rwkv_r_k[j]), row(rwkv_gn_g[j]), row(rwkv_gn_b[j]),
                            (sg_w_in, j + 1) if has_next else None)
            mix, side_bf = mix if has_next else (mix, None)
        x2 = _out_proj(mix, w_out[kind], j, x2, mod3, final_g, (c_lanes, mod_w, mod_b3, i + 1) if has_next else None)
        x2, mod8 = x2 if has_next else (x2, None)
    return x2.reshape(batch, seq, D)
```
